```python
import jax, jax.numpy as jnp
from jax import lax
import numpy as np

D_MODEL = 2048
BATCH = 2
SEQ = 16384
DEPTH = 2

GRID_W = 64
CTX_LEN = 256
D_MIX = D_MODEL
RW_HEADS = 8
RW_HEAD = 64
RW_WIDTH = RW_HEADS * RW_HEAD
W_RANK = 64
A_RANK = 64
G_RANK = 128
CONV_W = 3
GN_EPS = 64e-5
FT_GROUPS = 8
FT_GROUP = 64
FT_WIDTH = FT_GROUPS * FT_GROUP
MLA_HEADS = 16
QK_NOPE = 64
QK_ROPE = 32
V_HEAD = 64
MLA_WIDTH = MLA_HEADS * V_HEAD
Q_RANK = 512
KV_RANK = 256
ROPE_AXIS = QK_ROPE // 2
ROPE_THETA = 10000.0
Q_BLOCK = 128
ATTN_SCALE = (QK_NOPE + QK_ROPE) ** -0.5
D_FF = 5632
N_EXPERTS = 8
TOP_K = 2
D_FF_EXPERT = 7168
MOE_BLOCK = 512
N_DENSE = (DEPTH + 1) // 2
N_MOE = DEPTH // 2
DEEPNORM_ALPHA = (2 * DEPTH) ** 0.25
DEEPNORM_BETA = (8 * DEPTH) ** -0.25
LN_EPS = 1e-5
MOD_EPS = 1e-6
RMS_EPS = 1e-6

IN_SPLITS = (3 * RW_WIDTH, G_RANK, W_RANK, W_RANK, A_RANK, A_RANK, FT_WIDTH, Q_RANK, KV_RANK, QK_ROPE)
IN_COLS = sum(IN_SPLITS)
IN_OFFSETS = tuple(int(o) for o in np.cumsum(IN_SPLITS)[:-1])

kernel_name = "hybrid_rwkv7_fnet_mla_moe_dit_block"


def _standardize(x, eps):
    xf = x.astype(jnp.float32)
    mu = xf.mean(-1, keepdims=True)
    var = jnp.square(xf - mu).mean(-1, keepdims=True)
    return (xf - mu) * lax.rsqrt(var + eps)


def layer_norm(x, w, b):
    return (_standardize(x, LN_EPS) * w + b).astype(x.dtype)


def modulate(x, shift, scale):
    return (_standardize(x, MOD_EPS) * (1.0 + scale) + shift).astype(x.dtype)


def rms_norm(x, w):
    xf = x.astype(jnp.float32)
    y = xf * lax.rsqrt(jnp.mean(xf * xf, -1, keepdims=True) + RMS_EPS)
    return (y * w).astype(x.dtype)


def axial_rope_angles(T):
    rows = T // GRID_W
    row = jnp.repeat(jnp.arange(rows), GRID_W).astype(jnp.float32)
    col = (jnp.arange(T) % GRID_W).astype(jnp.float32)
    inv = ROPE_THETA ** (-jnp.arange(0, ROPE_AXIS, 2, dtype=jnp.float32) / ROPE_AXIS)
    ang = jnp.stack([row[:, None] * inv, col[:, None] * inv], axis=1)
    return jnp.cos(ang), jnp.sin(ang)


def apply_axial_rope(x, cos, sin):
    xs = x.astype(jnp.float32).reshape(*x.shape[:-1], 2, 2, ROPE_AXIS // 2)
    x0, x1 = xs[..., 0, :], xs[..., 1, :]
    out = jnp.stack([x0 * cos - x1 * sin, x0 * sin + x1 * cos], axis=-2)
    return out.reshape(x.shape).astype(x.dtype)


def short_conv(x, w):
    xp = jnp.pad(x, ((0, 0), (1, 1), (0, 0)))
    return xp[:, :-2] * w[0] + xp[:, 1:-1] * w[1] + xp[:, 2:] * w[2]


def rwkv_inputs(rkv, w_dn, a_dn, p):
    B, T, _ = rkv.shape
    heads = lambda t: t.reshape(B, T, RW_HEADS, RW_HEAD)
    r, k, v = jnp.split(short_conv(rkv, p["rw_conv"]), 3, axis=-1)
    kk = heads(k * p["rw_k_k"]).astype(jnp.float32)
    kk = kk / jnp.maximum(jnp.sqrt(jnp.sum(kk * kk, -1, keepdims=True)), 1e-12)
    dirs = []
    for d in range(2):
        logw = -jax.nn.softplus(-(p["rw_w0"][d] + jnp.tanh(w_dn[d]) @ p["rw_w_up"][d])) - 0.5
        decay = jnp.exp(-jnp.exp(logw.astype(jnp.float32)))
        a = jax.nn.sigmoid(p["rw_a0"][d] + a_dn[d] @ p["rw_a_up"][d])
        k_rep = k * (1.0 + (a - 1.0) * p["rw_k_a"])
        dirs.append((heads(decay), heads(a), heads(k_rep)))
    return heads(r), heads(v), kk, dirs


def wkv7_scan(S0, dir_inputs, kk, v, r, reverse, emit):
    decay, a, k_rep = dir_inputs
    seqs = (decay, a, k_rep, kk, v) + ((r,) if emit else ())
    xs = tuple(jnp.moveaxis(t.astype(jnp.float32), 1, 0) for t in seqs)

    def step(S, inp):
        w_t, a_t, k_t, kk_t, v_t = inp[:5]
        sa = jnp.einsum('bhvk,bhk->bhv', S, kk_t)
        S = (S * w_t[:, :, None, :] - sa[..., None] * (kk_t * a_t)[:, :, None, :]
             + v_t[..., None] * k_t[:, :, None, :])
        y = jnp.einsum('bhvk,bhk->bhv', S, inp[5]) if emit else None
        return S, y

    S_T, ys = lax.scan(step, S0, xs, reverse=reverse)
    return (jnp.moveaxis(ys, 0, 1) if emit else None), S_T


def rwkv_readout(y_sum, r, v, k_reps, g_dn, p):
    B, T = r.shape[:2]
    y = _standardize(y_sum, GN_EPS) * p["rw_gn_w"].reshape(RW_HEADS, RW_HEAD) + p["rw_gn_b"].reshape(RW_HEADS, RW_HEAD)
    bonus = sum(jnp.sum(r * kr * p["rw_r_k"], -1, keepdims=True) * v for kr in k_reps)
    gate = jax.nn.sigmoid(g_dn) @ p["rw_g_up"]
    return ((y.astype(r.dtype) + bonus).reshape(B, T, RW_WIDTH) * gate).astype(r.dtype)


def fourier_mixer(u):
    B, T, _ = u.shape
    z = jnp.fft.fft2(u.astype(jnp.float32).reshape(B, T, FT_GROUPS, FT_GROUP), axes=(1, 3), norm="ortho")
    return z.real.reshape(B, T, FT_WIDTH).astype(u.dtype)


def mla_queries(c_q, p, rope):
    B, T, _ = c_q.shape
    q = (rms_norm(c_q, p["mla_q_norm"]) @ p["mla_w_uq"]).reshape(B, T, MLA_HEADS, QK_NOPE + QK_ROPE)
    q_nope, q_rope = q[..., :QK_NOPE], q[..., QK_NOPE:]
    if rope is not None:
        cos, sin = rope
        q_rope = apply_axial_rope(q_rope, cos[:, None], sin[:, None])
    return q_nope, q_rope


def mla_keys(c_kv, k_rope, p, rope):
    B, T, _ = c_kv.shape
    kv = (rms_norm(c_kv, p["mla_kv_norm"]) @ p["mla_w_ukv"]).reshape(B, T, MLA_HEADS, QK_NOPE + V_HEAD)
    k_nope, v = kv[..., :QK_NOPE], kv[..., QK_NOPE:]
    if rope is not None:
        cos, sin = rope
        k_rope = apply_axial_rope(k_rope, cos, sin)
    return k_nope, k_rope, v


def mla_attend(q_nope, q_rope, k_nope, k_rope, v):
    B, T = q_nope.shape[:2]
    nb = T // Q_BLOCK
    qn = jnp.moveaxis(q_nope.reshape(B, nb, Q_BLOCK, MLA_HEADS, QK_NOPE), 1, 0)
    qr = jnp.moveaxis(q_rope.reshape(B, nb, Q_BLOCK, MLA_HEADS, QK_ROPE), 1, 0)

    def block(args):
        qn_b, qr_b = args
        s = jnp.einsum('bqhd,bkhd->bhqk', qn_b, k_nope) + jnp.einsum('bqhd,bkd->bhqk', qr_b, k_rope)
        prob = jax.nn.softmax(s.astype(jnp.float32) * ATTN_SCALE, axis=-1).astype(v.dtype)
        return jnp.einsum('bhqk,bkhd->bqhd', prob, v)

    o = lax.map(block, (qn, qr))
    return jnp.moveaxis(o, 0, 1).reshape(B, T, MLA_WIDTH)


def hybrid_mixer(h_lat, h_ctx, p, rope, emit_ctx):
    B = h_lat.shape[0]
    pl = jnp.split(h_lat @ p["w_in"], IN_OFFSETS, axis=-1)
    pc = jnp.split(h_ctx @ p["w_in"], IN_OFFSETS, axis=-1)

    rl, vl, kkl, dl = rwkv_inputs(pl[0], pl[2:4], pl[4:6], p)
    rc, vc, kkc, dc = rwkv_inputs(pc[0], pc[2:4], pc[4:6], p)
    S0 = jnp.zeros((B, RW_HEADS, RW_HEAD, RW_HEAD), jnp.float32)
    yc_f, Sc_f = wkv7_scan(S0, dc[0], kkc, vc, rc, False, emit_ctx)
    yc_b, Sc_b = wkv7_scan(S0, dc[1], kkc, vc, rc, True, emit_ctx)
    yl_f, _ = wkv7_scan(Sc_f, dl[0], kkl, vl, rl, False, True)
    yl_b, _ = wkv7_scan(Sc_b, dl[1], kkl, vl, rl, True, True)
    rw_lat = rwkv_readout(yl_f + yl_b, rl, vl, (dl[0][2], dl[1][2]), pl[1], p)

    ft_lat = fourier_mixer(pl[6])

    qn_l, qr_l = mla_queries(pl[7], p, rope)
    kn_l, kr_l, v_l = mla_keys(pl[8], pl[9], p, rope)
    kn_c, kr_c, v_c = mla_keys(pc[8], pc[9], p, None)
    att_lat = mla_attend(qn_l, qr_l, jnp.concatenate([kn_l, kn_c], 1),
                         jnp.concatenate([kr_l, kr_c], 1), jnp.concatenate([v_l, v_c], 1))
    out_lat = jnp.concatenate([rw_lat, ft_lat, att_lat], axis=-1) @ p["w_out"]

    out_ctx = None
    if emit_ctx:
        rw_ctx = rwkv_readout(yc_f + yc_b, rc, vc, (dc[0][2], dc[1][2]), pc[1], p)
        ft_ctx = fourier_mixer(pc[6])
        qn_c, qr_c = mla_queries(pc[7], p, None)
        att_ctx = mla_attend(qn_c, qr_c, kn_c, kr_c, v_c)
        out_ctx = jnp.concatenate([rw_ctx, ft_ctx, att_ctx], axis=-1) @ p["w_out"]
    return out_lat, out_ctx


def swiglu(x, w1, w3, w2):
    return (jax.nn.silu(x @ w1) * (x @ w3)) @ w2


def moe_swiglu(x, router, w1, w3, w2):
    B, T, D = x.shape
    N = B * T
    xt = x.reshape(N, D)
    top_v, top_i = lax.top_k((xt @ router).astype(jnp.float32), TOP_K)
    gates = jax.nn.softmax(top_v, axis=-1).astype(x.dtype)
    e_flat = top_i.reshape(-1)
    tok_flat = jnp.repeat(jnp.arange(N, dtype=jnp.int32), TOP_K)
    g_flat = gates.reshape(-1)
    order = jnp.argsort(e_flat)
    e_s, tok_s, g_s = e_flat[order], tok_flat[order], g_flat[order]
    counts = jnp.bincount(e_flat, length=N_EXPERTS)
    starts = jnp.cumsum(counts) - counts
    padded = (counts + MOE_BLOCK - 1) // MOE_BLOCK * MOE_BLOCK
    p_ends = jnp.cumsum(padded)
    p_starts = p_ends - padded
    dest = p_starts[e_s] + (jnp.arange(N * TOP_K) - starts[e_s])
    P = -(-(N * TOP_K) // MOE_BLOCK) * MOE_BLOCK + N_EXPERTS * MOE_BLOCK
    nblk = P // MOE_BLOCK
    slot_tok = jnp.zeros((P,), jnp.int32).at[dest].set(tok_s)
    slot_gate = jnp.zeros((P,), x.dtype).at[dest].set(g_s)
    blk_expert = jnp.minimum(jnp.searchsorted(p_ends, jnp.arange(nblk) * MOE_BLOCK, side='right'), N_EXPERTS - 1)

    def run(args):
        e, toks = args
        xb = xt[toks]
        return swiglu(xb, w1[e], w3[e], w2[e])

    yb = lax.map(run, (blk_expert, slot_tok.reshape(nblk, MOE_BLOCK))).reshape(P, D)
    y = jnp.zeros((N, D), x.dtype).at[slot_tok].add(yb * slot_gate[:, None])
    return y.reshape(B, T, D)


def channel_mixer(t, l, ffn_w1, ffn_w3, ffn_w2, moe_router, moe_w1, moe_w3, moe_w2):
    i = l // 2
    if l % 2 == 0:
        return swiglu(t, ffn_w1[i], ffn_w3[i], ffn_w2[i])
    return moe_swiglu(t, moe_router[i], moe_w1[i], moe_w3[i], moe_w2[i])


def setup_inputs(seed: int = 0) -> dict:
    key = jax.random.key(seed)
    ks = iter(jax.random.split(key, 48))
    f32 = jnp.float32
    nrm = lambda shape, scale: jax.random.normal(next(ks), shape, f32) * scale
    L = DEPTH
    return {
        "x": nrm((BATCH, SEQ, D_MODEL), 1.0),
        "c": nrm((BATCH, D_MODEL), 1.0),
        "ctx": nrm((BATCH, CTX_LEN, D_MODEL), 1.0),
        "c_ctx": nrm((D_MODEL,), 1.0),
        "ada_w": nrm((L, D_MODEL, 6 * D_MODEL), D_MODEL ** -0.5),
        "ada_b": nrm((L, 6 * D_MODEL), 0.02),
        "w_in": nrm((L, D_MODEL, IN_COLS), D_MODEL ** -0.5),
        "rw_conv": nrm((L, CONV_W, 3 * RW_WIDTH), 0.2) + jnp.array([0.0, 1.0, 0.0], f32)[None, :, None],
        "rw_w0": jax.random.uniform(next(ks), (L, 2, RW_WIDTH), f32, -5.0, 0.0),
        "rw_w_up": nrm((L, 2, W_RANK, RW_WIDTH), 0.1),
        "rw_a0": nrm((L, 2, RW_WIDTH), 0.5),
        "rw_a_up": nrm((L, 2, A_RANK, RW_WIDTH), 0.1),
        "rw_g_up": nrm((L, G_RANK, RW_WIDTH), G_RANK ** -0.5),
        "rw_k_k": 0.85 + nrm((L, RW_WIDTH), 0.05),
        "rw_k_a": 1.0 + nrm((L, RW_WIDTH), 0.05),
        "rw_r_k": nrm((L, RW_HEADS, RW_HEAD), 0.1),
        "rw_gn_w": 1.0 + nrm((L, RW_WIDTH), 0.05),
        "rw_gn_b": nrm((L, RW_WIDTH), 0.02),
        "mla_q_norm": 1.0 + nrm((L, Q_RANK), 0.05),
        "mla_w_uq": nrm((L, Q_RANK, MLA_HEADS * (QK_NOPE + QK_ROPE)), Q_RANK ** -0.5),
        "mla_kv_norm": 1.0 + nrm((L, KV_RANK), 0.05),
        "mla_w_ukv": nrm((L, KV_RANK, MLA_HEADS * (QK_NOPE + V_HEAD)), KV_RANK ** -0.5),
        "w_out": nrm((L, D_MIX, D_MODEL), DEEPNORM_BETA * D_MIX ** -0.5),
        "ln1_w": 1.0 + nrm((L, D_MODEL), 0.05),
        "ln1_b": nrm((L, D_MODEL), 0.02),
        "ln2_w": 1.0 + nrm((L, D_MODEL), 0.05),
        "ln2_b": nrm((L, D_MODEL), 0.02),
        "ffn_w1": nrm((N_DENSE, D_MODEL, D_FF), D_MODEL ** -0.5),
        "ffn_w3": nrm((N_DENSE, D_MODEL, D_FF), D_MODEL ** -0.5),
        "ffn_w2": nrm((N_DENSE, D_FF, D_MODEL), DEEPNORM_BETA * D_FF ** -0.5),
        "moe_router": nrm((N_MOE, D_MODEL, N_EXPERTS), D_MODEL ** -0.5),
        "moe_w1": nrm((N_MOE, N_EXPERTS, D_MODEL, D_FF_EXPERT), D_MODEL ** -0.5),
        "moe_w3": nrm((N_MOE, N_EXPERTS, D_MODEL, D_FF_EXPERT), D_MODEL ** -0.5),
        "moe_w2": nrm((N_MOE, N_EXPERTS, D_FF_EXPERT, D_MODEL), DEEPNORM_BETA * D_FF_EXPERT ** -0.5),
    }


def reference(x, c, ctx, c_ctx, ada_w, ada_b, w_in, rw_conv, rw_w0, rw_w_up, rw_a0, rw_a_up, rw_g_up,
              rw_k_k, rw_k_a, rw_r_k, rw_gn_w, rw_gn_b, mla_q_norm, mla_w_uq, mla_kv_norm, mla_w_ukv,
              w_out, ln1_w, ln1_b, ln2_w, ln2_b, ffn_w1, ffn_w3, ffn_w2, moe_router, moe_w1, moe_w3, moe_w2):
    T = x.shape[1]
    rope = axial_rope_angles(T)
    silu_c = jax.nn.silu(c)
    silu_cc = jax.nn.silu(c_ctx)
    for l in range(DEPTH):
        last = l == DEPTH - 1
        p = {"w_in": w_in[l], "rw_conv": rw_conv[l], "rw_w0": rw_w0[l], "rw_w_up": rw_w_up[l],
             "rw_a0": rw_a0[l], "rw_a_up": rw_a_up[l], "rw_g_up": rw_g_up[l], "rw_k_k": rw_k_k[l],
             "rw_k_a": rw_k_a[l], "rw_r_k": rw_r_k[l], "rw_gn_w": rw_gn_w[l], "rw_gn_b": rw_gn_b[l],
             "mla_q_norm": mla_q_norm[l], "mla_w_uq": mla_w_uq[l], "mla_kv_norm": mla_kv_norm[l],
             "mla_w_ukv": mla_w_ukv[l], "w_out": w_out[l]}
        sh_m, sc_m, g_m, sh_f, sc_f, g_f = [m[:, None, :] for m in jnp.split(silu_c @ ada_w[l] + ada_b[l], 6, axis=-1)]
        csh_m, csc_m, cg_m, csh_f, csc_f, cg_f = jnp.split(silu_cc @ ada_w[l] + ada_b[l], 6, axis=-1)

        h_lat = modulate(x, sh_m, sc_m)
        h_ctx = modulate(ctx, csh_m, csc_m)
        mix_lat, mix_ctx = hybrid_mixer(h_lat, h_ctx, p, rope, not last)
        x = layer_norm(DEEPNORM_ALPHA * x + g_m * mix_lat, ln1_w[l], ln1_b[l])
        f_lat = channel_mixer(modulate(x, sh_f, sc_f), l, ffn_w1, ffn_w3, ffn_w2, moe_router, moe_w1, moe_w3, moe_w2)
        x = layer_norm(DEEPNORM_ALPHA * x + g_f * f_lat, ln2_w[l], ln2_b[l])
        if not last:
            ctx = layer_norm(DEEPNORM_ALPHA * ctx + cg_m * mix_ctx, ln1_w[l], ln1_b[l])
            f_ctx = channel_mixer(modulate(ctx, csh_f, csc_f), l, ffn_w1, ffn_w3, ffn_w2, moe_router, moe_w1, moe_w3, moe_w2)
            ctx = layer_norm(DEEPNORM_ALPHA * ctx + cg_f * f_ctx, ln2_w[l], ln2_b[l])
    return x
```

```python
import functools
import math

import numpy as np
import jax
import jax.numpy as jnp
from jax import lax
from jax.experimental import pallas as pl
from jax.experimental.pallas import tpu as pltpu

F32 = jnp.float32
BF16 = jnp.bfloat16
HI = lax.Precision.HIGHEST

LANES = 128
SUBLANES = 8
VMEM_LIMIT = 56 * 1024 * 1024

GRID_W = 64
RW_HEADS = 8
RW_HEAD = 64
RW_WIDTH = RW_HEADS * RW_HEAD
G_RANK = 128
LORA_RANK = 64
GN_EPS = 64e-5
FT_GROUP = 64
FT_WIDTH = 512
MLA_HEADS = 16
QK_NOPE = 64
QK_ROPE = 32
V_HEAD = 64
MLA_WIDTH = MLA_HEADS * V_HEAD
Q_RANK = 512
KV_RANK = 256
ROPE_AXIS = QK_ROPE // 2
ROPE_THETA = 10000.0
ATTN_SCALE = (QK_NOPE + QK_ROPE) ** -0.5
N_EXPERTS = 8
LN_EPS = 1e-5
MOD_EPS = 1e-6
RMS_EPS = 1e-6
CHUNK = 64
INV_BLOCK = 16
MOE_ROWS = 1024

NT = (((1,), (1,)), ((), ()))
TN = (((0,), (0,)), ((), ()))


def _cparams(*sem):
    return pltpu.CompilerParams(dimension_semantics=sem, vmem_limit_bytes=VMEM_LIMIT)


def _dot(a, b, prec=None):
    return jnp.dot(a, b, precision=prec, preferred_element_type=F32)


def _standardize(x, eps):
    mu = jnp.mean(x, axis=-1, keepdims=True)
    xc = x - mu
    var = jnp.mean(xc * xc, axis=-1, keepdims=True)
    return xc * lax.rsqrt(var + eps)


def _tile(n, pref):
    t = min(n, pref)
    assert n % t == 0, (n, pref)
    return t


def _ada_kernel(c_ref, w_ref, b_ref, o_ref):
    c = c_ref[...]
    s = c * jax.nn.sigmoid(c)
    o_ref[0] = _dot(s, w_ref[0], HI) + b_ref[0]


def ada_vectors(cc, ada_w, ada_b):
    L, D, N6 = ada_w.shape
    tn = _tile(N6, 1024)
    return pl.pallas_call(
        _ada_kernel,
        grid=(L, N6 // tn),
        in_specs=[pl.BlockSpec((SUBLANES, D), lambda l, j: (0, 0)),
                  pl.BlockSpec((1, D, tn), lambda l, j: (l, 0, j)),
                  pl.BlockSpec((1, 1, tn), lambda l, j: (l, 0, j))],
        out_specs=pl.BlockSpec((1, SUBLANES, tn), lambda l, j: (l, 0, j)),
        out_shape=jax.ShapeDtypeStruct((L, SUBLANES, N6), F32),
        compiler_params=_cparams("parallel", "parallel"),
        name="ada_vectors",
    )(cc, ada_w, ada_b.reshape(L, 1, N6))


def _inproj_kernel(x_ref, sh_ref, sc_ref, *refs):
    nw = len(refs) // 2
    h = _standardize(x_ref[0], MOD_EPS) * (1.0 + sc_ref[0]) + sh_ref[0]
    hb = h.astype(BF16)
    for w_ref, o_ref in zip(refs[:nw], refs[nw:]):
        o_ref[0] = _dot(hb, w_ref[...])


def in_projection(x, shift, scale, weights):
    B, T, D = x.shape
    tm = _tile(T, 256)
    vec = pl.BlockSpec((1, 1, D), lambda b, i: (b, 0, 0))
    return pl.pallas_call(
        _inproj_kernel,
        grid=(B, T // tm),
        in_specs=[pl.BlockSpec((1, tm, D), lambda b, i: (b, i, 0)), vec, vec]
        + [pl.BlockSpec(w.shape, lambda b, i: (0, 0)) for w in weights],
        out_specs=[pl.BlockSpec((1, tm, w.shape[1]), lambda b, i: (b, i, 0)) for w in weights],
        out_shape=[jax.ShapeDtypeStruct((B, T, w.shape[1]), F32) for w in weights],
        compiler_params=_cparams("parallel", "parallel"),
        name="in_projection",
    )(x, shift, scale, *weights)


RW_COLS = 3 * RW_WIDTH + G_RANK + 4 * LANES


def _softplus(z):
    return jnp.maximum(z, 0.0) + jnp.log(1.0 + jnp.exp(-jnp.abs(z)))


def _rwkv_prep_kernel(x_ref, xp_ref, xn_ref, conv_ref, kk_ref, ka_ref, rk_ref, w0_ref, a0_ref,
                      wup_ref, aup_ref, gup_ref, e_ref,
                      r_o, v_o, kk_o, ld0_o, b0_o, kr0_o, ld1_o, b1_o, kr1_o, bonus_o, gate_o):
    i = pl.program_id(1)
    n = pl.num_programs(1)
    x = x_ref[0]
    W3 = 3 * RW_WIDTH
    raw = x[:, :W3]
    tm = raw.shape[0]
    row = lax.broadcasted_iota(jnp.int32, (tm, 1), 0)
    prev_row = jnp.where(i > 0, xp_ref[0, SUBLANES - 1:SUBLANES, :], 0.0)
    next_row = jnp.where(i < n - 1, xn_ref[0, 0:1, :], 0.0)
    xm = jnp.where(row == 0, prev_row, pltpu.roll(raw, 1, 0))
    xq = jnp.where(row == tm - 1, next_row, pltpu.roll(raw, tm - 1, 0))
    cw = conv_ref[...]
    y = xm * cw[0:1] + raw * cw[1:2] + xq * cw[2:3]
    r = y[:, :RW_WIDTH]
    k = y[:, RW_WIDTH:2 * RW_WIDTH]
    v = y[:, 2 * RW_WIDTH:W3]
    E = e_ref[...]
    kkv = k * kk_ref[...]
    kk = kkv / jnp.maximum(jnp.sqrt(_dot(kkv * kkv, E, HI)), 1e-12)
    r_o[0] = r
    v_o[0] = v
    kk_o[0] = kk
    g_dn = x[:, W3:W3 + G_RANK]
    gate_o[0] = _dot(jax.nn.sigmoid(g_dn), gup_ref[...], HI)
    bonus = jnp.zeros_like(r)
    outs = ((ld0_o, b0_o, kr0_o), (ld1_o, b1_o, kr1_o))
    for d in range(2):
        base = W3 + G_RANK
        w_dn = x[:, base + d * LANES: base + (d + 1) * LANES]
        a_dn = x[:, base + (2 + d) * LANES: base + (3 + d) * LANES]
        z = w0_ref[d:d + 1, :] + _dot(jnp.tanh(w_dn), wup_ref[d], HI)
        logw = -_softplus(-z) - 0.5
        a = jax.nn.sigmoid(a0_ref[d:d + 1, :] + _dot(a_dn, aup_ref[d], HI))
        kr = k * (1.0 + (a - 1.0) * ka_ref[...])
        ld_o, b_o, kr_o = outs[d]
        ld_o[0] = -jnp.exp(logw)
        b_o[0] = a * kk
        kr_o[0] = kr
        bonus = bonus + _dot(r * kr * rk_ref[...], E, HI) * v
    bonus_o[0] = bonus


def rwkv_prep(rw, p):
    B, T, _ = rw.shape
    tm = _tile(T, 256)
    nh = tm // SUBLANES
    last = T // SUBLANES - 1
    W3 = 3 * RW_WIDTH
    full = lambda a: pl.BlockSpec(a.shape, lambda b, i: (0,) * a.ndim)
    params = [p["conv"], p["k_k"], p["k_a"], p["r_k"], p["w0"], p["a0"], p["w_up"], p["a_up"], p["g_up"], p["E"]]
    outs = pl.pallas_call(
        _rwkv_prep_kernel,
        grid=(B, T // tm),
        in_specs=[pl.BlockSpec((1, tm, RW_COLS), lambda b, i: (b, i, 0)),
                  pl.BlockSpec((1, SUBLANES, W3), lambda b, i: (b, jnp.maximum(i * nh - 1, 0), 0)),
                  pl.BlockSpec((1, SUBLANES, W3), lambda b, i: (b, jnp.minimum((i + 1) * nh, last), 0))]
        + [full(a) for a in params],
        out_specs=[pl.BlockSpec((1, tm, RW_WIDTH), lambda b, i: (b, i, 0))] * 11,
        out_shape=[jax.ShapeDtypeStruct((B, T, RW_WIDTH), F32)] * 11,
        compiler_params=_cparams("parallel", "parallel"),
        name="rwkv_prep",
    )(rw, rw, rw, *params)
    return outs


def _chunk_masks(reverse):
    t = lax.broadcasted_iota(jnp.int32, (CHUNK, CHUNK), 0)
    j = lax.broadcasted_iota(jnp.int32, (CHUNK, CHUNK), 1)
    strict = (j > t) if reverse else (j < t)
    incl = (j >= t) if reverse else (j <= t)
    blk = (t // INV_BLOCK) == (j // INV_BLOCK)
    eye = jnp.where(t == j, 1.0, 0.0).astype(F32)
    return strict, incl, blk, eye


def _chunk_math(ld, r, v, kk, b, kr, masks):
    strict, incl, blk, eye = masks
    tri = jnp.where(incl, 1.0, 0.0).astype(F32)
    Lc = _dot(tri, ld, HI)
    Lx = Lc - ld
    Lt = jnp.sum(ld, axis=0, keepdims=True)
    kh = kk * jnp.exp(Lx)
    rh = r * jnp.exp(Lc)
    ginv = jnp.exp(-Lc)
    bh = b * ginv
    kkh = kr * ginv
    gout = jnp.exp(Lt - Lc)
    Bb = b * gout
    Kb = kr * gout
    mm = lambda a, c: _dot(a, c, HI)
    nt = lambda a, c: lax.dot_general(a, c, NT, precision=HI, preferred_element_type=F32)
    tn = lambda a, c: lax.dot_general(a, c, TN, precision=HI, preferred_element_type=F32)
    Mab = jnp.where(strict, nt(kh, bh), 0.0)
    Mak = jnp.where(strict, nt(kh, kkh), 0.0)
    Arb = jnp.where(incl, nt(rh, bh), 0.0)
    Ark = jnp.where(incl, nt(rh, kkh), 0.0)
    Nd = jnp.where(blk, Mab, 0.0)
    No = Mab - Nd
    N2 = mm(Nd, Nd)
    N4 = mm(N2, N2)
    N8 = mm(N4, N4)
    Td = mm(mm(mm(eye - Nd, eye + N2), eye + N4), eye + N8)
    M2 = mm(Td, No)
    Tm = mm(mm(eye - M2, eye + mm(M2, M2)), Td)
    P1 = mm(Tm, kh)
    P2 = mm(Tm, mm(Mak, v))
    Q1 = rh - mm(Arb, P1)
    Yi = mm(Ark, v) - mm(Arb, P2)
    G = eye * jnp.exp(Lt) - tn(Bb, P1)
    H = tn(Kb, v) - tn(Bb, P2)
    return Q1, Yi, G, H


def _rwkv_chunk_kernel(ld_ref, r_ref, v_ref, kk_ref, b_ref, kr_ref, q_o, yi_o, g_o, h_o, *, nch, reverse):
    masks = _chunk_masks(reverse)

    def body(idx, carry):
        h = idx // nch
        c = idx % nch
        sl = pl.ds(pl.multiple_of(c * CHUNK, CHUNK), CHUNK)
        get = lambda ref: ref[0, h, sl, :]
        Q1, Yi, G, H = _chunk_math(get(ld_ref), get(r_ref), get(v_ref), get(kk_ref), get(b_ref), get(kr_ref), masks)
        q_o[0, h, sl, :] = Q1
        yi_o[0, h, sl, :] = Yi
        g_o[0, h, sl, :] = G
        h_o[0, h, sl, :] = H
        return carry

    lax.fori_loop(0, RW_HEADS * nch, body, 0)


def rwkv_chunks(ld, r, v, kk, b, kr, reverse):
    B, H, T, K = ld.shape
    tt = _tile(T, 4 * CHUNK)
    spec = pl.BlockSpec((1, H, tt, K), lambda bb, i: (bb, 0, i, 0))
    return pl.pallas_call(
        functools.partial(_rwkv_chunk_kernel, nch=tt // CHUNK, reverse=reverse),
        grid=(B, T // tt),
        in_specs=[spec] * 6,
        out_specs=[spec] * 4,
        out_shape=[jax.ShapeDtypeStruct((B, H, T, K), F32)] * 4,
        compiler_params=_cparams("parallel", "parallel"),
        name="rwkv_chunks_bwd" if reverse else "rwkv_chunks_fwd",
    )(ld, r, v, kk, b, kr)


def _rwkv_seq_kernel(q_ref, yi_ref, g_ref, h_ref, s0_ref, y_o, sf_o, s_scr, *, nch, reverse):
    @pl.when(pl.program_id(1) == 0)
    def _():
        s_scr[...] = s0_ref[0]

    def body(cc, carry):
        c = (nch - 1 - cc) if reverse else cc
        sl = pl.ds(pl.multiple_of(c * CHUNK, CHUNK), CHUNK)
        S = s_scr[...]
        y = jnp.einsum("hck,hkv->hcv", q_ref[0, :, sl, :], S, precision=HI, preferred_element_type=F32)
        y_o[0, :, sl, :] = y + yi_ref[0, :, sl, :]
        s_scr[...] = jnp.einsum("hjk,hkv->hjv", g_ref[0, :, sl, :], S, precision=HI,
                                preferred_element_type=F32) + h_ref[0, :, sl, :]
        return carry

    lax.fori_loop(0, nch, body, 0)
    sf_o[0] = s_scr[...]


def rwkv_sequential(q1, yi, g, hm, s0, reverse):
    B, H, T, K = q1.shape
    tt = _tile(T, 8 * CHUNK)
    n = T // tt
    idx = (lambda bb, i: (bb, 0, n - 1 - i, 0)) if reverse else (lambda bb, i: (bb, 0, i, 0))
    spec = pl.BlockSpec((1, H, tt, K), idx)
    sspec = pl.BlockSpec((1, H, K, K), lambda bb, i: (bb, 0, 0, 0))
    return pl.pallas_call(
        functools.partial(_rwkv_seq_kernel, nch=tt // CHUNK, reverse=reverse),
        grid=(B, n),
        in_specs=[spec] * 4 + [sspec],
        out_specs=[spec, sspec],
        out_shape=[jax.ShapeDtypeStruct((B, H, T, K), F32), jax.ShapeDtypeStruct((B, H, K, K), F32)],
        scratch_shapes=[pltpu.VMEM((H, K, K), F32)],
        compiler_params=_cparams("parallel", "arbitrary"),
        name="rwkv_seq_bwd" if reverse else "rwkv_seq_fwd",
    )(q1, yi, g, hm, s0)


def _rwkv_readout_kernel(yf_ref, yb_ref, bonus_ref, gate_ref, gw_ref, gb_ref, e_ref, o_ref):
    Em = e_ref[...] * (1.0 / RW_HEAD)
    ys = yf_ref[0] + yb_ref[0]
    yc = ys - _dot(ys, Em, HI)
    var = _dot(yc * yc, Em, HI)
    yn = yc * lax.rsqrt(var + GN_EPS) * gw_ref[...] + gb_ref[...]
    o_ref[0] = (yn + bonus_ref[0]) * gate_ref[0]


def rwkv_readout(yf, yb, bonus, gate, p):
    B, T, W = yf.shape
    tm = _tile(T, 512)
    spec = pl.BlockSpec((1, tm, W), lambda b, i: (b, i, 0))
    full = lambda a: pl.BlockSpec(a.shape, lambda b, i: (0,) * a.ndim)
    return pl.pallas_call(
        _rwkv_readout_kernel,
        grid=(B, T // tm),
        in_specs=[spec] * 4 + [full(p["gn_w"]), full(p["gn_b"]), full(p["E"])],
        out_specs=spec,
        out_shape=jax.ShapeDtypeStruct((B, T, W), F32),
        compiler_params=_cparams("parallel", "parallel"),
        name="rwkv_readout",
    )(yf, yb, bonus, gate, p["gn_w"], p["gn_b"], p["E"])


def _to_heads(a):
    B, T, _ = a.shape
    return a.reshape(B, T, RW_HEADS, RW_HEAD).transpose(0, 2, 1, 3)


def _from_heads(a):
    B, H, T, K = a.shape
    return a.transpose(0, 2, 1, 3).reshape(B, T, H * K)


def rwkv_branch(rw_lat, rw_ctx, p, emit_ctx):
    prep_l = rwkv_prep(rw_lat, p)
    prep_c = rwkv_prep(rw_ctx, p)
    B = rw_lat.shape[0]
    s_zero = jnp.zeros((B, RW_HEADS, RW_HEAD, RW_HEAD), F32)

    def scans(prep, s0s):
        r, v, kk = (_to_heads(a) for a in prep[:3])
        ys, finals = [], []
        for d in range(2):
            ld, b, kr = (_to_heads(a) for a in prep[3 + 3 * d: 6 + 3 * d])
            q1, yi, g, hm = rwkv_chunks(ld, r, v, kk, b, kr, reverse=bool(d))
            y, sf = rwkv_sequential(q1, yi, g, hm, s0s[d], reverse=bool(d))
            ys.append(_from_heads(y))
            finals.append(sf)
        return ys, finals

    ys_c, fin_c = scans(prep_c, (s_zero, s_zero))
    ys_l, _ = scans(prep_l, fin_c)
    out_l = rwkv_readout(ys_l[0], ys_l[1], prep_l[9], prep_l[10], p)
    out_c = rwkv_readout(ys_c[0], ys_c[1], prep_c[9], prep_c[10], p) if emit_ctx else None
    return out_l, out_c


def _dft_mats(n):
    a = 2.0 * np.pi * np.outer(np.arange(n), np.arange(n)) / n
    return np.cos(a), np.sin(a)


def _fft1_kernel(u_ref, c_ref, s_ref, twc_ref, tws_ref, ar_o, ai_o, *, tn2, ch):
    U = u_ref[0]
    Ar = _dot(c_ref[...], U, HI)
    Ai = -_dot(s_ref[...], U, HI)
    twc = twc_ref[0]
    tws = tws_ref[0]
    for j in range(tn2):
        ct = twc[:, j:j + 1]
        st = tws[:, j:j + 1]
        a_r = Ar[:, j * ch:(j + 1) * ch]
        a_i = Ai[:, j * ch:(j + 1) * ch]
        ar_o[0, j] = a_r * ct + a_i * st
        ai_o[0, j] = a_i * ct - a_r * st


def _fft2_kernel(ar_ref, ai_ref, c_ref, s_ref, cc_ref, sc_ref, o_ref):
    Ar = ar_ref[0]
    Ai = ai_ref[0]
    C = c_ref[...]
    S = s_ref[...]
    Yr = _dot(C, Ar, HI) + _dot(S, Ai, HI)
    Yi = _dot(C, Ai, HI) - _dot(S, Ar, HI)
    Cc = cc_ref[...]
    Sc = sc_ref[...]
    for m in range(Ar.shape[1] // LANES):
        sl = slice(m * LANES, (m + 1) * LANES)
        o_ref[0, :, sl] = _dot(Yr[:, sl], Cc, HI) + _dot(Yi[:, sl], Sc, HI)


def fourier_mixer(u):
    B, T, ch = u.shape
    lg = int(round(math.log2(T)))
    assert 1 << lg == T
    N1 = 1 << ((lg + 1) // 2)
    N2 = T // N1
    c1, s1 = _dft_mats(N1)
    c2, s2 = _dft_mats(N2)
    tw = 2.0 * np.pi * np.outer(np.arange(N1), np.arange(N2)) / T
    tn2 = min(SUBLANES, N2)
    nj = N2 // tn2
    twc = np.cos(tw).reshape(N1, nj, tn2).transpose(1, 0, 2)
    tws = np.sin(tw).reshape(N1, nj, tn2).transpose(1, 0, 2)
    cg, sg = _dft_mats(FT_GROUP)
    scale = 1.0 / math.sqrt(T * FT_GROUP)
    eye2 = np.eye(LANES // FT_GROUP)
    cc = np.kron(eye2, cg) * scale
    sc = np.kron(eye2, sg) * scale
    f = lambda a: jnp.asarray(a, F32)
    full2 = lambda n, m: pl.BlockSpec((n, m), lambda b, j: (0, 0))

    ar, ai = pl.pallas_call(
        functools.partial(_fft1_kernel, tn2=tn2, ch=ch),
        grid=(B, nj),
        in_specs=[pl.BlockSpec((1, N1, tn2 * ch), lambda b, j: (b, 0, j)),
                  full2(N1, N1), full2(N1, N1),
                  pl.BlockSpec((1, N1, tn2), lambda b, j: (j, 0, 0)),
                  pl.BlockSpec((1, N1, tn2), lambda b, j: (j, 0, 0))],
        out_specs=[pl.BlockSpec((1, tn2, N1, ch), lambda b, j: (b, j, 0, 0))] * 2,
        out_shape=[jax.ShapeDtypeStruct((B, N2, N1, ch), F32)] * 2,
        compiler_params=_cparams("parallel", "parallel"),
        name="fft_stage1",
    )(u.reshape(B, N1, N2 * ch), f(c1), f(s1), f(twc), f(tws))

    tk1 = min(SUBLANES, N1)
    blk = pl.BlockSpec((1, N2, tk1 * ch), lambda b, j: (b, 0, j))
    out = pl.pallas_call(
        _fft2_kernel,
        grid=(B, N1 // tk1),
        in_specs=[blk, blk, full2(N2, N2), full2(N2, N2), full2(LANES, LANES), full2(LANES, LANES)],
        out_specs=blk,
        out_shape=jax.ShapeDtypeStruct((B, N2, N1 * ch), F32),
        compiler_params=_cparams("parallel", "parallel"),
        name="fft_stage2",
    )(ar.reshape(B, N2, N1 * ch), ai.reshape(B, N2, N1 * ch), f(c2), f(s2), f(cc), f(sc))
    return out.reshape(B, T, ch)


HEAD_SLAB = LANES
ROPE_SHIFT = HEAD_SLAB - QK_ROPE


def _rms(x, w):
    return x * lax.rsqrt(jnp.mean(x * x, axis=-1, keepdims=True) + RMS_EPS) * w


def _qproj_kernel(cq_ref, nw_ref, w_ref, ct_ref, st_ref, q_o):
    q = _dot(_rms(cq_ref[0], nw_ref[...]).astype(BF16), w_ref[...])
    ct = ct_ref[...]
    st = st_ref[...]
    for h in range(MLA_HEADS):
        sl = slice(h * HEAD_SLAB, (h + 1) * HEAD_SLAB)
        s = q[:, sl]
        q_o[0, :, sl] = ((s * ct + pltpu.roll(s, ROPE_SHIFT, 1) * st) * ATTN_SCALE).astype(BF16)


def _kvproj_kernel(ckv_ref, nw_ref, wk_ref, wv_ref, ct_ref, st_ref, k_o, v_o):
    x = ckv_ref[0]
    n = _rms(x[:, :KV_RANK], nw_ref[...]).astype(BF16)
    rs = x[:, KV_RANK:KV_RANK + HEAD_SLAB]
    rope = rs * ct_ref[...] + pltpu.roll(rs, ROPE_SHIFT, 1) * st_ref[...]
    kn = _dot(n, wk_ref[...])
    for h in range(MLA_HEADS):
        sl = slice(h * HEAD_SLAB, (h + 1) * HEAD_SLAB)
        k_o[0, :, sl] = (kn[:, sl] + rope).astype(BF16)
    v_o[0] = _dot(n, wv_ref[...]).astype(BF16)


def q_projection(cq, nw, w, ct, st):
    B, T, R = cq.shape
    tm = _tile(T, 256)
    W = MLA_HEADS * HEAD_SLAB
    tab = pl.BlockSpec((tm, HEAD_SLAB), lambda b, i: (i, 0))
    return pl.pallas_call(
        _qproj_kernel,
        grid=(B, T // tm),
        in_specs=[pl.BlockSpec((1, tm, R), lambda b, i: (b, i, 0)),
                  pl.BlockSpec(nw.shape, lambda b, i: (0, 0)),
                  pl.BlockSpec(w.shape, lambda b, i: (0, 0)), tab, tab],
        out_specs=pl.BlockSpec((1, tm, W), lambda b, i: (b, i, 0)),
        out_shape=jax.ShapeDtypeStruct((B, T, W), BF16),
        compiler_params=_cparams("parallel", "parallel"),
        name="q_projection",
    )(cq, nw, w, ct, st)


def kv_projection(ckv, nw, wk, wv, ct, st):
    B, T, R = ckv.shape
    tm = _tile(T, 256)
    W = MLA_HEADS * HEAD_SLAB
    tab = pl.BlockSpec((tm, HEAD_SLAB), lambda b, i: (i, 0))
    return pl.pallas_call(
        _kvproj_kernel,
        grid=(B, T // tm),
        in_specs=[pl.BlockSpec((1, tm, R), lambda b, i: (b, i, 0)),
                  pl.BlockSpec(nw.shape, lambda b, i: (0, 0)),
                  pl.BlockSpec(wk.shape, lambda b, i: (0, 0)),
                  pl.BlockSpec(wv.shape, lambda b, i: (0, 0)), tab, tab],
        out_specs=[pl.BlockSpec((1, tm, W), lambda b, i: (b, i, 0)),
                   pl.BlockSpec((1, tm, MLA_WIDTH), lambda b, i: (b, i, 0))],
        out_shape=[jax.ShapeDtypeStruct((B, T, W), BF16), jax.ShapeDtypeStruct((B, T, MLA_WIDTH), BF16)],
        compiler_params=_cparams("parallel", "parallel"),
        name="kv_projection",
    )(ckv, nw, wk, wv, ct, st)


def _attn_kernel(q_ref, k_ref, v_ref, o_ref, m_scr, l_scr, acc_scr, *, tkc, nkc):
    tq = q_ref.shape[1]
    m_scr[...] = jnp.full(m_scr.shape, -jnp.inf, F32)
    l_scr[...] = jnp.zeros(l_scr.shape, F32)
    acc_scr[...] = jnp.zeros(acc_scr.shape, F32)

    def body(c, carry):
        ks = pl.ds(pl.multiple_of(c * tkc, tkc), tkc)
        vv = v_ref[0, ks, :]
        for hh in range(2):
            q = q_ref[0, :, hh * HEAD_SLAB:(hh + 1) * HEAD_SLAB]
            kc = k_ref[0, ks, hh * HEAD_SLAB:(hh + 1) * HEAD_SLAB]
            s = lax.dot_general(q, kc, NT, preferred_element_type=F32)
            m_old = m_scr[hh]
            m_new = jnp.maximum(m_old, jnp.max(s, axis=-1, keepdims=True))
            pr = jnp.exp(s - m_new)
            alpha = jnp.exp(m_old - m_new)
            l_scr[hh] = alpha * l_scr[hh] + jnp.sum(pr, axis=-1, keepdims=True)
            acc_scr[hh] = alpha * acc_scr[hh] + _dot(pr.astype(BF16), vv)
            m_scr[hh] = m_new
        return carry

    lax.fori_loop(0, nkc, body, 0)
    lane = lax.broadcasted_iota(jnp.int32, (tq, 2 * V_HEAD), 1)
    o0 = acc_scr[0] / l_scr[0]
    o1 = acc_scr[1] / l_scr[1]
    o_ref[0] = jnp.where(lane < V_HEAD, o0, o1).astype(o_ref.dtype)


def attention(q, k, v):
    B, T, _ = q.shape
    Tk = k.shape[1]
    tq = _tile(T, 512)
    tkc = next(c for c in (640, 512, 256, 128, 64, 32) if Tk % c == 0)
    hp = MLA_HEADS // 2
    return pl.pallas_call(
        functools.partial(_attn_kernel, tkc=tkc, nkc=Tk // tkc),
        grid=(B, hp, T // tq),
        in_specs=[pl.BlockSpec((1, tq, 2 * HEAD_SLAB), lambda b, h, i: (b, i, h)),
                  pl.BlockSpec((1, Tk, 2 * HEAD_SLAB), lambda b, h, i: (b, 0, h)),
                  pl.BlockSpec((1, Tk, 2 * V_HEAD), lambda b, h, i: (b, 0, h))],
        out_specs=pl.BlockSpec((1, tq, 2 * V_HEAD), lambda b, h, i: (b, i, h)),
        out_shape=jax.ShapeDtypeStruct((B, T, MLA_WIDTH), BF16),
        scratch_shapes=[pltpu.VMEM((2, tq, 1), F32), pltpu.VMEM((2, tq, 1), F32),
                        pltpu.VMEM((2, tq, 2 * V_HEAD), F32)],
        compiler_params=_cparams("parallel", "parallel", "arbitrary"),
        name="mla_attention",
    )(q, k, v)


def _outproj_kernel(rw_ref, ft_ref, att_ref, x_ref, g_ref, lw_ref, lb_ref, w_ref, o_ref, *, alpha):
    w = w_ref
    mix = (_dot(rw_ref[0].astype(BF16), w[0:RW_WIDTH, :])
           + _dot(ft_ref[0].astype(BF16), w[RW_WIDTH:RW_WIDTH + FT_WIDTH, :])
           + _dot(att_ref[0], w[RW_WIDTH + FT_WIDTH:, :]))
    z = alpha * x_ref[0] + g_ref[0] * mix
    o_ref[0] = _standardize(z, LN_EPS) * lw_ref[...] + lb_ref[...]


def out_projection_ln(rw, ft, att, x, gate, ln_w, ln_b, w_out, alpha):
    B, T, D = x.shape
    tm = _tile(T, 256)
    tok = lambda n: pl.BlockSpec((1, tm, n), lambda b, i: (b, i, 0))
    row = pl.BlockSpec((1, D), lambda b, i: (0, 0))
    return pl.pallas_call(
        functools.partial(_outproj_kernel, alpha=alpha),
        grid=(B, T // tm),
        in_specs=[tok(RW_WIDTH), tok(FT_WIDTH), tok(MLA_WIDTH), tok(D),
                  pl.BlockSpec((1, 1, D), lambda b, i: (b, 0, 0)), row, row,
                  pl.BlockSpec(w_out.shape, lambda b, i: (0, 0))],
        out_specs=tok(D),
        out_shape=jax.ShapeDtypeStruct((B, T, D), F32),
        compiler_params=_cparams("parallel", "parallel"),
        name="out_projection_ln",
    )(rw, ft, att, x, gate, ln_w, ln_b, w_out)


def _ffn_kernel(x_ref, sh_ref, sc_ref, g_ref, lw_ref, lb_ref, w1_ref, w3_ref, w2_ref, o_ref, h_scr, acc_scr, *, alpha):
    f = pl.program_id(2)

    @pl.when(f == 0)
    def _():
        h = _standardize(x_ref[0], MOD_EPS) * (1.0 + sc_ref[0]) + sh_ref[0]
        h_scr[...] = h.astype(BF16)
        acc_scr[...] = jnp.zeros(acc_scr.shape, F32)

    hb = h_scr[...]
    a = _dot(hb, w1_ref[...])
    b = _dot(hb, w3_ref[...])
    acc_scr[...] += _dot((a * jax.nn.sigmoid(a) * b).astype(BF16), w2_ref[...])

    @pl.when(f == pl.num_programs(2) - 1)
    def _():
        z = alpha * x_ref[0] + g_ref[0] * acc_scr[...]
        o_ref[0] = _standardize(z, LN_EPS) * lw_ref[...] + lb_ref[...]


def ffn_ln(x, shift, scale, gate, ln_w, ln_b, w1, w3, w2, alpha):
    B, T, D = x.shape
    F = w1.shape[1]
    tm = _tile(T, 512)
    tf = _tile(F, 512)
    tok = pl.BlockSpec((1, tm, D), lambda b, i, f: (b, i, 0))
    vec = pl.BlockSpec((1, 1, D), lambda b, i, f: (b, 0, 0))
    row = pl.BlockSpec((1, D), lambda b, i, f: (0, 0))
    return pl.pallas_call(
        functools.partial(_ffn_kernel, alpha=alpha),
        grid=(B, T // tm, F // tf),
        in_specs=[tok, vec, vec, vec, row, row,
                  pl.BlockSpec((D, tf), lambda b, i, f: (0, f)),
                  pl.BlockSpec((D, tf), lambda b, i, f: (0, f)),
                  pl.BlockSpec((tf, D), lambda b, i, f: (f, 0))],
        out_specs=tok,
        out_shape=jax.ShapeDtypeStruct((B, T, D), F32),
        scratch_shapes=[pltpu.VMEM((tm, D), BF16), pltpu.VMEM((tm, D), F32)],
        compiler_params=_cparams("parallel", "parallel", "arbitrary"),
        name="ffn_ln",
    )(x, shift, scale, gate, ln_w, ln_b, w1, w3, w2)


def _router_kernel(x_ref, sh_ref, sc_ref, wr_ref, h_o, ti_o, tg_o):
    h = _standardize(x_ref[0], MOD_EPS) * (1.0 + sc_ref[0]) + sh_ref[0]
    h_o[0] = h.astype(BF16)
    logits = _dot(h, wr_ref[...], HI)
    lane = lax.broadcasted_iota(jnp.int32, logits.shape, 1)
    neg = jnp.float32(-jnp.inf)
    logits = jnp.where(lane < N_EXPERTS, logits, neg)
    m1 = jnp.max(logits, axis=-1, keepdims=True)
    i1 = jnp.min(jnp.where(logits == m1, lane, LANES), axis=-1, keepdims=True)
    rest = jnp.where(lane == i1, neg, logits)
    m2 = jnp.max(rest, axis=-1, keepdims=True)
    i2 = jnp.min(jnp.where(rest == m2, lane, LANES), axis=-1, keepdims=True)
    e = jnp.exp(m2 - m1)
    g1 = 1.0 / (1.0 + e)
    g2 = e / (1.0 + e)
    ti_o[0] = jnp.where(lane == 0, i1, jnp.where(lane == 1, i2, 0))
    tg_o[0] = jnp.where(lane == 0, g1, jnp.where(lane == 1, g2, 0.0))


def moe_router(x, shift, scale, wr):
    B, T, D = x.shape
    tm = _tile(T, 512)
    tok = lambda n: pl.BlockSpec((1, tm, n), lambda b, i: (b, i, 0))
    vec = pl.BlockSpec((1, 1, D), lambda b, i: (b, 0, 0))
    return pl.pallas_call(
        _router_kernel,
        grid=(B, T // tm),
        in_specs=[tok(D), vec, vec, pl.BlockSpec(wr.shape, lambda b, i: (0, 0))],
        out_specs=[tok(D), tok(LANES), tok(LANES)],
        out_shape=[jax.ShapeDtypeStruct((B, T, D), BF16), jax.ShapeDtypeStruct((B, T, LANES), jnp.int32),
                   jax.ShapeDtypeStruct((B, T, LANES), F32)],
        compiler_params=_cparams("parallel", "parallel"),
        name="moe_router",
    )(x, shift, scale, wr)


def _expert_kernel(be_ref, bv_ref, x_ref, sg_ref, w1_ref, w3_ref, w2_ref, o_ref, acc_scr):
    blk = pl.program_id(0)
    f = pl.program_id(1)
    valid = bv_ref[blk] > 0

    @pl.when(f == 0)
    def _():
        acc_scr[...] = jnp.zeros(acc_scr.shape, F32)

    @pl.when(valid)
    def _():
        xb = x_ref[...]
        a = _dot(xb, w1_ref[0])
        b = _dot(xb, w3_ref[0])
        acc_scr[...] += _dot((a * jax.nn.sigmoid(a) * b).astype(BF16), w2_ref[0])

    @pl.when(f == pl.num_programs(1) - 1)
    def _():
        o_ref[...] = acc_scr[...] * sg_ref[...]


def expert_ffn(xg, slot_gate, blk_expert, blk_valid, w1, w3, w2):
    P, D = xg.shape
    F = w1.shape[2]
    tf = _tile(F, 512)
    nblk = P // MOE_ROWS
    grid_spec = pltpu.PrefetchScalarGridSpec(
        num_scalar_prefetch=2,
        grid=(nblk, F // tf),
        in_specs=[pl.BlockSpec((MOE_ROWS, D), lambda i, f, be, bv: (i, 0)),
                  pl.BlockSpec((MOE_ROWS, 1), lambda i, f, be, bv: (i, 0)),
                  pl.BlockSpec((1, D, tf), lambda i, f, be, bv: (be[i], 0, jnp.where(bv[i] > 0, f, 0))),
                  pl.BlockSpec((1, D, tf), lambda i, f, be, bv: (be[i], 0, jnp.where(bv[i] > 0, f, 0))),
                  pl.BlockSpec((1, tf, D), lambda i, f, be, bv: (be[i], jnp.where(bv[i] > 0, f, 0), 0))],
        out_specs=pl.BlockSpec((MOE_ROWS, D), lambda i, f, be, bv: (i, 0)),
        scratch_shapes=[pltpu.VMEM((MOE_ROWS, D), F32)],
    )
    return pl.pallas_call(
        _expert_kernel,
        grid_spec=grid_spec,
        out_shape=jax.ShapeDtypeStruct((P, D), F32),
        compiler_params=_cparams("parallel", "arbitrary"),
        name="expert_ffn",
    )(blk_expert, blk_valid, xg, slot_gate, w1, w3, w2)


def _residual_ln_kernel(x_ref, y_ref, g_ref, lw_ref, lb_ref, o_ref, *, alpha):
    z = alpha * x_ref[0] + g_ref[0] * y_ref[0]
    o_ref[0] = _standardize(z, LN_EPS) * lw_ref[...] + lb_ref[...]


def residual_ln(x, y, gate, ln_w, ln_b, alpha):
    B, T, D = x.shape
    tm = _tile(T, 512)
    tok = pl.BlockSpec((1, tm, D), lambda b, i: (b, i, 0))
    row = pl.BlockSpec((1, D), lambda b, i: (0, 0))
    return pl.pallas_call(
        functools.partial(_residual_ln_kernel, alpha=alpha),
        grid=(B, T // tm),
        in_specs=[tok, tok, pl.BlockSpec((1, 1, D), lambda b, i: (b, 0, 0)), row, row],
        out_specs=tok,
        out_shape=jax.ShapeDtypeStruct((B, T, D), F32),
        compiler_params=_cparams("parallel", "parallel"),
        name="residual_ln",
    )(x, y, gate, ln_w, ln_b)


def moe_ln(x, shift, scale, gate, ln_w, ln_b, router, w1, w3, w2, alpha):
    B, T, D = x.shape
    N = B * T
    E = router.shape[1]
    wr = jnp.zeros((D, LANES), F32).at[:, :E].set(router)
    h, ti, tg = moe_router(x, shift, scale, wr)
    top_i = ti.reshape(N, LANES)[:, :2]
    top_g = tg.reshape(N, LANES)[:, :2]
    e_flat = top_i.reshape(-1)
    onehot = (e_flat[:, None] == jnp.arange(E, dtype=jnp.int32)[None, :]).astype(jnp.int32)
    rank = jnp.sum((jnp.cumsum(onehot, axis=0) - onehot) * onehot, axis=1)
    counts = jnp.sum(onehot, axis=0)
    padded = (counts + MOE_ROWS - 1) // MOE_ROWS * MOE_ROWS
    p_ends = jnp.cumsum(padded)
    p_starts = p_ends - padded
    dest = p_starts[e_flat] + rank
    P = -(-(2 * N) // MOE_ROWS) * MOE_ROWS + E * MOE_ROWS
    nblk = P // MOE_ROWS
    tok_flat = jnp.repeat(jnp.arange(N, dtype=jnp.int32), 2)
    slot_tok = jnp.zeros((P,), jnp.int32).at[dest].set(tok_flat)
    slot_gate = jnp.zeros((P,), F32).at[dest].set(top_g.reshape(-1))
    blk_start = jnp.arange(nblk, dtype=jnp.int32) * MOE_ROWS
    blk_expert = jnp.minimum(jnp.searchsorted(p_ends, blk_start, side="right"), E - 1).astype(jnp.int32)
    blk_valid = (blk_start < p_ends[-1]).astype(jnp.int32)
    xg = h.reshape(N, D)[slot_tok]
    yg = expert_ffn(xg, slot_gate[:, None], blk_expert, blk_valid, w1, w3, w2)
    d2 = dest.reshape(N, 2)
    y = (yg[d2[:, 0]] + yg[d2[:, 1]]).reshape(B, T, D)
    return residual_ln(x, y, gate, ln_w, ln_b, alpha)


def _rope_partner(w):
    half = ROPE_AXIS // 2
    idx = np.arange(QK_ROPE)
    first = (idx % ROPE_AXIS) < half
    src = np.where(first, idx + half, idx - half)
    sign = np.where(first, -1.0, 1.0).astype(np.float32)
    return w[:, src] * sign


def _layer_params(l, w_in, rw_conv, rw_w0, rw_w_up, rw_a0, rw_a_up, rw_g_up, rw_k_k, rw_k_a, rw_r_k, rw_gn_w,
                  rw_gn_b, mla_q_norm, mla_w_uq, mla_kv_norm, mla_w_ukv, w_out):
    D = w_in.shape[1]
    wi = w_in[l]
    o = np.cumsum([0, 3 * RW_WIDTH, G_RANK, LORA_RANK, LORA_RANK, LORA_RANK, LORA_RANK, FT_WIDTH, Q_RANK, KV_RANK, QK_ROPE])
    piece = lambda i: wi[:, o[i]:o[i + 1]]
    zpad = lambda n: jnp.zeros((D, n), F32)
    lora = [jnp.concatenate([piece(i), zpad(LANES - LORA_RANK)], axis=1) for i in (2, 3, 4, 5)]
    w_rw = jnp.concatenate([piece(0), piece(1)] + lora, axis=1)
    kr = piece(9)
    w_ckv = jnp.concatenate([piece(8), zpad(QK_NOPE), kr, _rope_partner(kr)], axis=1)
    pad_rows = lambda a: jnp.concatenate([a, jnp.zeros((a.shape[0], LANES - LORA_RANK, a.shape[2]), F32)], axis=1)
    head = jnp.arange(RW_WIDTH) // RW_HEAD
    uq = mla_w_uq[l].reshape(Q_RANK, MLA_HEADS, QK_NOPE + QK_ROPE)
    uq_rope = uq[:, :, QK_NOPE:]
    uq_partner = _rope_partner(uq_rope.reshape(Q_RANK * MLA_HEADS, QK_ROPE)).reshape(Q_RANK, MLA_HEADS, QK_ROPE)
    w_q = jnp.concatenate([uq, uq_partner], axis=2).reshape(Q_RANK, MLA_HEADS * HEAD_SLAB)
    ukv = mla_w_ukv[l].reshape(KV_RANK, MLA_HEADS, QK_NOPE + V_HEAD)
    w_k = jnp.concatenate([ukv[:, :, :QK_NOPE], jnp.zeros((KV_RANK, MLA_HEADS, HEAD_SLAB - QK_NOPE), F32)],
                          axis=2).reshape(KV_RANK, MLA_HEADS * HEAD_SLAB)
    w_v = ukv[:, :, QK_NOPE:].reshape(KV_RANK, MLA_WIDTH)
    return {
        "w_in": [w.astype(BF16) for w in (w_rw, piece(6), piece(7), w_ckv)],
        "rw": {"conv": rw_conv[l], "k_k": rw_k_k[l][None], "k_a": rw_k_a[l][None], "r_k": rw_r_k[l].reshape(1, RW_WIDTH),
               "w0": rw_w0[l], "a0": rw_a0[l], "w_up": pad_rows(rw_w_up[l]), "a_up": pad_rows(rw_a_up[l]),
               "g_up": rw_g_up[l], "gn_w": rw_gn_w[l][None], "gn_b": rw_gn_b[l][None],
               "E": (head[:, None] == head[None, :]).astype(F32)},
        "q_norm": mla_q_norm[l][None], "w_q": w_q.astype(BF16),
        "kv_norm": mla_kv_norm[l][None], "w_k": w_k.astype(BF16), "w_v": w_v.astype(BF16),
        "w_out": w_out[l].astype(BF16),
    }


def _rope_tables(T, use_rope):
    ones = jnp.ones((T, QK_NOPE), F32)
    zeros = jnp.zeros((T, QK_NOPE), F32)
    zpad = jnp.zeros((T, HEAD_SLAB - QK_NOPE - QK_ROPE), F32)
    if use_rope:
        row = jnp.repeat(jnp.arange(T // GRID_W), GRID_W).astype(F32)
        col = (jnp.arange(T) % GRID_W).astype(F32)
        inv = ROPE_THETA ** (-jnp.arange(0, ROPE_AXIS, 2, dtype=F32) / ROPE_AXIS)
        ang = jnp.stack([row[:, None] * inv, col[:, None] * inv], axis=1)
        ang = jnp.broadcast_to(ang[:, :, None, :], (T, 2, 2, ROPE_AXIS // 2)).reshape(T, QK_ROPE)
        cos, sin = jnp.cos(ang), jnp.sin(ang)
    else:
        cos, sin = jnp.ones((T, QK_ROPE), F32), jnp.zeros((T, QK_ROPE), F32)
    return jnp.concatenate([ones, cos, zpad], axis=1), jnp.concatenate([zeros, sin, zpad], axis=1)


def _mixer(h_pieces_lat, h_pieces_ctx, p, tabs_lat, tabs_ctx, emit_ctx):
    rw_l, ft_l, cq_l, ckv_l = h_pieces_lat
    rw_c, ft_c, cq_c, ckv_c = h_pieces_ctx
    rwo_l, rwo_c = rwkv_branch(rw_l, rw_c, p["rw"], emit_ctx)
    fto_l = fourier_mixer(ft_l)
    q_l = q_projection(cq_l, p["q_norm"], p["w_q"], *tabs_lat)
    k_l, v_l = kv_projection(ckv_l, p["kv_norm"], p["w_k"], p["w_v"], *tabs_lat)
    k_c, v_c = kv_projection(ckv_c, p["kv_norm"], p["w_k"], p["w_v"], *tabs_ctx)
    att_l = attention(q_l, jnp.concatenate([k_l, k_c], axis=1), jnp.concatenate([v_l, v_c], axis=1))
    out_c = None
    if emit_ctx:
        fto_c = fourier_mixer(ft_c)
        q_c = q_projection(cq_c, p["q_norm"], p["w_q"], *tabs_ctx)
        att_c = attention(q_c, k_c, v_c)
        out_c = (rwo_c, fto_c, att_c)
    return (rwo_l, fto_l, att_l), out_c


def kernel(x, c, ctx, c_ctx, ada_w, ada_b, w_in, rw_conv, rw_w0, rw_w_up, rw_a0, rw_a_up, rw_g_up, rw_k_k, rw_k_a,
           rw_r_k, rw_gn_w, rw_gn_b, mla_q_norm, mla_w_uq, mla_kv_norm, mla_w_ukv, w_out, ln1_w, ln1_b, ln2_w, ln2_b,
           ffn_w1, ffn_w3, ffn_w2, moe_router, moe_w1, moe_w3, moe_w2):
    B, T, D = x.shape
    Tc = ctx.shape[1]
    depth = w_in.shape[0]
    alpha = (2 * depth) ** 0.25
    assert B + 1 <= SUBLANES
    cc = jnp.zeros((SUBLANES, D), F32).at[:B].set(c).at[B].set(c_ctx)
    ada = ada_vectors(cc, ada_w, ada_b)
    tabs_lat = _rope_tables(T, True)
    tabs_ctx = _rope_tables(Tc, False)
    for l in range(depth):
        last = l == depth - 1
        p = _layer_params(l, w_in, rw_conv, rw_w0, rw_w_up, rw_a0, rw_a_up, rw_g_up, rw_k_k, rw_k_a, rw_r_k,
                          rw_gn_w, rw_gn_b, mla_q_norm, mla_w_uq, mla_kv_norm, mla_w_ukv, w_out)
        mods = ada[l].reshape(SUBLANES, 6, D)
        lat = [mods[:B, j][:, None, :] for j in range(6)]
        cx = [jnp.broadcast_to(mods[B, j][None, None, :], (B, 1, D)) for j in range(6)]
        sh_m, sc_m, g_m, sh_f, sc_f, g_f = lat
        csh_m, csc_m, cg_m, csh_f, csc_f, cg_f = cx
        ln1 = (ln1_w[l][None], ln1_b[l][None])
        ln2 = (ln2_w[l][None], ln2_b[l][None])

        pieces_l = in_projection(x, sh_m, sc_m, p["w_in"])
        pieces_c = in_projection(ctx, csh_m, csc_m, p["w_in"])
        mix_l, mix_c = _mixer(pieces_l, pieces_c, p, tabs_lat, tabs_ctx, not last)
        x = out_projection_ln(*mix_l, x, g_m, *ln1, p["w_out"], alpha)
        i = l // 2
        if l % 2 == 0:
            dense = (ffn_w1[i].astype(BF16), ffn_w3[i].astype(BF16), ffn_w2[i].astype(BF16))
            x = ffn_ln(x, sh_f, sc_f, g_f, *ln2, *dense, alpha)
        else:
            x = moe_ln(x, sh_f, sc_f, g_f, *ln2, moe_router[i], moe_w1[i].astype(BF16), moe_w3[i].astype(BF16),
                       moe_w2[i].astype(BF16), alpha)
        if not last:
            ctx = out_projection_ln(*mix_c, ctx, cg_m, *ln1, p["w_out"], alpha)
            if l % 2 == 0:
                ctx = ffn_ln(ctx, csh_f, csc_f, cg_f, *ln2, *dense, alpha)
            else:
                ctx = moe_ln(ctx, csh_f, csc_f, cg_f, *ln2, moe_router[i], moe_w1[i].astype(BF16),
                             moe_w3[i].astype(BF16), moe_w2[i].astype(BF16), alpha)
    return x
```

```python
import functools
import math

import numpy as np
import jax
import jax.numpy as jnp
from jax import lax
from jax.experimental import pallas as pl
from jax.experimental.pallas import tpu as pltpu

F32 = jnp.float32
BF16 = jnp.bfloat16
HI = lax.Precision.HIGHEST

LANES = 128
SUBLANES = 8
VMEM_LIMIT = 56 * 1024 * 1024

GRID_W = 64
RW_HEADS = 8
RW_HEAD = 64
RW_WIDTH = RW_HEADS * RW_HEAD
G_RANK = 128
LORA_RANK = 64
GN_EPS = 64e-5
FT_GROUP = 64
FT_WIDTH = 512
MLA_HEADS = 16
QK_NOPE = 64
QK_ROPE = 32
V_HEAD = 64
MLA_WIDTH = MLA_HEADS * V_HEAD
Q_RANK = 512
KV_RANK = 256
ROPE_AXIS = QK_ROPE // 2
ROPE_THETA = 10000.0
ATTN_SCALE = (QK_NOPE + QK_ROPE) ** -0.5
Q_SCALE = ATTN_SCALE * math.log2(math.e)
N_EXPERTS = 8
LN_EPS = 1e-5
MOD_EPS = 1e-6
RMS_EPS = 1e-6
CHUNK = 64
INV_BLOCK = 16
MOE_ROWS = 1024

NT = (((1,), (1,)), ((), ()))
TN = (((0,), (0,)), ((), ()))


def _cparams(*sem):
    return pltpu.CompilerParams(dimension_semantics=sem, vmem_limit_bytes=VMEM_LIMIT)


def _dot(a, b, prec=None):
    return jnp.dot(a, b, precision=prec, preferred_element_type=F32)


def _standardize(x, eps):
    mu = jnp.mean(x, axis=-1, keepdims=True)
    xc = x - mu
    var = jnp.mean(xc * xc, axis=-1, keepdims=True)
    return xc * lax.rsqrt(var + eps)


def _tile(n, pref):
    t = min(n, pref)
    assert n % t == 0, (n, pref)
    return t


def _ada_kernel(c_ref, w_ref, b_ref, o_ref):
    c = c_ref[...]
    s = c * jax.nn.sigmoid(c)
    o_ref[0] = _dot(s, w_ref[0], HI) + b_ref[0]


def ada_vectors(cc, ada_w, ada_b):
    L, D, N6 = ada_w.shape
    tn = _tile(N6, 1024)
    return pl.pallas_call(
        _ada_kernel,
        grid=(L, N6 // tn),
        in_specs=[pl.BlockSpec((SUBLANES, D), lambda l, j: (0, 0)),
                  pl.BlockSpec((1, D, tn), lambda l, j: (l, 0, j)),
                  pl.BlockSpec((1, 1, tn), lambda l, j: (l, 0, j))],
        out_specs=pl.BlockSpec((1, SUBLANES, tn), lambda l, j: (l, 0, j)),
        out_shape=jax.ShapeDtypeStruct((L, SUBLANES, N6), F32),
        compiler_params=_cparams("parallel", "parallel"),
        name="ada_vectors",
    )(cc, ada_w, ada_b.reshape(L, 1, N6))


def _inproj_kernel(x_ref, sh_ref, sc_ref, *refs):
    nw = len(refs) // 2
    h = _standardize(x_ref[0], MOD_EPS) * (1.0 + sc_ref[0]) + sh_ref[0]
    hb = h.astype(BF16)
    for w_ref, o_ref in zip(refs[:nw], refs[nw:]):
        o_ref[0] = _dot(hb, w_ref[...])


def in_projection(x, shift, scale, weights):
    B, T, D = x.shape
    tm = _tile(T, 256)
    vec = pl.BlockSpec((1, 1, D), lambda b, i: (b, 0, 0))
    return pl.pallas_call(
        _inproj_kernel,
        grid=(B, T // tm),
        in_specs=[pl.BlockSpec((1, tm, D), lambda b, i: (b, i, 0)), vec, vec]
        + [pl.BlockSpec(w.shape, lambda b, i: (0, 0)) for w in weights],
        out_specs=[pl.BlockSpec((1, tm, w.shape[1]), lambda b, i: (b, i, 0)) for w in weights],
        out_shape=[jax.ShapeDtypeStruct((B, T, w.shape[1]), F32) for w in weights],
        compiler_params=_cparams("parallel", "parallel"),
        name="in_projection",
    )(x, shift, scale, *weights)


RW_COLS = 3 * RW_WIDTH + G_RANK + 4 * LANES


def _softplus(z):
    return jnp.maximum(z, 0.0) + jnp.log(1.0 + jnp.exp(-jnp.abs(z)))


def _rwkv_prep_kernel(x_ref, xp_ref, xn_ref, conv_ref, kk_ref, ka_ref, rk_ref, w0_ref, a0_ref,
                      wup_ref, aup_ref, gup_ref, e_ref,
                      r_o, v_o, kk_o, ld0_o, b0_o, kr0_o, ld1_o, b1_o, kr1_o, bonus_o, gate_o):
    i = pl.program_id(1)
    n = pl.num_programs(1)
    x = x_ref[0]
    W3 = 3 * RW_WIDTH
    raw = x[:, :W3]
    tm = raw.shape[0]
    row = lax.broadcasted_iota(jnp.int32, (tm, 1), 0)
    prev_row = jnp.where(i > 0, xp_ref[0, SUBLANES - 1:SUBLANES, :], 0.0)
    next_row = jnp.where(i < n - 1, xn_ref[0, 0:1, :], 0.0)
    xm = jnp.where(row == 0, prev_row, pltpu.roll(raw, 1, 0))
    xq = jnp.where(row == tm - 1, next_row, pltpu.roll(raw, tm - 1, 0))
    cw = conv_ref[...]
    y = xm * cw[0:1] + raw * cw[1:2] + xq * cw[2:3]
    r = y[:, :RW_WIDTH]
    k = y[:, RW_WIDTH:2 * RW_WIDTH]
    v = y[:, 2 * RW_WIDTH:W3]
    E = e_ref[...]
    kkv = k * kk_ref[...]
    kk = kkv / jnp.maximum(jnp.sqrt(_dot(kkv * kkv, E, HI)), 1e-12)
    r_o[0] = r
    v_o[0] = v
    kk_o[0] = kk
    g_dn = x[:, W3:W3 + G_RANK]
    gate_o[0] = _dot(jax.nn.sigmoid(g_dn), gup_ref[...], HI)
    bonus = jnp.zeros_like(r)
    outs = ((ld0_o, b0_o, kr0_o), (ld1_o, b1_o, kr1_o))
    for d in range(2):
        base = W3 + G_RANK
        w_dn = x[:, base + d * LANES: base + (d + 1) * LANES]
        a_dn = x[:, base + (2 + d) * LANES: base + (3 + d) * LANES]
        z = w0_ref[d:d + 1, :] + _dot(jnp.tanh(w_dn), wup_ref[d], HI)
        logw = -_softplus(-z) - 0.5
        a = jax.nn.sigmoid(a0_ref[d:d + 1, :] + _dot(a_dn, aup_ref[d], HI))
        kr = k * (1.0 + (a - 1.0) * ka_ref[...])
        ld_o, b_o, kr_o = outs[d]
        ld_o[0] = -jnp.exp(logw)
        b_o[0] = a * kk
        kr_o[0] = kr
        bonus = bonus + _dot(r * kr * rk_ref[...], E, HI) * v
    bonus_o[0] = bonus


def rwkv_prep(rw, p):
    B, T, _ = rw.shape
    tm = _tile(T, 256)
    nh = tm // SUBLANES
    last = T // SUBLANES - 1
    W3 = 3 * RW_WIDTH
    full = lambda a: pl.BlockSpec(a.shape, lambda b, i: (0,) * a.ndim)
    params = [p["conv"], p["k_k"], p["k_a"], p["r_k"], p["w0"], p["a0"], p["w_up"], p["a_up"], p["g_up"], p["E"]]
    outs = pl.pallas_call(
        _rwkv_prep_kernel,
        grid=(B, T // tm),
        in_specs=[pl.BlockSpec((1, tm, RW_COLS), lambda b, i: (b, i, 0)),
                  pl.BlockSpec((1, SUBLANES, W3), lambda b, i: (b, jnp.maximum(i * nh - 1, 0), 0)),
                  pl.BlockSpec((1, SUBLANES, W3), lambda b, i: (b, jnp.minimum((i + 1) * nh, last), 0))]
        + [full(a) for a in params],
        out_specs=[pl.BlockSpec((1, tm, RW_WIDTH), lambda b, i: (b, i, 0))] * 11,
        out_shape=[jax.ShapeDtypeStruct((B, T, RW_WIDTH), F32)] * 11,
        compiler_params=_cparams("parallel", "parallel"),
        name="rwkv_prep",
    )(rw, rw, rw, *params)
    return outs


def _chunk_masks(reverse):
    t = lax.broadcasted_iota(jnp.int32, (CHUNK, CHUNK), 0)
    j = lax.broadcasted_iota(jnp.int32, (CHUNK, CHUNK), 1)
    strict = (j > t) if reverse else (j < t)
    incl = (j >= t) if reverse else (j <= t)
    blk = (t // INV_BLOCK) == (j // INV_BLOCK)
    eye = jnp.where(t == j, 1.0, 0.0).astype(F32)
    return strict, incl, blk, eye


def _chunk_math(ld, r, v, kk, b, kr, masks):
    strict, incl, blk, eye = masks
    H = ld.shape[0]
    tri = jnp.broadcast_to(jnp.where(incl, 1.0, 0.0).astype(F32), (H, CHUNK, CHUNK))
    Lc = jnp.einsum("hct,htk->hck", tri, ld, precision=HI, preferred_element_type=F32)
    Lx = Lc - ld
    Lt = jnp.sum(ld, axis=1, keepdims=True)
    kh = kk * jnp.exp(Lx)
    rh = r * jnp.exp(Lc)
    ginv = jnp.exp(-Lc)
    bh = b * ginv
    kkh = kr * ginv
    gout = jnp.exp(Lt - Lc)
    Bb = b * gout
    Kb = kr * gout
    ein = lambda spec, a, c: jnp.einsum(spec, a.astype(BF16), c.astype(BF16), preferred_element_type=F32)
    mm = lambda a, c: ein("hij,hjk->hik", a, c)
    nt = lambda a, c: ein("hik,hjk->hij", a, c)
    tn = lambda a, c: ein("hji,hjk->hik", a, c)
    Mab = jnp.where(strict, nt(kh, bh), 0.0)
    Mak = jnp.where(strict, nt(kh, kkh), 0.0)
    Arb = jnp.where(incl, nt(rh, bh), 0.0)
    Ark = jnp.where(incl, nt(rh, kkh), 0.0)
    Nd = jnp.where(blk, Mab, 0.0)
    No = Mab - Nd
    N2 = mm(Nd, Nd)
    N4 = mm(N2, N2)
    N8 = mm(N4, N4)
    Td = mm(mm(mm(eye - Nd, eye + N2), eye + N4), eye + N8)
    M2 = mm(Td, No)
    Tm = mm(mm(eye - M2, eye + mm(M2, M2)), Td)
    P1 = mm(Tm, kh)
    P2 = mm(Tm, mm(Mak, v))
    Q1 = rh - mm(Arb, P1)
    Yi = mm(Ark, v) - mm(Arb, P2)
    G = eye * jnp.exp(Lt) - tn(Bb, P1)
    H = tn(Kb, v) - tn(Bb, P2)
    return Q1, Yi, G, H


def _rwkv_chunk_kernel(ld_ref, r_ref, v_ref, kk_ref, b_ref, kr_ref, q_o, yi_o, g_o, h_o, *, nch, reverse):
    masks = _chunk_masks(reverse)

    def body(c, carry):
        sl = pl.ds(pl.multiple_of(c * CHUNK, CHUNK), CHUNK)
        get = lambda ref: ref[0, :, sl, :]
        Q1, Yi, G, H = _chunk_math(get(ld_ref), get(r_ref), get(v_ref), get(kk_ref), get(b_ref), get(kr_ref), masks)
        q_o[0, :, sl, :] = Q1
        yi_o[0, :, sl, :] = Yi
        g_o[0, :, sl, :] = G
        h_o[0, :, sl, :] = H
        return carry

    lax.fori_loop(0, nch, body, 0)


def rwkv_chunks(ld, r, v, kk, b, kr, reverse):
    B, H, T, K = ld.shape
    tt = _tile(T, 4 * CHUNK)
    spec = pl.BlockSpec((1, H, tt, K), lambda bb, i: (bb, 0, i, 0))
    return pl.pallas_call(
        functools.partial(_rwkv_chunk_kernel, nch=tt // CHUNK, reverse=reverse),
        grid=(B, T // tt),
        in_specs=[spec] * 6,
        out_specs=[spec] * 4,
        out_shape=[jax.ShapeDtypeStruct((B, H, T, K), F32)] * 4,
        compiler_params=_cparams("parallel", "parallel"),
        name="rwkv_chunks_bwd" if reverse else "rwkv_chunks_fwd",
    )(ld, r, v, kk, b, kr)


def _rwkv_seq_kernel(q_ref, yi_ref, g_ref, h_ref, s0_ref, y_o, sf_o, s_scr, *, nch, reverse):
    @pl.when(pl.program_id(1) == 0)
    def _():
        s_scr[...] = s0_ref[0]

    def body(cc, carry):
        c = (nch - 1 - cc) if reverse else cc
        sl = pl.ds(pl.multiple_of(c * CHUNK, CHUNK), CHUNK)
        S = s_scr[...]
        y = jnp.einsum("hck,hkv->hcv", q_ref[0, :, sl, :], S, precision=HI, preferred_element_type=F32)
        y_o[0, :, sl, :] = y + yi_ref[0, :, sl, :]
        s_scr[...] = jnp.einsum("hjk,hkv->hjv", g_ref[0, :, sl, :], S, precision=HI,
                                preferred_element_type=F32) + h_ref[0, :, sl, :]
        return carry

    lax.fori_loop(0, nch, body, 0)
    sf_o[0] = s_scr[...]


def rwkv_sequential(q1, yi, g, hm, s0, reverse):
    B, H, T, K = q1.shape
    tt = _tile(T, 8 * CHUNK)
    n = T // tt
    idx = (lambda bb, i: (bb, 0, n - 1 - i, 0)) if reverse else (lambda bb, i: (bb, 0, i, 0))
    spec = pl.BlockSpec((1, H, tt, K), idx)
    sspec = pl.BlockSpec((1, H, K, K), lambda bb, i: (bb, 0, 0, 0))
    return pl.pallas_call(
        functools.partial(_rwkv_seq_kernel, nch=tt // CHUNK, reverse=reverse),
        grid=(B, n),
        in_specs=[spec] * 4 + [sspec],
        out_specs=[spec, sspec],
        out_shape=[jax.ShapeDtypeStruct((B, H, T, K), F32), jax.ShapeDtypeStruct((B, H, K, K), F32)],
        scratch_shapes=[pltpu.VMEM((H, K, K), F32)],
        compiler_params=_cparams("parallel", "arbitrary"),
        name="rwkv_seq_bwd" if reverse else "rwkv_seq_fwd",
    )(q1, yi, g, hm, s0)


def _rwkv_readout_kernel(yf_ref, yb_ref, bonus_ref, gate_ref, gw_ref, gb_ref, e_ref, o_ref):
    Em = e_ref[...] * (1.0 / RW_HEAD)
    ys = yf_ref[0] + yb_ref[0]
    yc = ys - _dot(ys, Em, HI)
    var = _dot(yc * yc, Em, HI)
    yn = yc * lax.rsqrt(var + GN_EPS) * gw_ref[...] + gb_ref[...]
    o_ref[0] = (yn + bonus_ref[0]) * gate_ref[0]


def rwkv_readout(yf, yb, bonus, gate, p):
    B, T, W = yf.shape
    tm = _tile(T, 512)
    spec = pl.BlockSpec((1, tm, W), lambda b, i: (b, i, 0))
    full = lambda a: pl.BlockSpec(a.shape, lambda b, i: (0,) * a.ndim)
    return pl.pallas_call(
        _rwkv_readout_kernel,
        grid=(B, T // tm),
        in_specs=[spec] * 4 + [full(p["gn_w"]), full(p["gn_b"]), full(p["E"])],
        out_specs=spec,
        out_shape=jax.ShapeDtypeStruct((B, T, W), F32),
        compiler_params=_cparams("parallel", "parallel"),
        name="rwkv_readout",
    )(yf, yb, bonus, gate, p["gn_w"], p["gn_b"], p["E"])


def _to_heads(a):
    B, T, _ = a.shape
    return a.reshape(B, T, RW_HEADS, RW_HEAD).transpose(0, 2, 1, 3)


def _from_heads(a):
    B, H, T, K = a.shape
    return a.transpose(0, 2, 1, 3).reshape(B, T, H * K)


def rwkv_branch(rw_lat, rw_ctx, p, emit_ctx):
    prep_l = rwkv_prep(rw_lat, p)
    prep_c = rwkv_prep(rw_ctx, p)
    B = rw_lat.shape[0]
    s_zero = jnp.zeros((B, RW_HEADS, RW_HEAD, RW_HEAD), F32)

    def scans(prep, s0s):
        r, v, kk = (_to_heads(a) for a in prep[:3])
        ys, finals = [], []
        for d in range(2):
            ld, b, kr = (_to_heads(a) for a in prep[3 + 3 * d: 6 + 3 * d])
            q1, yi, g, hm = rwkv_chunks(ld, r, v, kk, b, kr, reverse=bool(d))
            y, sf = rwkv_sequential(q1, yi, g, hm, s0s[d], reverse=bool(d))
            ys.append(_from_heads(y))
            finals.append(sf)
        return ys, finals

    ys_c, fin_c = scans(prep_c, (s_zero, s_zero))
    ys_l, _ = scans(prep_l, fin_c)
    out_l = rwkv_readout(ys_l[0], ys_l[1], prep_l[9], prep_l[10], p)
    out_c = rwkv_readout(ys_c[0], ys_c[1], prep_c[9], prep_c[10], p) if emit_ctx else None
    return out_l, out_c


def _dft_mats(n):
    a = 2.0 * np.pi * np.outer(np.arange(n), np.arange(n)) / n
    return np.cos(a), np.sin(a)


def _fft1_kernel(u_ref, c_ref, s_ref, twc_ref, tws_ref, ar_o, ai_o, *, tn2, ch):
    U = u_ref[0]
    Ar = _dot(c_ref[...], U, HI)
    Ai = -_dot(s_ref[...], U, HI)
    twc = twc_ref[0]
    tws = tws_ref[0]
    for j in range(tn2):
        ct = twc[:, j:j + 1]
        st = tws[:, j:j + 1]
        a_r = Ar[:, j * ch:(j + 1) * ch]
        a_i = Ai[:, j * ch:(j + 1) * ch]
        ar_o[0, j] = a_r * ct + a_i * st
        ai_o[0, j] = a_i * ct - a_r * st


def _fft2_kernel(ar_ref, ai_ref, c_ref, s_ref, cc_ref, sc_ref, o_ref):
    Ar = ar_ref[0]
    Ai = ai_ref[0]
    C = c_ref[...]
    S = s_ref[...]
    Yr = _dot(C, Ar, HI) + _dot(S, Ai, HI)
    Yi = _dot(C, Ai, HI) - _dot(S, Ar, HI)
    Cc = cc_ref[...]
    Sc = sc_ref[...]
    for m in range(Ar.shape[1] // LANES):
        sl = slice(m * LANES, (m + 1) * LANES)
        o_ref[0, :, sl] = _dot(Yr[:, sl], Cc, HI) + _dot(Yi[:, sl], Sc, HI)


def fourier_mixer(u):
    B, T, ch = u.shape
    lg = int(round(math.log2(T)))
    assert 1 << lg == T
    N1 = 1 << ((lg + 1) // 2)
    N2 = T // N1
    c1, s1 = _dft_mats(N1)
    c2, s2 = _dft_mats(N2)
    tw = 2.0 * np.pi * np.outer(np.arange(N1), np.arange(N2)) / T
    tn2 = min(SUBLANES, N2)
    nj = N2 // tn2
    twc = np.cos(tw).reshape(N1, nj, tn2).transpose(1, 0, 2)
    tws = np.sin(tw).reshape(N1, nj, tn2).transpose(1, 0, 2)
    cg, sg = _dft_mats(FT_GROUP)
    scale = 1.0 / math.sqrt(T * FT_GROUP)
    eye2 = np.eye(LANES // FT_GROUP)
    cc = np.kron(eye2, cg) * scale
    sc = np.kron(eye2, sg) * scale
    f = lambda a: jnp.asarray(a, F32)
    full2 = lambda n, m: pl.BlockSpec((n, m), lambda b, j: (0, 0))

    ar, ai = pl.pallas_call(
        functools.partial(_fft1_kernel, tn2=tn2, ch=ch),
        grid=(B, nj),
        in_specs=[pl.BlockSpec((1, N1, tn2 * ch), lambda b, j: (b, 0, j)),
                  full2(N1, N1), full2(N1, N1),
                  pl.BlockSpec((1, N1, tn2), lambda b, j: (j, 0, 0)),
                  pl.BlockSpec((1, N1, tn2), lambda b, j: (j, 0, 0))],
        out_specs=[pl.BlockSpec((1, tn2, N1, ch), lambda b, j: (b, j, 0, 0))] * 2,
        out_shape=[jax.ShapeDtypeStruct((B, N2, N1, ch), F32)] * 2,
        compiler_params=_cparams("parallel", "parallel"),
        name="fft_stage1",
    )(u.reshape(B, N1, N2 * ch), f(c1), f(s1), f(twc), f(tws))

    tk1 = min(SUBLANES, N1)
    blk = pl.BlockSpec((1, N2, tk1 * ch), lambda b, j: (b, 0, j))
    out = pl.pallas_call(
        _fft2_kernel,
        grid=(B, N1 // tk1),
        in_specs=[blk, blk, full2(N2, N2), full2(N2, N2), full2(LANES, LANES), full2(LANES, LANES)],
        out_specs=blk,
        out_shape=jax.ShapeDtypeStruct((B, N2, N1 * ch), F32),
        compiler_params=_cparams("parallel", "parallel"),
        name="fft_stage2",
    )(ar.reshape(B, N2, N1 * ch), ai.reshape(B, N2, N1 * ch), f(c2), f(s2), f(cc), f(sc))
    return out.reshape(B, T, ch)


HEAD_SLAB = LANES
ROPE_SHIFT = HEAD_SLAB - QK_ROPE


def _rms(x, w):
    return x * lax.rsqrt(jnp.mean(x * x, axis=-1, keepdims=True) + RMS_EPS) * w


def _qproj_kernel(cq_ref, nw_ref, w_ref, ct_ref, st_ref, q_o):
    q = _dot(_rms(cq_ref[0], nw_ref[...]).astype(BF16), w_ref[...])
    ct = ct_ref[...]
    st = st_ref[...]
    for h in range(MLA_HEADS):
        sl = slice(h * HEAD_SLAB, (h + 1) * HEAD_SLAB)
        s = q[:, sl]
        q_o[0, :, sl] = ((s * ct + pltpu.roll(s, ROPE_SHIFT, 1) * st) * Q_SCALE).astype(BF16)


def _kvproj_kernel(ckv_ref, nw_ref, wk_ref, wvt_ref, ct_ref, st_ref, k_o, vt_o):
    x = ckv_ref[0]
    n = _rms(x[:, :KV_RANK], nw_ref[...]).astype(BF16)
    rs = x[:, KV_RANK:KV_RANK + HEAD_SLAB]
    rope = rs * ct_ref[...] + pltpu.roll(rs, ROPE_SHIFT, 1) * st_ref[...]
    kn = _dot(n, wk_ref[...])
    for h in range(MLA_HEADS):
        sl = slice(h * HEAD_SLAB, (h + 1) * HEAD_SLAB)
        k_o[0, :, sl] = (kn[:, sl] + rope).astype(BF16)
    vt_o[0] = lax.dot_general(wvt_ref[...], n, NT, preferred_element_type=F32).astype(BF16)


def q_projection(cq, nw, w, ct, st):
    B, T, R = cq.shape
    tm = _tile(T, 256)
    W = MLA_HEADS * HEAD_SLAB
    tab = pl.BlockSpec((tm, HEAD_SLAB), lambda b, i: (i, 0))
    return pl.pallas_call(
        _qproj_kernel,
        grid=(B, T // tm),
        in_specs=[pl.BlockSpec((1, tm, R), lambda b, i: (b, i, 0)),
                  pl.BlockSpec(nw.shape, lambda b, i: (0, 0)),
                  pl.BlockSpec(w.shape, lambda b, i: (0, 0)), tab, tab],
        out_specs=pl.BlockSpec((1, tm, W), lambda b, i: (b, i, 0)),
        out_shape=jax.ShapeDtypeStruct((B, T, W), BF16),
        compiler_params=_cparams("parallel", "parallel"),
        name="q_projection",
    )(cq, nw, w, ct, st)


def kv_projection(ckv, nw, wk, wvt, ct, st):
    B, T, R = ckv.shape
    tm = _tile(T, 256)
    W = MLA_HEADS * HEAD_SLAB
    tab = pl.BlockSpec((tm, HEAD_SLAB), lambda b, i: (i, 0))
    return pl.pallas_call(
        _kvproj_kernel,
        grid=(B, T // tm),
        in_specs=[pl.BlockSpec((1, tm, R), lambda b, i: (b, i, 0)),
                  pl.BlockSpec(nw.shape, lambda b, i: (0, 0)),
                  pl.BlockSpec(wk.shape, lambda b, i: (0, 0)),
                  pl.BlockSpec(wvt.shape, lambda b, i: (0, 0)), tab, tab],
        out_specs=[pl.BlockSpec((1, tm, W), lambda b, i: (b, i, 0)),
                   pl.BlockSpec((1, MLA_WIDTH, tm), lambda b, i: (b, 0, i))],
        out_shape=[jax.ShapeDtypeStruct((B, T, W), BF16), jax.ShapeDtypeStruct((B, MLA_WIDTH, T), BF16)],
        compiler_params=_cparams("parallel", "parallel"),
        name="kv_projection",
    )(ckv, nw, wk, wvt, ct, st)


def _attn_kernel(q_ref, k_ref, vt_ref, o_ref, m_scr, l_scr, acc_scr, sa_scr, sb_scr, *, tkc, nkc):
    m_scr[...] = jnp.full(m_scr.shape, -jnp.inf, F32)
    l_scr[...] = jnp.zeros(l_scr.shape, F32)
    acc_scr[...] = jnp.zeros(acc_scr.shape, F32)

    def chunk(c):
        return pl.ds(pl.multiple_of(c * tkc, tkc), tkc)

    def scores(c, dst):
        for hh in range(2):
            q = q_ref[0, :, hh * HEAD_SLAB:(hh + 1) * HEAD_SLAB]
            kc = k_ref[0, chunk(c), hh * HEAD_SLAB:(hh + 1) * HEAD_SLAB]
            dst[hh] = lax.dot_general(kc, q, NT, preferred_element_type=F32)

    def consume(src, c):
        for hh in range(2):
            s = src[hh]
            m_old = m_scr[hh]
            m_new = jnp.maximum(m_old, jnp.max(s, axis=0, keepdims=True))
            pr = jnp.exp2(s - m_new)
            alpha = jnp.exp2(m_old - m_new)
            l_scr[hh] = alpha * l_scr[hh] + jnp.sum(pr, axis=0, keepdims=True)
            m_scr[hh] = m_new
            rows = slice(hh * V_HEAD, (hh + 1) * V_HEAD)
            acc_scr[rows, :] = alpha * acc_scr[rows, :] + _dot(vt_ref[0, rows, chunk(c)], pr.astype(BF16))

    scores(0, sa_scr)
    npairs = (nkc - 1) // 2

    def body(i, carry):
        c = 2 * i
        scores(c + 1, sb_scr)
        consume(sa_scr, c)
        scores(c + 2, sa_scr)
        consume(sb_scr, c + 1)
        return carry

    lax.fori_loop(0, npairs, body, 0)
    cc = 2 * npairs
    if (nkc - 1) % 2 == 1:
        scores(cc + 1, sb_scr)
        consume(sa_scr, cc)
        consume(sb_scr, cc + 1)
    else:
        consume(sa_scr, cc)
    for hh in range(2):
        rows = slice(hh * V_HEAD, (hh + 1) * V_HEAD)
        acc_scr[rows, :] = acc_scr[rows, :] / l_scr[hh]
    o_ref[0] = acc_scr[...].T.astype(o_ref.dtype)


def attention(q, k, vt):
    B, T, _ = q.shape
    Tk = k.shape[1]
    tq = _tile(T, 256)
    tkc = next(c for c in (640, 512, 256, 128) if Tk % c == 0)
    hp = MLA_HEADS // 2
    return pl.pallas_call(
        functools.partial(_attn_kernel, tkc=tkc, nkc=Tk // tkc),
        grid=(B, hp, T // tq),
        in_specs=[pl.BlockSpec((1, tq, 2 * HEAD_SLAB), lambda b, h, i: (b, i, h)),
                  pl.BlockSpec((1, Tk, 2 * HEAD_SLAB), lambda b, h, i: (b, 0, h)),
                  pl.BlockSpec((1, 2 * V_HEAD, Tk), lambda b, h, i: (b, h, 0))],
        out_specs=pl.BlockSpec((1, tq, 2 * V_HEAD), lambda b, h, i: (b, i, h)),
        out_shape=jax.ShapeDtypeStruct((B, T, MLA_WIDTH), BF16),
        scratch_shapes=[pltpu.VMEM((2, 1, tq), F32), pltpu.VMEM((2, 1, tq), F32),
                        pltpu.VMEM((2 * V_HEAD, tq), F32),
                        pltpu.VMEM((2, tkc, tq), F32), pltpu.VMEM((2, tkc, tq), F32)],
        compiler_params=_cparams("parallel", "parallel", "arbitrary"),
        name="mla_attention",
    )(q, k, vt)


def _outproj_kernel(rw_ref, ft_ref, att_ref, x_ref, g_ref, lw_ref, lb_ref, w_ref, o_ref, *, alpha):
    w = w_ref
    mix = (_dot(rw_ref[0].astype(BF16), w[0:RW_WIDTH, :])
           + _dot(ft_ref[0].astype(BF16), w[RW_WIDTH:RW_WIDTH + FT_WIDTH, :])
           + _dot(att_ref[0], w[RW_WIDTH + FT_WIDTH:, :]))
    z = alpha * x_ref[0] + g_ref[0] * mix
    o_ref[0] = _standardize(z, LN_EPS) * lw_ref[...] + lb_ref[...]


def out_projection_ln(rw, ft, att, x, gate, ln_w, ln_b, w_out, alpha):
    B, T, D = x.shape
    tm = _tile(T, 256)
    tok = lambda n: pl.BlockSpec((1, tm, n), lambda b, i: (b, i, 0))
    row = pl.BlockSpec((1, D), lambda b, i: (0, 0))
    return pl.pallas_call(
        functools.partial(_outproj_kernel, alpha=alpha),
        grid=(B, T // tm),
        in_specs=[tok(RW_WIDTH), tok(FT_WIDTH), tok(MLA_WIDTH), tok(D),
                  pl.BlockSpec((1, 1, D), lambda b, i: (b, 0, 0)), row, row,
                  pl.BlockSpec(w_out.shape, lambda b, i: (0, 0))],
        out_specs=tok(D),
        out_shape=jax.ShapeDtypeStruct((B, T, D), F32),
        compiler_params=_cparams("parallel", "parallel"),
        name="out_projection_ln",
    )(rw, ft, att, x, gate, ln_w, ln_b, w_out)


def _ffn_kernel(x_ref, sh_ref, sc_ref, g_ref, lw_ref, lb_ref, w1_ref, w3_ref, w2_ref, o_ref, h_scr, acc_scr, *, alpha):
    f = pl.program_id(2)

    @pl.when(f == 0)
    def _():
        h = _standardize(x_ref[0], MOD_EPS) * (1.0 + sc_ref[0]) + sh_ref[0]
        h_scr[...] = h.astype(BF16)
        acc_scr[...] = jnp.zeros(acc_scr.shape, F32)

    hb = h_scr[...]
    a = _dot(hb, w1_ref[...])
    b = _dot(hb, w3_ref[...])
    acc_scr[...] += _dot((a * jax.nn.sigmoid(a) * b).astype(BF16), w2_ref[...])

    @pl.when(f == pl.num_programs(2) - 1)
    def _():
        z = alpha * x_ref[0] + g_ref[0] * acc_scr[...]
        o_ref[0] = _standardize(z, LN_EPS) * lw_ref[...] + lb_ref[...]


def ffn_ln(x, shift, scale, gate, ln_w, ln_b, w1, w3, w2, alpha):
    B, T, D = x.shape
    F = w1.shape[1]
    tm = _tile(T, 512)
    tf = _tile(F, 512)
    tok = pl.BlockSpec((1, tm, D), lambda b, i, f: (b, i, 0))
    vec = pl.BlockSpec((1, 1, D), lambda b, i, f: (b, 0, 0))
    row = pl.BlockSpec((1, D), lambda b, i, f: (0, 0))
    return pl.pallas_call(
        functools.partial(_ffn_kernel, alpha=alpha),
        grid=(B, T // tm, F // tf),
        in_specs=[tok, vec, vec, vec, row, row,
                  pl.BlockSpec((D, tf), lambda b, i, f: (0, f)),
                  pl.BlockSpec((D, tf), lambda b, i, f: (0, f)),
                  pl.BlockSpec((tf, D), lambda b, i, f: (f, 0))],
        out_specs=tok,
        out_shape=jax.ShapeDtypeStruct((B, T, D), F32),
        scratch_shapes=[pltpu.VMEM((tm, D), BF16), pltpu.VMEM((tm, D), F32)],
        compiler_params=_cparams("parallel", "parallel", "arbitrary"),
        name="ffn_ln",
    )(x, shift, scale, gate, ln_w, ln_b, w1, w3, w2)


def _router_kernel(x_ref, sh_ref, sc_ref, wr_ref, h_o, ti_o, tg_o):
    h = _standardize(x_ref[0], MOD_EPS) * (1.0 + sc_ref[0]) + sh_ref[0]
    h_o[0] = h.astype(BF16)
    logits = _dot(h, wr_ref[...], HI)
    lane = lax.broadcasted_iota(jnp.int32, logits.shape, 1)
    neg = jnp.float32(-jnp.inf)
    logits = jnp.where(lane < N_EXPERTS, logits, neg)
    m1 = jnp.max(logits, axis=-1, keepdims=True)
    i1 = jnp.min(jnp.where(logits == m1, lane, LANES), axis=-1, keepdims=True)
    rest = jnp.where(lane == i1, neg, logits)
    m2 = jnp.max(rest, axis=-1, keepdims=True)
    i2 = jnp.min(jnp.where(rest == m2, lane, LANES), axis=-1, keepdims=True)
    e = jnp.exp(m2 - m1)
    g1 = 1.0 / (1.0 + e)
    g2 = e / (1.0 + e)
    ti_o[0] = jnp.where(lane == 0, i1, jnp.where(lane == 1, i2, 0))
    tg_o[0] = jnp.where(lane == 0, g1, jnp.where(lane == 1, g2, 0.0))


def moe_router(x, shift, scale, wr):
    B, T, D = x.shape
    tm = _tile(T, 512)
    tok = lambda n: pl.BlockSpec((1, tm, n), lambda b, i: (b, i, 0))
    vec = pl.BlockSpec((1, 1, D), lambda b, i: (b, 0, 0))
    return pl.pallas_call(
        _router_kernel,
        grid=(B, T // tm),
        in_specs=[tok(D), vec, vec, pl.BlockSpec(wr.shape, lambda b, i: (0, 0))],
        out_specs=[tok(D), tok(LANES), tok(LANES)],
        out_shape=[jax.ShapeDtypeStruct((B, T, D), BF16), jax.ShapeDtypeStruct((B, T, LANES), jnp.int32),
                   jax.ShapeDtypeStruct((B, T, LANES), F32)],
        compiler_params=_cparams("parallel", "parallel"),
        name="moe_router",
    )(x, shift, scale, wr)


def _expert_kernel(be_ref, bv_ref, x_ref, sg_ref, w1_ref, w3_ref, w2_ref, o_ref, acc_scr):
    blk = pl.program_id(0)
    f = pl.program_id(1)
    valid = bv_ref[blk] > 0

    @pl.when(f == 0)
    def _():
        acc_scr[...] = jnp.zeros(acc_scr.shape, F32)

    @pl.when(valid)
    def _():
        xb = x_ref[...]
        a = _dot(xb, w1_ref[0])
        b = _dot(xb, w3_ref[0])
        acc_scr[...] += _dot((a * jax.nn.sigmoid(a) * b).astype(BF16), w2_ref[0])

    @pl.when(f == pl.num_programs(1) - 1)
    def _():
        o_ref[...] = acc_scr[...] * sg_ref[...]


def expert_ffn(xg, slot_gate, blk_expert, blk_valid, w1, w3, w2):
    P, D = xg.shape
    F = w1.shape[2]
    tf = _tile(F, 512)
    nblk = P // MOE_ROWS
    grid_spec = pltpu.PrefetchScalarGridSpec(
        num_scalar_prefetch=2,
        grid=(nblk, F // tf),
        in_specs=[pl.BlockSpec((MOE_ROWS, D), lambda i, f, be, bv: (i, 0)),
                  pl.BlockSpec((MOE_ROWS, 1), lambda i, f, be, bv: (i, 0)),
                  pl.BlockSpec((1, D, tf), lambda i, f, be, bv: (be[i], 0, jnp.where(bv[i] > 0, f, 0))),
                  pl.BlockSpec((1, D, tf), lambda i, f, be, bv: (be[i], 0, jnp.where(bv[i] > 0, f, 0))),
                  pl.BlockSpec((1, tf, D), lambda i, f, be, bv: (be[i], jnp.where(bv[i] > 0, f, 0), 0))],
        out_specs=pl.BlockSpec((MOE_ROWS, D), lambda i, f, be, bv: (i, 0)),
        scratch_shapes=[pltpu.VMEM((MOE_ROWS, D), F32)],
    )
    return pl.pallas_call(
        _expert_kernel,
        grid_spec=grid_spec,
        out_shape=jax.ShapeDtypeStruct((P, D), F32),
        compiler_params=_cparams("parallel", "arbitrary"),
        name="expert_ffn",
    )(blk_expert, blk_valid, xg, slot_gate, w1, w3, w2)


def _residual_ln_kernel(x_ref, y_ref, g_ref, lw_ref, lb_ref, o_ref, *, alpha):
    z = alpha * x_ref[0] + g_ref[0] * y_ref[0]
    o_ref[0] = _standardize(z, LN_EPS) * lw_ref[...] + lb_ref[...]


def residual_ln(x, y, gate, ln_w, ln_b, alpha):
    B, T, D = x.shape
    tm = _tile(T, 512)
    tok = pl.BlockSpec((1, tm, D), lambda b, i: (b, i, 0))
    row = pl.BlockSpec((1, D), lambda b, i: (0, 0))
    return pl.pallas_call(
        functools.partial(_residual_ln_kernel, alpha=alpha),
        grid=(B, T // tm),
        in_specs=[tok, tok, pl.BlockSpec((1, 1, D), lambda b, i: (b, 0, 0)), row, row],
        out_specs=tok,
        out_shape=jax.ShapeDtypeStruct((B, T, D), F32),
        compiler_params=_cparams("parallel", "parallel"),
        name="residual_ln",
    )(x, y, gate, ln_w, ln_b)


def moe_ln(x, shift, scale, gate, ln_w, ln_b, router, w1, w3, w2, alpha):
    B, T, D = x.shape
    N = B * T
    E = router.shape[1]
    wr = jnp.zeros((D, LANES), F32).at[:, :E].set(router)
    h, ti, tg = moe_router(x, shift, scale, wr)
    top_i = ti.reshape(N, LANES)[:, :2]
    top_g = tg.reshape(N, LANES)[:, :2]
    e_flat = top_i.reshape(-1)
    onehot = (e_flat[:, None] == jnp.arange(E, dtype=jnp.int32)[None, :]).astype(jnp.int32)
    rank = jnp.sum((jnp.cumsum(onehot, axis=0) - onehot) * onehot, axis=1)
    counts = jnp.sum(onehot, axis=0)
    padded = (counts + MOE_ROWS - 1) // MOE_ROWS * MOE_ROWS
    p_ends = jnp.cumsum(padded)
    p_starts = p_ends - padded
    dest = p_starts[e_flat] + rank
    P = -(-(2 * N) // MOE_ROWS) * MOE_ROWS + E * MOE_ROWS
    nblk = P // MOE_ROWS
    tok_flat = jnp.repeat(jnp.arange(N, dtype=jnp.int32), 2)
    slot_tok = jnp.zeros((P,), jnp.int32).at[dest].set(tok_flat)
    slot_gate = jnp.zeros((P,), F32).at[dest].set(top_g.reshape(-1))
    blk_start = jnp.arange(nblk, dtype=jnp.int32) * MOE_ROWS
    blk_expert = jnp.minimum(jnp.searchsorted(p_ends, blk_start, side="right"), E - 1).astype(jnp.int32)
    blk_valid = (blk_start < p_ends[-1]).astype(jnp.int32)
    xg = h.reshape(N, D)[slot_tok]
    yg = expert_ffn(xg, slot_gate[:, None], blk_expert, blk_valid, w1, w3, w2)
    d2 = dest.reshape(N, 2)
    y = (yg[d2[:, 0]] + yg[d2[:, 1]]).reshape(B, T, D)
    return residual_ln(x, y, gate, ln_w, ln_b, alpha)


def _rope_partner(w):
    half = ROPE_AXIS // 2
    idx = np.arange(QK_ROPE)
    first = (idx % ROPE_AXIS) < half
    src = np.where(first, idx + half, idx - half)
    sign = np.where(first, -1.0, 1.0).astype(np.float32)
    return w[:, src] * sign


def _layer_params(l, w_in, rw_conv, rw_w0, rw_w_up, rw_a0, rw_a_up, rw_g_up, rw_k_k, rw_k_a, rw_r_k, rw_gn_w,
                  rw_gn_b, mla_q_norm, mla_w_uq, mla_kv_norm, mla_w_ukv, w_out):
    D = w_in.shape[1]
    wi = w_in[l]
    o = np.cumsum([0, 3 * RW_WIDTH, G_RANK, LORA_RANK, LORA_RANK, LORA_RANK, LORA_RANK, FT_WIDTH, Q_RANK, KV_RANK, QK_ROPE])
    piece = lambda i: wi[:, o[i]:o[i + 1]]
    zpad = lambda n: jnp.zeros((D, n), F32)
    lora = [jnp.concatenate([piece(i), zpad(LANES - LORA_RANK)], axis=1) for i in (2, 3, 4, 5)]
    w_rw = jnp.concatenate([piece(0), piece(1)] + lora, axis=1)
    kr = piece(9)
    w_ckv = jnp.concatenate([piece(8), zpad(QK_NOPE), kr, _rope_partner(kr)], axis=1)
    pad_rows = lambda a: jnp.concatenate([a, jnp.zeros((a.shape[0], LANES - LORA_RANK, a.shape[2]), F32)], axis=1)
    head = jnp.arange(RW_WIDTH) // RW_HEAD
    uq = mla_w_uq[l].reshape(Q_RANK, MLA_HEADS, QK_NOPE + QK_ROPE)
    uq_rope = uq[:, :, QK_NOPE:]
    uq_partner = _rope_partner(uq_rope.reshape(Q_RANK * MLA_HEADS, QK_ROPE)).reshape(Q_RANK, MLA_HEADS, QK_ROPE)
    w_q = jnp.concatenate([uq, uq_partner], axis=2).reshape(Q_RANK, MLA_HEADS * HEAD_SLAB)
    ukv = mla_w_ukv[l].reshape(KV_RANK, MLA_HEADS, QK_NOPE + V_HEAD)
    w_k = jnp.concatenate([ukv[:, :, :QK_NOPE], jnp.zeros((KV_RANK, MLA_HEADS, HEAD_SLAB - QK_NOPE), F32)],
                          axis=2).reshape(KV_RANK, MLA_HEADS * HEAD_SLAB)
    w_v = ukv[:, :, QK_NOPE:].reshape(KV_RANK, MLA_WIDTH)
    return {
        "w_in": [w.astype(BF16) for w in (w_rw, piece(6), piece(7), w_ckv)],
        "rw": {"conv": rw_conv[l], "k_k": rw_k_k[l][None], "k_a": rw_k_a[l][None], "r_k": rw_r_k[l].reshape(1, RW_WIDTH),
               "w0": rw_w0[l], "a0": rw_a0[l], "w_up": pad_rows(rw_w_up[l]), "a_up": pad_rows(rw_a_up[l]),
               "g_up": rw_g_up[l], "gn_w": rw_gn_w[l][None], "gn_b": rw_gn_b[l][None],
               "E": (head[:, None] == head[None, :]).astype(F32)},
        "q_norm": mla_q_norm[l][None], "w_q": w_q.astype(BF16),
        "kv_norm": mla_kv_norm[l][None], "w_k": w_k.astype(BF16), "w_vt": w_v.T.astype(BF16),
        "w_out": w_out[l].astype(BF16),
    }


def _rope_tables(T, use_rope):
    ones = jnp.ones((T, QK_NOPE), F32)
    zeros = jnp.zeros((T, QK_NOPE), F32)
    zpad = jnp.zeros((T, HEAD_SLAB - QK_NOPE - QK_ROPE), F32)
    if use_rope:
        row = jnp.repeat(jnp.arange(T // GRID_W), GRID_W).astype(F32)
        col = (jnp.arange(T) % GRID_W).astype(F32)
        inv = ROPE_THETA ** (-jnp.arange(0, ROPE_AXIS, 2, dtype=F32) / ROPE_AXIS)
        ang = jnp.stack([row[:, None] * inv, col[:, None] * inv], axis=1)
        ang = jnp.broadcast_to(ang[:, :, None, :], (T, 2, 2, ROPE_AXIS // 2)).reshape(T, QK_ROPE)
        cos, sin = jnp.cos(ang), jnp.sin(ang)
    else:
        cos, sin = jnp.ones((T, QK_ROPE), F32), jnp.zeros((T, QK_ROPE), F32)
    return jnp.concatenate([ones, cos, zpad], axis=1), jnp.concatenate([zeros, sin, zpad], axis=1)


def _mixer(h_pieces_lat, h_pieces_ctx, p, tabs_lat, tabs_ctx, emit_ctx):
    rw_l, ft_l, cq_l, ckv_l = h_pieces_lat
    rw_c, ft_c, cq_c, ckv_c = h_pieces_ctx
    rwo_l, rwo_c = rwkv_branch(rw_l, rw_c, p["rw"], emit_ctx)
    fto_l = fourier_mixer(ft_l)
    q_l = q_projection(cq_l, p["q_norm"], p["w_q"], *tabs_lat)
    k_l, vt_l = kv_projection(ckv_l, p["kv_norm"], p["w_k"], p["w_vt"], *tabs_lat)
    k_c, vt_c = kv_projection(ckv_c, p["kv_norm"], p["w_k"], p["w_vt"], *tabs_ctx)
    att_l = attention(q_l, jnp.concatenate([k_l, k_c], axis=1), jnp.concatenate([vt_l, vt_c], axis=2))
    out_c = None
    if emit_ctx:
        fto_c = fourier_mixer(ft_c)
        q_c = q_projection(cq_c, p["q_norm"], p["w_q"], *tabs_ctx)
        att_c = attention(q_c, k_c, vt_c)
        out_c = (rwo_c, fto_c, att_c)
    return (rwo_l, fto_l, att_l), out_c


def kernel(x, c, ctx, c_ctx, ada_w, ada_b, w_in, rw_conv, rw_w0, rw_w_up, rw_a0, rw_a_up, rw_g_up, rw_k_k, rw_k_a,
           rw_r_k, rw_gn_w, rw_gn_b, mla_q_norm, mla_w_uq, mla_kv_norm, mla_w_ukv, w_out, ln1_w, ln1_b, ln2_w, ln2_b,
           ffn_w1, ffn_w3, ffn_w2, moe_router, moe_w1, moe_w3, moe_w2):
    B, T, D = x.shape
    Tc = ctx.shape[1]
    depth = w_in.shape[0]
    alpha = (2 * depth) ** 0.25
    assert B + 1 <= SUBLANES
    cc = jnp.zeros((SUBLANES, D), F32).at[:B].set(c).at[B].set(c_ctx)
    ada = ada_vectors(cc, ada_w, ada_b)
    tabs_lat = _rope_tables(T, True)
    tabs_ctx = _rope_tables(Tc, False)
    for l in range(depth):
        last = l == depth - 1
        p = _layer_params(l, w_in, rw_conv, rw_w0, rw_w_up, rw_a0, rw_a_up, rw_g_up, rw_k_k, rw_k_a, rw_r_k,
                          rw_gn_w, rw_gn_b, mla_q_norm, mla_w_uq, mla_kv_norm, mla_w_ukv, w_out)
        mods = ada[l].reshape(SUBLANES, 6, D)
        lat = [mods[:B, j][:, None, :] for j in range(6)]
        cx = [jnp.broadcast_to(mods[B, j][None, None, :], (B, 1, D)) for j in range(6)]
        sh_m, sc_m, g_m, sh_f, sc_f, g_f = lat
        csh_m, csc_m, cg_m, csh_f, csc_f, cg_f = cx
        ln1 = (ln1_w[l][None], ln1_b[l][None])
        ln2 = (ln2_w[l][None], ln2_b[l][None])

        pieces_l = in_projection(x, sh_m, sc_m, p["w_in"])
        pieces_c = in_projection(ctx, csh_m, csc_m, p["w_in"])
        mix_l, mix_c = _mixer(pieces_l, pieces_c, p, tabs_lat, tabs_ctx, not last)
        x = out_projection_ln(*mix_l, x, g_m, *ln1, p["w_out"], alpha)
        i = l // 2
        if l % 2 == 0:
            dense = (ffn_w1[i].astype(BF16), ffn_w3[i].astype(BF16), ffn_w2[i].astype(BF16))
            x = ffn_ln(x, sh_f, sc_f, g_f, *ln2, *dense, alpha)
        else:
            x = moe_ln(x, sh_f, sc_f, g_f, *ln2, moe_router[i], moe_w1[i].astype(BF16), moe_w3[i].astype(BF16),
                       moe_w2[i].astype(BF16), alpha)
        if not last:
            ctx = out_projection_ln(*mix_c, ctx, cg_m, *ln1, p["w_out"], alpha)
            if l % 2 == 0:
                ctx = ffn_ln(ctx, csh_f, csc_f, cg_f, *ln2, *dense, alpha)
            else:
                ctx = moe_ln(ctx, csh_f, csc_f, cg_f, *ln2, moe_router[i], moe_w1[i].astype(BF16),
                             moe_w3[i].astype(BF16), moe_w2[i].astype(BF16), alpha)
    return x
```

```python
import functools
import math

import numpy as np
import jax
import jax.numpy as jnp
from jax import lax
from jax.experimental import pallas as pl
from jax.experimental.pallas import tpu as pltpu

F32 = jnp.float32
BF16 = jnp.bfloat16
HI = lax.Precision.HIGHEST

LANES = 128
SUBLANES = 8
VMEM_LIMIT = 56 * 1024 * 1024

GRID_W = 64
RW_HEADS = 8
RW_HEAD = 64
RW_WIDTH = RW_HEADS * RW_HEAD
G_RANK = 128
LORA_RANK = 64
GN_EPS = 64e-5
FT_GROUP = 64
FT_WIDTH = 512
MLA_HEADS = 16
QK_NOPE = 64
QK_ROPE = 32
V_HEAD = 64
MLA_WIDTH = MLA_HEADS * V_HEAD
Q_RANK = 512
KV_RANK = 256
ROPE_AXIS = QK_ROPE // 2
ROPE_THETA = 10000.0
ATTN_SCALE = (QK_NOPE + QK_ROPE) ** -0.5
Q_SCALE = ATTN_SCALE * math.log2(math.e)
N_EXPERTS = 8
LN_EPS = 1e-5
MOD_EPS = 1e-6
RMS_EPS = 1e-6
CHUNK = 64
INV_BLOCK = 16
MOE_ROWS = 1024

NT = (((1,), (1,)), ((), ()))
TN = (((0,), (0,)), ((), ()))


def _cparams(*sem):
    return pltpu.CompilerParams(dimension_semantics=sem, vmem_limit_bytes=VMEM_LIMIT)


def _dot(a, b, prec=None):
    return jnp.dot(a, b, precision=prec, preferred_element_type=F32)


def _dot_split(x, w):
    hi = x.astype(BF16)
    r1 = x - hi.astype(F32)
    mid = r1.astype(BF16)
    lo = (r1 - mid.astype(F32)).astype(BF16)
    return _dot(hi, w) + _dot(mid, w) + _dot(lo, w)


def _standardize(x, eps):
    mu = jnp.mean(x, axis=-1, keepdims=True)
    xc = x - mu
    var = jnp.mean(xc * xc, axis=-1, keepdims=True)
    return xc * lax.rsqrt(var + eps)


def _tile(n, pref):
    t = min(n, pref)
    assert n % t == 0, (n, pref)
    return t


def _ada_kernel(c_ref, w_ref, b_ref, o_ref):
    c = c_ref[...]
    s = c * jax.nn.sigmoid(c)
    o_ref[0] = _dot(s, w_ref[0], HI) + b_ref[0]


def ada_vectors(cc, ada_w, ada_b):
    L, D, N6 = ada_w.shape
    tn = _tile(N6, 1024)
    return pl.pallas_call(
        _ada_kernel,
        grid=(L, N6 // tn),
        in_specs=[pl.BlockSpec((SUBLANES, D), lambda l, j: (0, 0)),
                  pl.BlockSpec((1, D, tn), lambda l, j: (l, 0, j)),
                  pl.BlockSpec((1, 1, tn), lambda l, j: (l, 0, j))],
        out_specs=pl.BlockSpec((1, SUBLANES, tn), lambda l, j: (l, 0, j)),
        out_shape=jax.ShapeDtypeStruct((L, SUBLANES, N6), F32),
        compiler_params=_cparams("parallel", "parallel"),
        name="ada_vectors",
    )(cc, ada_w, ada_b.reshape(L, 1, N6))


def _inproj_kernel(x_ref, sh_ref, sc_ref, *refs):
    nw = len(refs) // 2
    h = _standardize(x_ref[0], MOD_EPS) * (1.0 + sc_ref[0]) + sh_ref[0]
    hb = h.astype(BF16)
    for w_ref, o_ref in zip(refs[:nw], refs[nw:]):
        o_ref[0] = _dot(hb, w_ref[...])


def in_projection(x, shift, scale, weights):
    B, T, D = x.shape
    tm = _tile(T, 256)
    vec = pl.BlockSpec((1, 1, D), lambda b, i: (b, 0, 0))
    return pl.pallas_call(
        _inproj_kernel,
        grid=(B, T // tm),
        in_specs=[pl.BlockSpec((1, tm, D), lambda b, i: (b, i, 0)), vec, vec]
        + [pl.BlockSpec(w.shape, lambda b, i: (0, 0)) for w in weights],
        out_specs=[pl.BlockSpec((1, tm, w.shape[1]), lambda b, i: (b, i, 0)) for w in weights],
        out_shape=[jax.ShapeDtypeStruct((B, T, w.shape[1]), F32) for w in weights],
        compiler_params=_cparams("parallel", "parallel"),
        name="in_projection",
    )(x, shift, scale, *weights)


RW_COLS = 3 * RW_WIDTH + G_RANK + 4 * LANES


def _softplus(z):
    return jnp.maximum(z, 0.0) + jnp.log(1.0 + jnp.exp(-jnp.abs(z)))


def _rwkv_prep_kernel(x_ref, xp_ref, xn_ref, conv_ref, kk_ref, ka_ref, rk_ref, w0_ref, a0_ref,
                      wup_ref, aup_ref, gup_ref, e_ref,
                      r_o, v_o, kk_o, ld0_o, b0_o, kr0_o, ld1_o, b1_o, kr1_o, bonus_o, gate_o):
    i = pl.program_id(1)
    n = pl.num_programs(1)
    x = x_ref[0]
    W3 = 3 * RW_WIDTH
    raw = x[:, :W3]
    tm = raw.shape[0]
    row = lax.broadcasted_iota(jnp.int32, (tm, 1), 0)
    prev_row = jnp.where(i > 0, xp_ref[0, SUBLANES - 1:SUBLANES, :], 0.0)
    next_row = jnp.where(i < n - 1, xn_ref[0, 0:1, :], 0.0)
    xm = jnp.where(row == 0, prev_row, pltpu.roll(raw, 1, 0))
    xq = jnp.where(row == tm - 1, next_row, pltpu.roll(raw, tm - 1, 0))
    cw = conv_ref[...]
    y = xm * cw[0:1] + raw * cw[1:2] + xq * cw[2:3]
    r = y[:, :RW_WIDTH]
    k = y[:, RW_WIDTH:2 * RW_WIDTH]
    v = y[:, 2 * RW_WIDTH:W3]
    E = e_ref[...]
    kkv = k * kk_ref[...]
    kk = kkv / jnp.maximum(jnp.sqrt(_dot_split(kkv * kkv, E)), 1e-12)
    r_o[0] = r
    v_o[0] = v
    kk_o[0] = kk
    g_dn = x[:, W3:W3 + G_RANK]
    gate_o[0] = _dot(jax.nn.sigmoid(g_dn).astype(BF16), gup_ref[...])
    bonus = jnp.zeros_like(r)
    outs = ((ld0_o, b0_o, kr0_o), (ld1_o, b1_o, kr1_o))
    for d in range(2):
        base = W3 + G_RANK
        w_dn = x[:, base + d * LANES: base + (d + 1) * LANES]
        a_dn = x[:, base + (2 + d) * LANES: base + (3 + d) * LANES]
        z = w0_ref[d:d + 1, :] + _dot(jnp.tanh(w_dn).astype(BF16), wup_ref[d])
        logw = -_softplus(-z) - 0.5
        a = jax.nn.sigmoid(a0_ref[d:d + 1, :] + _dot(a_dn.astype(BF16), aup_ref[d]))
        kr = k * (1.0 + (a - 1.0) * ka_ref[...])
        ld_o, b_o, kr_o = outs[d]
        ld_o[0] = -jnp.exp(logw)
        b_o[0] = a * kk
        kr_o[0] = kr
        bonus = bonus + _dot_split(r * kr * rk_ref[...], E) * v
    bonus_o[0] = bonus


def rwkv_prep(rw, p):
    B, T, _ = rw.shape
    tm = _tile(T, 256)
    nh = tm // SUBLANES
    last = T // SUBLANES - 1
    W3 = 3 * RW_WIDTH
    full = lambda a: pl.BlockSpec(a.shape, lambda b, i: (0,) * a.ndim)
    params = [p["conv"], p["k_k"], p["k_a"], p["r_k"], p["w0"], p["a0"], p["w_up"], p["a_up"], p["g_up"], p["E"]]
    outs = pl.pallas_call(
        _rwkv_prep_kernel,
        grid=(B, T // tm),
        in_specs=[pl.BlockSpec((1, tm, RW_COLS), lambda b, i: (b, i, 0)),
                  pl.BlockSpec((1, SUBLANES, W3), lambda b, i: (b, jnp.maximum(i * nh - 1, 0), 0)),
                  pl.BlockSpec((1, SUBLANES, W3), lambda b, i: (b, jnp.minimum((i + 1) * nh, last), 0))]
        + [full(a) for a in params],
        out_specs=[pl.BlockSpec((1, tm, RW_WIDTH), lambda b, i: (b, i, 0))] * 11,
        out_shape=[jax.ShapeDtypeStruct((B, T, RW_WIDTH), F32)] * 11,
        compiler_params=_cparams("parallel", "parallel"),
        name="rwkv_prep",
    )(rw, rw, rw, *params)
    return outs


PAIR = 2 * RW_HEAD
N_PAIRS = RW_HEADS // 2
CHUNK_GROUP = 4


def _pair_masks(reverse):
    i = lax.broadcasted_iota(jnp.int32, (PAIR, PAIR), 0)
    j = lax.broadcasted_iota(jnp.int32, (PAIR, PAIR), 1)
    same = (i // CHUNK) == (j // CHUNK)
    strict = same & ((j > i) if reverse else (j < i))
    incl = same & ((j >= i) if reverse else (j <= i))
    blk = (i // INV_BLOCK) == (j // INV_BLOCK)
    eye = jnp.where(i == j, 1.0, 0.0).astype(F32)
    t = lax.broadcasted_iota(jnp.int32, (CHUNK, CHUNK), 0)
    u = lax.broadcasted_iota(jnp.int32, (CHUNK, CHUNK), 1)
    tri = jnp.where((u >= t) if reverse else (u <= t), 1.0, 0.0).astype(F32)
    first = lax.broadcasted_iota(jnp.int32, (1, PAIR), 1) < RW_HEAD
    return strict, incl, blk, eye, tri, first


def _pair_chunk_math(ld, r, v, kk, b, kr, masks):
    strict, incl, blk, eye, tri, first = masks
    P = ld.shape[0]
    ein = lambda spec, a, c, prec=None: jnp.einsum(spec, a, c, precision=prec, preferred_element_type=F32)
    Lc = ein("pct,ptk->pck", jnp.broadcast_to(tri, (P, CHUNK, CHUNK)), ld, HI)
    Lx = Lc - ld
    Lt = jnp.sum(ld, axis=1, keepdims=True)
    ginv = jnp.exp(-Lc)
    gout = jnp.exp(Lt - Lc)
    stack = lambda x: jnp.concatenate([jnp.where(first, x, 0.0), jnp.where(first, 0.0, x)], axis=1).astype(BF16)
    twice = lambda x: jnp.concatenate([x, x], axis=1).astype(BF16)
    Xk = stack(kk * jnp.exp(Lx))
    Xr = stack(r * jnp.exp(Lc))
    Vs = stack(v)
    Bs = stack(b * gout)
    Ks = stack(kr * gout)
    mm = lambda a, c: ein("pij,pjk->pik", a.astype(BF16), c.astype(BF16))
    nt = lambda a, c: ein("pik,pjk->pij", a, c)
    tn = lambda a, c: ein("pji,pjk->pik", a, c.astype(BF16))
    XX = jnp.concatenate([Xk, Xr], axis=1)
    Mb = nt(XX, twice(b * ginv))
    Mk = nt(XX, twice(kr * ginv))
    Mab = jnp.where(strict, Mb[:, :PAIR], 0.0)
    Arb = jnp.where(incl, Mb[:, PAIR:], 0.0)
    Mak = jnp.where(strict, Mk[:, :PAIR], 0.0)
    Ark = jnp.where(incl, Mk[:, PAIR:], 0.0)
    Nd = jnp.where(blk, Mab, 0.0)
    No = Mab - Nd
    N2 = mm(Nd, Nd)
    N4 = mm(N2, N2)
    N8 = mm(N4, N4)
    Td = mm(mm(mm(eye - Nd, eye + N2), eye + N4), eye + N8)
    M2 = mm(Td, No)
    Tm = mm(mm(eye - M2, eye + mm(M2, M2)), Td)
    P1 = mm(Tm, Xk)
    P2 = mm(Tm, mm(Mak, Vs))
    Q1s = Xr.astype(F32) - mm(Arb, P1)
    Yis = mm(Ark, Vs) - mm(Arb, P2)
    G = eye * jnp.exp(Lt) - tn(Bs, P1)
    H = tn(Ks, Vs) - tn(Bs, P2)
    return Q1s[:, :CHUNK] + Q1s[:, CHUNK:], Yis[:, :CHUNK] + Yis[:, CHUNK:], G, H


def _rwkv_chunk_kernel(ld_ref, r_ref, v_ref, kk_ref, b_ref, kr_ref, q_o, yi_o, g_o, h_o, *, nch, reverse):
    masks = _pair_masks(reverse)
    group = min(CHUNK_GROUP, nch)

    def body(i, carry):
        c0 = i * group
        rows = [pl.ds(pl.multiple_of((c0 + u) * CHUNK, CHUNK), CHUNK) for u in range(group)]
        get = lambda ref: jnp.stack([ref[0, rows[u], j * PAIR:(j + 1) * PAIR]
                                     for u in range(group) for j in range(N_PAIRS)])
        Q1, Yi, G, H = _pair_chunk_math(get(ld_ref), get(r_ref), get(v_ref), get(kk_ref), get(b_ref), get(kr_ref),
                                        masks)
        for u in range(group):
            for j in range(N_PAIRS):
                q_o[0, rows[u], j * PAIR:(j + 1) * PAIR] = Q1[u * N_PAIRS + j]
                yi_o[0, rows[u], j * PAIR:(j + 1) * PAIR] = Yi[u * N_PAIRS + j]
        mats = pl.ds(c0 * N_PAIRS, group * N_PAIRS)
        g_o[0, mats] = G
        h_o[0, mats] = H
        return carry

    lax.fori_loop(0, nch // group, body, 0)


def rwkv_chunks(ld, r, v, kk, b, kr, reverse):
    B, T, W = ld.shape
    tt = _tile(T, 4 * CHUNK)
    nch = tt // CHUNK
    tok = pl.BlockSpec((1, tt, W), lambda bb, i: (bb, i, 0))
    mat = pl.BlockSpec((1, nch * N_PAIRS, PAIR, PAIR), lambda bb, i: (bb, i, 0, 0))
    mats = jax.ShapeDtypeStruct((B, T // CHUNK * N_PAIRS, PAIR, PAIR), F32)
    return pl.pallas_call(
        functools.partial(_rwkv_chunk_kernel, nch=nch, reverse=reverse),
        grid=(B, T // tt),
        in_specs=[tok] * 6,
        out_specs=[tok, tok, mat, mat],
        out_shape=[jax.ShapeDtypeStruct((B, T, W), F32)] * 2 + [mats, mats],
        compiler_params=_cparams("parallel", "parallel"),
        name="rwkv_chunks_bwd" if reverse else "rwkv_chunks_fwd",
    )(ld, r, v, kk, b, kr)


def _rwkv_seq_kernel(q_ref, yi_ref, g_ref, h_ref, s0_ref, y_o, sf_o, s_scr, *, nch, reverse):
    @pl.when(pl.program_id(1) == 0)
    def _():
        s_scr[...] = s0_ref[0]

    def body(cc, carry):
        c = (nch - 1 - cc) if reverse else cc
        rows = pl.ds(pl.multiple_of(c * CHUNK, CHUNK), CHUNK)
        for j in range(N_PAIRS):
            lanes = slice(j * PAIR, (j + 1) * PAIR)
            S = s_scr[j]
            y_o[0, rows, lanes] = _dot(q_ref[0, rows, lanes], S, HI) + yi_ref[0, rows, lanes]
            s_scr[j] = _dot(g_ref[0, c * N_PAIRS + j], S, HI) + h_ref[0, c * N_PAIRS + j]
        return carry

    lax.fori_loop(0, nch, body, 0)
    sf_o[0] = s_scr[...]


def rwkv_sequential(q1, yi, g, hm, s0, reverse):
    B, T, W = q1.shape
    tt = _tile(T, 8 * CHUNK)
    n = T // tt
    nch = tt // CHUNK
    step = (lambda i: n - 1 - i) if reverse else (lambda i: i)
    tok = pl.BlockSpec((1, tt, W), lambda bb, i: (bb, step(i), 0))
    mat = pl.BlockSpec((1, nch * N_PAIRS, PAIR, PAIR), lambda bb, i: (bb, step(i), 0, 0))
    sspec = pl.BlockSpec((1, N_PAIRS, PAIR, PAIR), lambda bb, i: (bb, 0, 0, 0))
    return pl.pallas_call(
        functools.partial(_rwkv_seq_kernel, nch=nch, reverse=reverse),
        grid=(B, n),
        in_specs=[tok, tok, mat, mat, sspec],
        out_specs=[tok, sspec],
        out_shape=[jax.ShapeDtypeStruct((B, T, W), F32), jax.ShapeDtypeStruct((B, N_PAIRS, PAIR, PAIR), F32)],
        scratch_shapes=[pltpu.VMEM((N_PAIRS, PAIR, PAIR), F32)],
        compiler_params=_cparams("parallel", "arbitrary"),
        name="rwkv_seq_bwd" if reverse else "rwkv_seq_fwd",
    )(q1, yi, g, hm, s0)


def _rwkv_readout_kernel(yf_ref, yb_ref, bonus_ref, gate_ref, gw_ref, gb_ref, e_ref, o_ref):
    E = e_ref[...]
    ys = yf_ref[0] + yb_ref[0]
    yc = ys - _dot_split(ys, E) * (1.0 / RW_HEAD)
    var = _dot_split(yc * yc, E) * (1.0 / RW_HEAD)
    yn = yc * lax.rsqrt(var + GN_EPS) * gw_ref[...] + gb_ref[...]
    o_ref[0] = (yn + bonus_ref[0]) * gate_ref[0]


def rwkv_readout(yf, yb, bonus, gate, p):
    B, T, W = yf.shape
    tm = _tile(T, 512)
    spec = pl.BlockSpec((1, tm, W), lambda b, i: (b, i, 0))
    full = lambda a: pl.BlockSpec(a.shape, lambda b, i: (0,) * a.ndim)
    return pl.pallas_call(
        _rwkv_readout_kernel,
        grid=(B, T // tm),
        in_specs=[spec] * 4 + [full(p["gn_w"]), full(p["gn_b"]), full(p["E"])],
        out_specs=spec,
        out_shape=jax.ShapeDtypeStruct((B, T, W), F32),
        compiler_params=_cparams("parallel", "parallel"),
        name="rwkv_readout",
    )(yf, yb, bonus, gate, p["gn_w"], p["gn_b"], p["E"])


def rwkv_branch(rw_lat, rw_ctx, p, emit_ctx):
    prep_l = rwkv_prep(rw_lat, p)
    prep_c = rwkv_prep(rw_ctx, p)
    B = rw_lat.shape[0]
    s_zero = jnp.zeros((B, N_PAIRS, PAIR, PAIR), F32)

    def scans(prep, s0s):
        r, v, kk = prep[:3]
        ys, finals = [], []
        for d in range(2):
            ld, b, kr = prep[3 + 3 * d: 6 + 3 * d]
            q1, yi, g, hm = rwkv_chunks(ld, r, v, kk, b, kr, reverse=bool(d))
            y, sf = rwkv_sequential(q1, yi, g, hm, s0s[d], reverse=bool(d))
            ys.append(y)
            finals.append(sf)
        return ys, finals

    ys_c, fin_c = scans(prep_c, (s_zero, s_zero))
    ys_l, _ = scans(prep_l, fin_c)
    out_l = rwkv_readout(ys_l[0], ys_l[1], prep_l[9], prep_l[10], p)
    out_c = rwkv_readout(ys_c[0], ys_c[1], prep_c[9], prep_c[10], p) if emit_ctx else None
    return out_l, out_c


def _dft_mats(n):
    a = 2.0 * np.pi * np.outer(np.arange(n), np.arange(n)) / n
    return np.cos(a), np.sin(a)


def _fft1_kernel(u_ref, c_ref, s_ref, twc_ref, tws_ref, ar_o, ai_o, *, tn2, ch):
    U = u_ref[0]
    Ar = _dot(c_ref[...], U, HI)
    Ai = -_dot(s_ref[...], U, HI)
    twc = twc_ref[0]
    tws = tws_ref[0]
    for j in range(tn2):
        ct = twc[:, j:j + 1]
        st = tws[:, j:j + 1]
        a_r = Ar[:, j * ch:(j + 1) * ch]
        a_i = Ai[:, j * ch:(j + 1) * ch]
        ar_o[0, j] = a_r * ct + a_i * st
        ai_o[0, j] = a_i * ct - a_r * st


def _fft2_kernel(ar_ref, ai_ref, c_ref, s_ref, cc_ref, sc_ref, o_ref):
    Ar = ar_ref[0]
    Ai = ai_ref[0]
    C = c_ref[...]
    S = s_ref[...]
    Yr = _dot(C, Ar, HI) + _dot(S, Ai, HI)
    Yi = _dot(C, Ai, HI) - _dot(S, Ar, HI)
    Cc = cc_ref[...]
    Sc = sc_ref[...]
    for m in range(Ar.shape[1] // LANES):
        sl = slice(m * LANES, (m + 1) * LANES)
        o_ref[0, :, sl] = _dot(Yr[:, sl], Cc, HI) + _dot(Yi[:, sl], Sc, HI)


def fourier_mixer(u):
    B, T, ch = u.shape
    lg = int(round(math.log2(T)))
    assert 1 << lg == T
    N1 = 1 << ((lg + 1) // 2)
    N2 = T // N1
    c1, s1 = _dft_mats(N1)
    c2, s2 = _dft_mats(N2)
    tw = 2.0 * np.pi * np.outer(np.arange(N1), np.arange(N2)) / T
    tn2 = min(SUBLANES, N2)
    nj = N2 // tn2
    twc = np.cos(tw).reshape(N1, nj, tn2).transpose(1, 0, 2)
    tws = np.sin(tw).reshape(N1, nj, tn2).transpose(1, 0, 2)
    cg, sg = _dft_mats(FT_GROUP)
    scale = 1.0 / math.sqrt(T * FT_GROUP)
    eye2 = np.eye(LANES // FT_GROUP)
    cc = np.kron(eye2, cg) * scale
    sc = np.kron(eye2, sg) * scale
    f = lambda a: jnp.asarray(a, F32)
    full2 = lambda n, m: pl.BlockSpec((n, m), lambda b, j: (0, 0))

    ar, ai = pl.pallas_call(
        functools.partial(_fft1_kernel, tn2=tn2, ch=ch),
        grid=(B, nj),
        in_specs=[pl.BlockSpec((1, N1, tn2 * ch), lambda b, j: (b, 0, j)),
                  full2(N1, N1), full2(N1, N1),
                  pl.BlockSpec((1, N1, tn2), lambda b, j: (j, 0, 0)),
                  pl.BlockSpec((1, N1, tn2), lambda b, j: (j, 0, 0))],
        out_specs=[pl.BlockSpec((1, tn2, N1, ch), lambda b, j: (b, j, 0, 0))] * 2,
        out_shape=[jax.ShapeDtypeStruct((B, N2, N1, ch), F32)] * 2,
        compiler_params=_cparams("parallel", "parallel"),
        name="fft_stage1",
    )(u.reshape(B, N1, N2 * ch), f(c1), f(s1), f(twc), f(tws))

    tk1 = min(SUBLANES, N1)
    blk = pl.BlockSpec((1, N2, tk1 * ch), lambda b, j: (b, 0, j))
    out = pl.pallas_call(
        _fft2_kernel,
        grid=(B, N1 // tk1),
        in_specs=[blk, blk, full2(N2, N2), full2(N2, N2), full2(LANES, LANES), full2(LANES, LANES)],
        out_specs=blk,
        out_shape=jax.ShapeDtypeStruct((B, N2, N1 * ch), F32),
        compiler_params=_cparams("parallel", "parallel"),
        name="fft_stage2",
    )(ar.reshape(B, N2, N1 * ch), ai.reshape(B, N2, N1 * ch), f(c2), f(s2), f(cc), f(sc))
    return out.reshape(B, T, ch)


HEAD_SLAB = LANES
ROPE_SHIFT = HEAD_SLAB - QK_ROPE
ONES_ROWS = 16
V_ROWS = V_HEAD + ONES_ROWS


def _rms(x, w):
    return x * lax.rsqrt(jnp.mean(x * x, axis=-1, keepdims=True) + RMS_EPS) * w


def _qproj_kernel(cq_ref, nw_ref, w_ref, ct_ref, st_ref, q_o):
    q = _dot(_rms(cq_ref[0], nw_ref[...]).astype(BF16), w_ref[...])
    ct = ct_ref[...]
    st = st_ref[...]
    for h in range(MLA_HEADS):
        sl = slice(h * HEAD_SLAB, (h + 1) * HEAD_SLAB)
        s = q[:, sl]
        q_o[0, :, sl] = ((s * ct + pltpu.roll(s, ROPE_SHIFT, 1) * st) * Q_SCALE).astype(BF16)


def _kvproj_kernel(ckv_ref, nw_ref, wk_ref, wvt_ref, ct_ref, st_ref, k_o, vt_o):
    x = ckv_ref[0]
    n = _rms(x[:, :KV_RANK], nw_ref[...]).astype(BF16)
    rs = x[:, KV_RANK:KV_RANK + HEAD_SLAB]
    rope = rs * ct_ref[...] + pltpu.roll(rs, ROPE_SHIFT, 1) * st_ref[...]
    kn = _dot(n, wk_ref[...])
    for h in range(MLA_HEADS):
        sl = slice(h * HEAD_SLAB, (h + 1) * HEAD_SLAB)
        k_o[0, :, sl] = (kn[:, sl] + rope).astype(BF16)
    vt = lax.dot_general(wvt_ref[...], n, NT, preferred_element_type=F32)
    row = lax.broadcasted_iota(jnp.int32, vt.shape, 0)
    vt_o[0] = jnp.where(row % V_ROWS >= V_HEAD, 1.0, vt).astype(BF16)


def q_projection(cq, nw, w, ct, st):
    B, T, R = cq.shape
    tm = _tile(T, 256)
    W = MLA_HEADS * HEAD_SLAB
    tab = pl.BlockSpec((tm, HEAD_SLAB), lambda b, i: (i, 0))
    return pl.pallas_call(
        _qproj_kernel,
        grid=(B, T // tm),
        in_specs=[pl.BlockSpec((1, tm, R), lambda b, i: (b, i, 0)),
                  pl.BlockSpec(nw.shape, lambda b, i: (0, 0)),
                  pl.BlockSpec(w.shape, lambda b, i: (0, 0)), tab, tab],
        out_specs=pl.BlockSpec((1, tm, W), lambda b, i: (b, i, 0)),
        out_shape=jax.ShapeDtypeStruct((B, T, W), BF16),
        compiler_params=_cparams("parallel", "parallel"),
        name="q_projection",
    )(cq, nw, w, ct, st)


def kv_projection(ckv, nw, wk, wvt, ct, st):
    B, T, R = ckv.shape
    tm = _tile(T, 256)
    W = MLA_HEADS * HEAD_SLAB
    tab = pl.BlockSpec((tm, HEAD_SLAB), lambda b, i: (i, 0))
    return pl.pallas_call(
        _kvproj_kernel,
        grid=(B, T // tm),
        in_specs=[pl.BlockSpec((1, tm, R), lambda b, i: (b, i, 0)),
                  pl.BlockSpec(nw.shape, lambda b, i: (0, 0)),
                  pl.BlockSpec(wk.shape, lambda b, i: (0, 0)),
                  pl.BlockSpec(wvt.shape, lambda b, i: (0, 0)), tab, tab],
        out_specs=[pl.BlockSpec((1, tm, W), lambda b, i: (b, i, 0)),
                   pl.BlockSpec((1, MLA_HEADS * V_ROWS, tm), lambda b, i: (b, 0, i))],
        out_shape=[jax.ShapeDtypeStruct((B, T, W), BF16), jax.ShapeDtypeStruct((B, MLA_HEADS * V_ROWS, T), BF16)],
        compiler_params=_cparams("parallel", "parallel"),
        name="kv_projection",
    )(ckv, nw, wk, wvt, ct, st)


def _attn_kernel(q_ref, k_ref, vt_ref, o_ref, m_scr, acc_scr, sa_scr, sb_scr, *, tkc, nkc):
    m_scr[...] = jnp.full(m_scr.shape, -jnp.inf, F32)
    acc_scr[...] = jnp.zeros(acc_scr.shape, F32)

    def chunk(c):
        return pl.ds(pl.multiple_of(c * tkc, tkc), tkc)

    def scores(c, dst):
        for hh in range(2):
            q = q_ref[0, :, hh * HEAD_SLAB:(hh + 1) * HEAD_SLAB]
            kc = k_ref[0, chunk(c), hh * HEAD_SLAB:(hh + 1) * HEAD_SLAB]
            dst[hh] = lax.dot_general(kc, q, NT, preferred_element_type=F32)

    def consume(src, c):
        for hh in range(2):
            s = src[hh]
            m_old = m_scr[hh]
            m_new = jnp.maximum(m_old, jnp.max(s, axis=0, keepdims=True))
            pr = jnp.exp2(s - m_new).astype(BF16)
            alpha = jnp.exp2(m_old - m_new)
            m_scr[hh] = m_new
            rows = slice(hh * V_ROWS, (hh + 1) * V_ROWS)
            acc_scr[rows, :] = alpha * acc_scr[rows, :] + _dot(vt_ref[0, rows, chunk(c)], pr)

    scores(0, sa_scr)
    unroll = 2 if nkc > 4 else 1
    ngroups = (nkc - 1) // (2 * unroll)

    def body(i, carry):
        c = 2 * unroll * i
        for _ in range(unroll):
            scores(c + 1, sb_scr)
            consume(sa_scr, c)
            scores(c + 2, sa_scr)
            consume(sb_scr, c + 1)
            c = c + 2
        return carry

    lax.fori_loop(0, ngroups, body, 0)
    done = 2 * unroll * ngroups
    for cc in range(done, nkc):
        src, dst = (sa_scr, sb_scr) if (cc - done) % 2 == 0 else (sb_scr, sa_scr)
        if cc + 1 < nkc:
            scores(cc + 1, dst)
        consume(src, cc)
    outs = []
    for hh in range(2):
        base = hh * V_ROWS
        outs.append(acc_scr[base:base + V_HEAD, :] / acc_scr[base + V_HEAD:base + V_HEAD + 1, :])
    o_ref[0] = jnp.concatenate(outs, axis=0).T.astype(o_ref.dtype)


def attention(q, k, vt):
    B, T, _ = q.shape
    Tk = k.shape[1]
    tq = _tile(T, 256)
    tkc = next(c for c in (640, 512, 256, 128) if Tk % c == 0)
    hp = MLA_HEADS // 2
    return pl.pallas_call(
        functools.partial(_attn_kernel, tkc=tkc, nkc=Tk // tkc),
        grid=(B, hp, T // tq),
        in_specs=[pl.BlockSpec((1, tq, 2 * HEAD_SLAB), lambda b, h, i: (b, i, h)),
                  pl.BlockSpec((1, Tk, 2 * HEAD_SLAB), lambda b, h, i: (b, 0, h)),
                  pl.BlockSpec((1, 2 * V_ROWS, Tk), lambda b, h, i: (b, h, 0))],
        out_specs=pl.BlockSpec((1, tq, 2 * V_HEAD), lambda b, h, i: (b, i, h)),
        out_shape=jax.ShapeDtypeStruct((B, T, MLA_WIDTH), BF16),
        scratch_shapes=[pltpu.VMEM((2, 1, tq), F32), pltpu.VMEM((2 * V_ROWS, tq), F32),
                        pltpu.VMEM((2, tkc, tq), F32), pltpu.VMEM((2, tkc, tq), F32)],
        compiler_params=_cparams("parallel", "parallel", "arbitrary"),
        name="mla_attention",
    )(q, k, vt)


def _outproj_kernel(rw_ref, ft_ref, att_ref, x_ref, g_ref, lw_ref, lb_ref, w_ref, o_ref, *, alpha):
    w = w_ref
    mix = (_dot(rw_ref[0].astype(BF16), w[0:RW_WIDTH, :])
           + _dot(ft_ref[0].astype(BF16), w[RW_WIDTH:RW_WIDTH + FT_WIDTH, :])
           + _dot(att_ref[0], w[RW_WIDTH + FT_WIDTH:, :]))
    z = alpha * x_ref[0] + g_ref[0] * mix
    o_ref[0] = _standardize(z, LN_EPS) * lw_ref[...] + lb_ref[...]


def out_projection_ln(rw, ft, att, x, gate, ln_w, ln_b, w_out, alpha):
    B, T, D = x.shape
    tm = _tile(T, 256)
    tok = lambda n: pl.BlockSpec((1, tm, n), lambda b, i: (b, i, 0))
    row = pl.BlockSpec((1, D), lambda b, i: (0, 0))
    return pl.pallas_call(
        functools.partial(_outproj_kernel, alpha=alpha),
        grid=(B, T // tm),
        in_specs=[tok(RW_WIDTH), tok(FT_WIDTH), tok(MLA_WIDTH), tok(D),
                  pl.BlockSpec((1, 1, D), lambda b, i: (b, 0, 0)), row, row,
                  pl.BlockSpec(w_out.shape, lambda b, i: (0, 0))],
        out_specs=tok(D),
        out_shape=jax.ShapeDtypeStruct((B, T, D), F32),
        compiler_params=_cparams("parallel", "parallel"),
        name="out_projection_ln",
    )(rw, ft, att, x, gate, ln_w, ln_b, w_out)


def _ffn_kernel(x_ref, sh_ref, sc_ref, g_ref, lw_ref, lb_ref, w1_ref, w3_ref, w2_ref, o_ref, h_scr, acc_scr, *, alpha):
    f = pl.program_id(2)

    @pl.when(f == 0)
    def _():
        h = _standardize(x_ref[0], MOD_EPS) * (1.0 + sc_ref[0]) + sh_ref[0]
        h_scr[...] = h.astype(BF16)
        acc_scr[...] = jnp.zeros(acc_scr.shape, F32)

    hb = h_scr[...]
    a = _dot(hb, w1_ref[...])
    b = _dot(hb, w3_ref[...])
    acc_scr[...] += _dot((a * jax.nn.sigmoid(a) * b).astype(BF16), w2_ref[...])

    @pl.when(f == pl.num_programs(2) - 1)
    def _():
        z = alpha * x_ref[0] + g_ref[0] * acc_scr[...]
        o_ref[0] = _standardize(z, LN_EPS) * lw_ref[...] + lb_ref[...]


def ffn_ln(x, shift, scale, gate, ln_w, ln_b, w1, w3, w2, alpha):
    B, T, D = x.shape
    F = w1.shape[1]
    tm = _tile(T, 512)
    tf = _tile(F, 512)
    tok = pl.BlockSpec((1, tm, D), lambda b, i, f: (b, i, 0))
    vec = pl.BlockSpec((1, 1, D), lambda b, i, f: (b, 0, 0))
    row = pl.BlockSpec((1, D), lambda b, i, f: (0, 0))
    return pl.pallas_call(
        functools.partial(_ffn_kernel, alpha=alpha),
        grid=(B, T // tm, F // tf),
        in_specs=[tok, vec, vec, vec, row, row,
                  pl.BlockSpec((D, tf), lambda b, i, f: (0, f)),
                  pl.BlockSpec((D, tf), lambda b, i, f: (0, f)),
                  pl.BlockSpec((tf, D), lambda b, i, f: (f, 0))],
        out_specs=tok,
        out_shape=jax.ShapeDtypeStruct((B, T, D), F32),
        scratch_shapes=[pltpu.VMEM((tm, D), BF16), pltpu.VMEM((tm, D), F32)],
        compiler_params=_cparams("parallel", "parallel", "arbitrary"),
        name="ffn_ln",
    )(x, shift, scale, gate, ln_w, ln_b, w1, w3, w2)


def _router_kernel(x_ref, sh_ref, sc_ref, wr_ref, h_o, ti_o, tg_o):
    h = _standardize(x_ref[0], MOD_EPS) * (1.0 + sc_ref[0]) + sh_ref[0]
    h_o[0] = h.astype(BF16)
    logits = _dot(h, wr_ref[...], HI)
    lane = lax.broadcasted_iota(jnp.int32, logits.shape, 1)
    neg = jnp.float32(-jnp.inf)
    logits = jnp.where(lane < N_EXPERTS, logits, neg)
    m1 = jnp.max(logits, axis=-1, keepdims=True)
    i1 = jnp.min(jnp.where(logits == m1, lane, LANES), axis=-1, keepdims=True)
    rest = jnp.where(lane == i1, neg, logits)
    m2 = jnp.max(rest, axis=-1, keepdims=True)
    i2 = jnp.min(jnp.where(rest == m2, lane, LANES), axis=-1, keepdims=True)
    e = jnp.exp(m2 - m1)
    g1 = 1.0 / (1.0 + e)
    g2 = e / (1.0 + e)
    ti_o[0] = jnp.where(lane == 0, i1, jnp.where(lane == 1, i2, 0))
    tg_o[0] = jnp.where(lane == 0, g1, jnp.where(lane == 1, g2, 0.0))


def moe_router(x, shift, scale, wr):
    B, T, D = x.shape
    tm = _tile(T, 512)
    tok = lambda n: pl.BlockSpec((1, tm, n), lambda b, i: (b, i, 0))
    vec = pl.BlockSpec((1, 1, D), lambda b, i: (b, 0, 0))
    return pl.pallas_call(
        _router_kernel,
        grid=(B, T // tm),
        in_specs=[tok(D), vec, vec, pl.BlockSpec(wr.shape, lambda b, i: (0, 0))],
        out_specs=[tok(D), tok(LANES), tok(LANES)],
        out_shape=[jax.ShapeDtypeStruct((B, T, D), BF16), jax.ShapeDtypeStruct((B, T, LANES), jnp.int32),
                   jax.ShapeDtypeStruct((B, T, LANES), F32)],
        compiler_params=_cparams("parallel", "parallel"),
        name="moe_router",
    )(x, shift, scale, wr)


def _expert_kernel(be_ref, bv_ref, x_ref, sg_ref, w1_ref, w3_ref, w2_ref, o_ref, acc_scr):
    blk = pl.program_id(0)
    f = pl.program_id(1)
    valid = bv_ref[blk] > 0

    @pl.when(f == 0)
    def _():
        acc_scr[...] = jnp.zeros(acc_scr.shape, F32)

    @pl.when(valid)
    def _():
        xb = x_ref[...]
        a = _dot(xb, w1_ref[0])
        b = _dot(xb, w3_ref[0])
        acc_scr[...] += _dot((a * jax.nn.sigmoid(a) * b).astype(BF16), w2_ref[0])

    @pl.when(f == pl.num_programs(1) - 1)
    def _():
        o_ref[...] = acc_scr[...] * sg_ref[...]


def expert_ffn(xg, slot_gate, blk_expert, blk_valid, w1, w3, w2):
    P, D = xg.shape
    F = w1.shape[2]
    tf = _tile(F, 512)
    nblk = P // MOE_ROWS
    grid_spec = pltpu.PrefetchScalarGridSpec(
        num_scalar_prefetch=2,
        grid=(nblk, F // tf),
        in_specs=[pl.BlockSpec((MOE_ROWS, D), lambda i, f, be, bv: (i, 0)),
                  pl.BlockSpec((MOE_ROWS, 1), lambda i, f, be, bv: (i, 0)),
                  pl.BlockSpec((1, D, tf), lambda i, f, be, bv: (be[i], 0, jnp.where(bv[i] > 0, f, 0))),
                  pl.BlockSpec((1, D, tf), lambda i, f, be, bv: (be[i], 0, jnp.where(bv[i] > 0, f, 0))),
                  pl.BlockSpec((1, tf, D), lambda i, f, be, bv: (be[i], jnp.where(bv[i] > 0, f, 0), 0))],
        out_specs=pl.BlockSpec((MOE_ROWS, D), lambda i, f, be, bv: (i, 0)),
        scratch_shapes=[pltpu.VMEM((MOE_ROWS, D), F32)],
    )
    return pl.pallas_call(
        _expert_kernel,
        grid_spec=grid_spec,
        out_shape=jax.ShapeDtypeStruct((P, D), F32),
        compiler_params=_cparams("parallel", "arbitrary"),
        name="expert_ffn",
    )(blk_expert, blk_valid, xg, slot_gate, w1, w3, w2)


def _residual_ln_kernel(x_ref, y_ref, g_ref, lw_ref, lb_ref, o_ref, *, alpha):
    z = alpha * x_ref[0] + g_ref[0] * y_ref[0]
    o_ref[0] = _standardize(z, LN_EPS) * lw_ref[...] + lb_ref[...]


def residual_ln(x, y, gate, ln_w, ln_b, alpha):
    B, T, D = x.shape
    tm = _tile(T, 512)
    tok = pl.BlockSpec((1, tm, D), lambda b, i: (b, i, 0))
    row = pl.BlockSpec((1, D), lambda b, i: (0, 0))
    return pl.pallas_call(
        functools.partial(_residual_ln_kernel, alpha=alpha),
        grid=(B, T // tm),
        in_specs=[tok, tok, pl.BlockSpec((1, 1, D), lambda b, i: (b, 0, 0)), row, row],
        out_specs=tok,
        out_shape=jax.ShapeDtypeStruct((B, T, D), F32),
        compiler_params=_cparams("parallel", "parallel"),
        name="residual_ln",
    )(x, y, gate, ln_w, ln_b)


def moe_ln(x, shift, scale, gate, ln_w, ln_b, router, w1, w3, w2, alpha):
    B, T, D = x.shape
    N = B * T
    E = router.shape[1]
    wr = jnp.zeros((D, LANES), F32).at[:, :E].set(router)
    h, ti, tg = moe_router(x, shift, scale, wr)
    top_i = ti.reshape(N, LANES)[:, :2]
    top_g = tg.reshape(N, LANES)[:, :2]
    e_flat = top_i.reshape(-1)
    onehot = (e_flat[:, None] == jnp.arange(E, dtype=jnp.int32)[None, :]).astype(jnp.int32)
    rank = jnp.sum((jnp.cumsum(onehot, axis=0) - onehot) * onehot, axis=1)
    counts = jnp.sum(onehot, axis=0)
    padded = (counts + MOE_ROWS - 1) // MOE_ROWS * MOE_ROWS
    p_ends = jnp.cumsum(padded)
    p_starts = p_ends - padded
    dest = p_starts[e_flat] + rank
    P = -(-(2 * N) // MOE_ROWS) * MOE_ROWS + E * MOE_ROWS
    nblk = P // MOE_ROWS
    tok_flat = jnp.repeat(jnp.arange(N, dtype=jnp.int32), 2)
    slot_tok = jnp.zeros((P,), jnp.int32).at[dest].set(tok_flat)
    slot_gate = jnp.zeros((P,), F32).at[dest].set(top_g.reshape(-1))
    blk_start = jnp.arange(nblk, dtype=jnp.int32) * MOE_ROWS
    blk_expert = jnp.minimum(jnp.searchsorted(p_ends, blk_start, side="right"), E - 1).astype(jnp.int32)
    blk_valid = (blk_start < p_ends[-1]).astype(jnp.int32)
    xg = h.reshape(N, D)[slot_tok]
    yg = expert_ffn(xg, slot_gate[:, None], blk_expert, blk_valid, w1, w3, w2)
    d2 = dest.reshape(N, 2)
    y = (yg[d2[:, 0]] + yg[d2[:, 1]]).reshape(B, T, D)
    return residual_ln(x, y, gate, ln_w, ln_b, alpha)


def _rope_partner(w):
    half = ROPE_AXIS // 2
    idx = np.arange(QK_ROPE)
    first = (idx % ROPE_AXIS) < half
    src = np.where(first, idx + half, idx - half)
    sign = np.where(first, -1.0, 1.0).astype(np.float32)
    return w[:, src] * sign


def _layer_params(l, w_in, rw_conv, rw_w0, rw_w_up, rw_a0, rw_a_up, rw_g_up, rw_k_k, rw_k_a, rw_r_k, rw_gn_w,
                  rw_gn_b, mla_q_norm, mla_w_uq, mla_kv_norm, mla_w_ukv, w_out):
    D = w_in.shape[1]
    wi = w_in[l]
    o = np.cumsum([0, 3 * RW_WIDTH, G_RANK, LORA_RANK, LORA_RANK, LORA_RANK, LORA_RANK, FT_WIDTH, Q_RANK, KV_RANK, QK_ROPE])
    piece = lambda i: wi[:, o[i]:o[i + 1]]
    zpad = lambda n: jnp.zeros((D, n), F32)
    lora = [jnp.concatenate([piece(i), zpad(LANES - LORA_RANK)], axis=1) for i in (2, 3, 4, 5)]
    w_rw = jnp.concatenate([piece(0), piece(1)] + lora, axis=1)
    kr = piece(9)
    w_ckv = jnp.concatenate([piece(8), zpad(QK_NOPE), kr, _rope_partner(kr)], axis=1)
    pad_rows = lambda a: jnp.concatenate([a, jnp.zeros((a.shape[0], LANES - LORA_RANK, a.shape[2]), F32)], axis=1)
    head = jnp.arange(RW_WIDTH) // RW_HEAD
    uq = mla_w_uq[l].reshape(Q_RANK, MLA_HEADS, QK_NOPE + QK_ROPE)
    uq_rope = uq[:, :, QK_NOPE:]
    uq_partner = _rope_partner(uq_rope.reshape(Q_RANK * MLA_HEADS, QK_ROPE)).reshape(Q_RANK, MLA_HEADS, QK_ROPE)
    w_q = jnp.concatenate([uq, uq_partner], axis=2).reshape(Q_RANK, MLA_HEADS * HEAD_SLAB)
    ukv = mla_w_ukv[l].reshape(KV_RANK, MLA_HEADS, QK_NOPE + V_HEAD)
    w_k = jnp.concatenate([ukv[:, :, :QK_NOPE], jnp.zeros((KV_RANK, MLA_HEADS, HEAD_SLAB - QK_NOPE), F32)],
                          axis=2).reshape(KV_RANK, MLA_HEADS * HEAD_SLAB)
    w_v = jnp.concatenate([ukv[:, :, QK_NOPE:], jnp.zeros((KV_RANK, MLA_HEADS, ONES_ROWS), F32)],
                          axis=2).reshape(KV_RANK, MLA_HEADS * V_ROWS)
    return {
        "w_in": [w.astype(BF16) for w in (w_rw, piece(6), piece(7), w_ckv)],
        "rw": {"conv": rw_conv[l], "k_k": rw_k_k[l][None], "k_a": rw_k_a[l][None], "r_k": rw_r_k[l].reshape(1, RW_WIDTH),
               "w0": rw_w0[l], "a0": rw_a0[l], "w_up": pad_rows(rw_w_up[l]).astype(BF16), "a_up": pad_rows(rw_a_up[l]).astype(BF16),
               "g_up": rw_g_up[l].astype(BF16), "gn_w": rw_gn_w[l][None], "gn_b": rw_gn_b[l][None],
               "E": (head[:, None] == head[None, :]).astype(BF16)},
        "q_norm": mla_q_norm[l][None], "w_q": w_q.astype(BF16),
        "kv_norm": mla_kv_norm[l][None], "w_k": w_k.astype(BF16), "w_vt": w_v.T.astype(BF16),
        "w_out": w_out[l].astype(BF16),
    }


def _rope_tables(T, use_rope):
    ones = jnp.ones((T, QK_NOPE), F32)
    zeros = jnp.zeros((T, QK_NOPE), F32)
    zpad = jnp.zeros((T, HEAD_SLAB - QK_NOPE - QK_ROPE), F32)
    if use_rope:
        row = jnp.repeat(jnp.arange(T // GRID_W), GRID_W).astype(F32)
        col = (jnp.arange(T) % GRID_W).astype(F32)
        inv = ROPE_THETA ** (-jnp.arange(0, ROPE_AXIS, 2, dtype=F32) / ROPE_AXIS)
        ang = jnp.stack([row[:, None] * inv, col[:, None] * inv], axis=1)
        ang = jnp.broadcast_to(ang[:, :, None, :], (T, 2, 2, ROPE_AXIS // 2)).reshape(T, QK_ROPE)
        cos, sin = jnp.cos(ang), jnp.sin(ang)
    else:
        cos, sin = jnp.ones((T, QK_ROPE), F32), jnp.zeros((T, QK_ROPE), F32)
    return jnp.concatenate([ones, cos, zpad], axis=1), jnp.concatenate([zeros, sin, zpad], axis=1)


def _mixer(h_pieces_lat, h_pieces_ctx, p, tabs_lat, tabs_ctx, emit_ctx):
    rw_l, ft_l, cq_l, ckv_l = h_pieces_lat
    rw_c, ft_c, cq_c, ckv_c = h_pieces_ctx
    rwo_l, rwo_c = rwkv_branch(rw_l, rw_c, p["rw"], emit_ctx)
    fto_l = fourier_mixer(ft_l)
    q_l = q_projection(cq_l, p["q_norm"], p["w_q"], *tabs_lat)
    k_l, vt_l = kv_projection(ckv_l, p["kv_norm"], p["w_k"], p["w_vt"], *tabs_lat)
    k_c, vt_c = kv_projection(ckv_c, p["kv_norm"], p["w_k"], p["w_vt"], *tabs_ctx)
    att_l = attention(q_l, jnp.concatenate([k_l, k_c], axis=1), jnp.concatenate([vt_l, vt_c], axis=2))
    out_c = None
    if emit_ctx:
        fto_c = fourier_mixer(ft_c)
        q_c = q_projection(cq_c, p["q_norm"], p["w_q"], *tabs_ctx)
        att_c = attention(q_c, k_c, vt_c)
        out_c = (rwo_c, fto_c, att_c)
    return (rwo_l, fto_l, att_l), out_c


def kernel(x, c, ctx, c_ctx, ada_w, ada_b, w_in, rw_conv, rw_w0, rw_w_up, rw_a0, rw_a_up, rw_g_up, rw_k_k, rw_k_a,
           rw_r_k, rw_gn_w, rw_gn_b, mla_q_norm, mla_w_uq, mla_kv_norm, mla_w_ukv, w_out, ln1_w, ln1_b, ln2_w, ln2_b,
           ffn_w1, ffn_w3, ffn_w2, moe_router, moe_w1, moe_w3, moe_w2):
    B, T, D = x.shape
    Tc = ctx.shape[1]
    depth = w_in.shape[0]
    alpha = (2 * depth) ** 0.25
    assert B + 1 <= SUBLANES
    cc = jnp.zeros((SUBLANES, D), F32).at[:B].set(c).at[B].set(c_ctx)
    ada = ada_vectors(cc, ada_w, ada_b)
    tabs_lat = _rope_tables(T, True)
    tabs_ctx = _rope_tables(Tc, False)
    for l in range(depth):
        last = l == depth - 1
        p = _layer_params(l, w_in, rw_conv, rw_w0, rw_w_up, rw_a0, rw_a_up, rw_g_up, rw_k_k, rw_k_a, rw_r_k,
                          rw_gn_w, rw_gn_b, mla_q_norm, mla_w_uq, mla_kv_norm, mla_w_ukv, w_out)
        mods = ada[l].reshape(SUBLANES, 6, D)
        lat = [mods[:B, j][:, None, :] for j in range(6)]
        cx = [jnp.broadcast_to(mods[B, j][None, None, :], (B, 1, D)) for j in range(6)]
        sh_m, sc_m, g_m, sh_f, sc_f, g_f = lat
        csh_m, csc_m, cg_m, csh_f, csc_f, cg_f = cx
        ln1 = (ln1_w[l][None], ln1_b[l][None])
        ln2 = (ln2_w[l][None], ln2_b[l][None])

        pieces_l = in_projection(x, sh_m, sc_m, p["w_in"])
        pieces_c = in_projection(ctx, csh_m, csc_m, p["w_in"])
        mix_l, mix_c = _mixer(pieces_l, pieces_c, p, tabs_lat, tabs_ctx, not last)
        x = out_projection_ln(*mix_l, x, g_m, *ln1, p["w_out"], alpha)
        i = l // 2
        if l % 2 == 0:
            dense = (ffn_w1[i].astype(BF16), ffn_w3[i].astype(BF16), ffn_w2[i].astype(BF16))
            x = ffn_ln(x, sh_f, sc_f, g_f, *ln2, *dense, alpha)
        else:
            x = moe_ln(x, sh_f, sc_f, g_f, *ln2, moe_router[i], moe_w1[i].astype(BF16), moe_w3[i].astype(BF16),
                       moe_w2[i].astype(BF16), alpha)
        if not last:
            ctx = out_projection_ln(*mix_c, ctx, cg_m, *ln1, p["w_out"], alpha)
            if l % 2 == 0:
                ctx = ffn_ln(ctx, csh_f, csc_f, cg_f, *ln2, *dense, alpha)
            else:
                ctx = moe_ln(ctx, csh_f, csc_f, cg_f, *ln2, moe_router[i], moe_w1[i].astype(BF16),
                             moe_w3[i].astype(BF16), moe_w2[i].astype(BF16), alpha)
    return x
```

```python
import functools
import math

import numpy as np
import jax
import jax.numpy as jnp
from jax import lax
from jax.experimental import pallas as pl
from jax.experimental.pallas import tpu as pltpu

F32 = jnp.float32
BF16 = jnp.bfloat16
HI = lax.Precision.HIGHEST

LANES = 128
SUBLANES = 8
VMEM_LIMIT = 56 * 1024 * 1024

GRID_W = 64
RW_HEADS = 8
RW_HEAD = 64
RW_WIDTH = RW_HEADS * RW_HEAD
G_RANK = 128
LORA_RANK = 64
GN_EPS = 64e-5
FT_GROUP = 64
FT_WIDTH = 512
MLA_HEADS = 16
QK_NOPE = 64
QK_ROPE = 32
V_HEAD = 64
MLA_WIDTH = MLA_HEADS * V_HEAD
Q_RANK = 512
KV_RANK = 256
ROPE_AXIS = QK_ROPE // 2
ROPE_THETA = 10000.0
ATTN_SCALE = (QK_NOPE + QK_ROPE) ** -0.5
Q_SCALE = ATTN_SCALE * math.log2(math.e)
N_EXPERTS = 8
LN_EPS = 1e-5
MOD_EPS = 1e-6
RMS_EPS = 1e-6
CHUNK = 64
INV_BLOCK = 16
MOE_ROWS = 1024

NT = (((1,), (1,)), ((), ()))
TN = (((0,), (0,)), ((), ()))


def _cparams(*sem):
    return pltpu.CompilerParams(dimension_semantics=sem, vmem_limit_bytes=VMEM_LIMIT)


def _dot(a, b, prec=None):
    return jnp.dot(a, b, precision=prec, preferred_element_type=F32)


def _dot_split(x, w):
    hi = x.astype(BF16)
    r1 = x - hi.astype(F32)
    mid = r1.astype(BF16)
    lo = (r1 - mid.astype(F32)).astype(BF16)
    return _dot(hi, w) + _dot(mid, w) + _dot(lo, w)


def _dot3(a, b):
    a_hi = a.astype(BF16)
    b_hi = b.astype(BF16)
    a_lo = (a - a_hi.astype(F32)).astype(BF16)
    b_lo = (b - b_hi.astype(F32)).astype(BF16)
    return _dot(a_hi, b_hi) + _dot(a_hi, b_lo) + _dot(a_lo, b_hi)


def _standardize(x, eps):
    mu = jnp.mean(x, axis=-1, keepdims=True)
    xc = x - mu
    var = jnp.mean(xc * xc, axis=-1, keepdims=True)
    return xc * lax.rsqrt(var + eps)


def _tile(n, pref):
    t = min(n, pref)
    assert n % t == 0, (n, pref)
    return t


def _ada_kernel(c_ref, w_ref, b_ref, o_ref):
    c = c_ref[...]
    s = c * jax.nn.sigmoid(c)
    o_ref[0] = _dot(s, w_ref[0], HI) + b_ref[0]


def ada_vectors(cc, ada_w, ada_b):
    L, D, N6 = ada_w.shape
    tn = _tile(N6, 1024)
    return pl.pallas_call(
        _ada_kernel,
        grid=(L, N6 // tn),
        in_specs=[pl.BlockSpec((SUBLANES, D), lambda l, j: (0, 0)),
                  pl.BlockSpec((1, D, tn), lambda l, j: (l, 0, j)),
                  pl.BlockSpec((1, 1, tn), lambda l, j: (l, 0, j))],
        out_specs=pl.BlockSpec((1, SUBLANES, tn), lambda l, j: (l, 0, j)),
        out_shape=jax.ShapeDtypeStruct((L, SUBLANES, N6), F32),
        compiler_params=_cparams("parallel", "parallel"),
        name="ada_vectors",
    )(cc, ada_w, ada_b.reshape(L, 1, N6))


def _inproj_kernel(x_ref, sh_ref, sc_ref, *refs):
    nw = len(refs) // 2
    h = _standardize(x_ref[0], MOD_EPS) * (1.0 + sc_ref[0]) + sh_ref[0]
    hb = h.astype(BF16)
    for w_ref, o_ref in zip(refs[:nw], refs[nw:]):
        o_ref[0] = _dot(hb, w_ref[...])


def in_projection(x, shift, scale, weights):
    B, T, D = x.shape
    tm = _tile(T, 256)
    vec = pl.BlockSpec((1, 1, D), lambda b, i: (b, 0, 0))
    return pl.pallas_call(
        _inproj_kernel,
        grid=(B, T // tm),
        in_specs=[pl.BlockSpec((1, tm, D), lambda b, i: (b, i, 0)), vec, vec]
        + [pl.BlockSpec(w.shape, lambda b, i: (0, 0)) for w in weights],
        out_specs=[pl.BlockSpec((1, tm, w.shape[1]), lambda b, i: (b, i, 0)) for w in weights],
        out_shape=[jax.ShapeDtypeStruct((B, T, w.shape[1]), F32) for w in weights],
        compiler_params=_cparams("parallel", "parallel"),
        name="in_projection",
    )(x, shift, scale, *weights)


RW_COLS = 3 * RW_WIDTH + G_RANK + 4 * LANES


def _softplus(z):
    return jnp.maximum(z, 0.0) + jnp.log(1.0 + jnp.exp(-jnp.abs(z)))


def _rwkv_prep_kernel(x_ref, xp_ref, xn_ref, conv_ref, kk_ref, ka_ref, rk_ref, w0_ref, a0_ref,
                      wup_ref, aup_ref, gup_ref, e_ref,
                      r_o, v_o, kk_o, ld0_o, b0_o, kr0_o, ld1_o, b1_o, kr1_o, bonus_o, gate_o):
    i = pl.program_id(1)
    n = pl.num_programs(1)
    x = x_ref[0]
    W3 = 3 * RW_WIDTH
    raw = x[:, :W3]
    tm = raw.shape[0]
    row = lax.broadcasted_iota(jnp.int32, (tm, 1), 0)
    prev_row = jnp.where(i > 0, xp_ref[0, SUBLANES - 1:SUBLANES, :], 0.0)
    next_row = jnp.where(i < n - 1, xn_ref[0, 0:1, :], 0.0)
    xm = jnp.where(row == 0, prev_row, pltpu.roll(raw, 1, 0))
    xq = jnp.where(row == tm - 1, next_row, pltpu.roll(raw, tm - 1, 0))
    cw = conv_ref[...]
    y = xm * cw[0:1] + raw * cw[1:2] + xq * cw[2:3]
    r = y[:, :RW_WIDTH]
    k = y[:, RW_WIDTH:2 * RW_WIDTH]
    v = y[:, 2 * RW_WIDTH:W3]
    E = e_ref[...]
    kkv = k * kk_ref[...]
    kk = kkv / jnp.maximum(jnp.sqrt(_dot_split(kkv * kkv, E)), 1e-12)
    r_o[0] = r
    v_o[0] = v
    kk_o[0] = kk
    g_dn = x[:, W3:W3 + G_RANK]
    gate_o[0] = _dot(jax.nn.sigmoid(g_dn).astype(BF16), gup_ref[...])
    bonus = jnp.zeros_like(r)
    outs = ((ld0_o, b0_o, kr0_o), (ld1_o, b1_o, kr1_o))
    for d in range(2):
        base = W3 + G_RANK
        w_dn = x[:, base + d * LANES: base + (d + 1) * LANES]
        a_dn = x[:, base + (2 + d) * LANES: base + (3 + d) * LANES]
        z = w0_ref[d:d + 1, :] + _dot(jnp.tanh(w_dn).astype(BF16), wup_ref[d])
        logw = -_softplus(-z) - 0.5
        a = jax.nn.sigmoid(a0_ref[d:d + 1, :] + _dot(a_dn.astype(BF16), aup_ref[d]))
        kr = k * (1.0 + (a - 1.0) * ka_ref[...])
        ld_o, b_o, kr_o = outs[d]
        ld_o[0] = -jnp.exp(logw)
        b_o[0] = a * kk
        kr_o[0] = kr
        bonus = bonus + _dot_split(r * kr * rk_ref[...], E) * v
    bonus_o[0] = bonus


def rwkv_prep(rw, p):
    B, T, _ = rw.shape
    tm = _tile(T, 256)
    nh = tm // SUBLANES
    last = T // SUBLANES - 1
    W3 = 3 * RW_WIDTH
    full = lambda a: pl.BlockSpec(a.shape, lambda b, i: (0,) * a.ndim)
    params = [p["conv"], p["k_k"], p["k_a"], p["r_k"], p["w0"], p["a0"], p["w_up"], p["a_up"], p["g_up"], p["E"]]
    outs = pl.pallas_call(
        _rwkv_prep_kernel,
        grid=(B, T // tm),
        in_specs=[pl.BlockSpec((1, tm, RW_COLS), lambda b, i: (b, i, 0)),
                  pl.BlockSpec((1, SUBLANES, W3), lambda b, i: (b, jnp.maximum(i * nh - 1, 0), 0)),
                  pl.BlockSpec((1, SUBLANES, W3), lambda b, i: (b, jnp.minimum((i + 1) * nh, last), 0))]
        + [full(a) for a in params],
        out_specs=[pl.BlockSpec((1, tm, RW_WIDTH), lambda b, i: (b, i, 0))] * 11,
        out_shape=[jax.ShapeDtypeStruct((B, T, RW_WIDTH), F32)] * 11,
        compiler_params=_cparams("parallel", "parallel"),
        name="rwkv_prep",
    )(rw, rw, rw, *params)
    return outs


PAIR = 2 * RW_HEAD
N_PAIRS = RW_HEADS // 2
CHUNK_GROUP = 4


def _pair_masks(reverse):
    i = lax.broadcasted_iota(jnp.int32, (PAIR, PAIR), 0)
    j = lax.broadcasted_iota(jnp.int32, (PAIR, PAIR), 1)
    same = (i // CHUNK) == (j // CHUNK)
    strict = same & ((j > i) if reverse else (j < i))
    incl = same & ((j >= i) if reverse else (j <= i))
    blk = (i // INV_BLOCK) == (j // INV_BLOCK)
    eye = jnp.where(i == j, 1.0, 0.0).astype(F32)
    t = lax.broadcasted_iota(jnp.int32, (CHUNK, CHUNK), 0)
    u = lax.broadcasted_iota(jnp.int32, (CHUNK, CHUNK), 1)
    tri = jnp.where((u >= t) if reverse else (u <= t), 1.0, 0.0).astype(F32)
    first = lax.broadcasted_iota(jnp.int32, (1, PAIR), 1) < RW_HEAD
    return strict, incl, blk, eye, tri, first


def _pair_chunk_math(ld, r, v, kk, b, kr, masks):
    strict, incl, blk, eye, tri, first = masks
    P = ld.shape[0]
    ein = lambda spec, a, c, prec=None: jnp.einsum(spec, a, c, precision=prec, preferred_element_type=F32)
    Lc = ein("pct,ptk->pck", jnp.broadcast_to(tri, (P, CHUNK, CHUNK)), ld, HI)
    Lx = Lc - ld
    Lt = jnp.sum(ld, axis=1, keepdims=True)
    ginv = jnp.exp(-Lc)
    gout = jnp.exp(Lt - Lc)
    stack = lambda x: jnp.concatenate([jnp.where(first, x, 0.0), jnp.where(first, 0.0, x)], axis=1).astype(BF16)
    twice = lambda x: jnp.concatenate([x, x], axis=1).astype(BF16)
    Xk = stack(kk * jnp.exp(Lx))
    Xr = stack(r * jnp.exp(Lc))
    Vs = stack(v)
    Bs = stack(b * gout)
    Ks = stack(kr * gout)
    mm = lambda a, c: ein("pij,pjk->pik", a.astype(BF16), c.astype(BF16))
    nt = lambda a, c: ein("pik,pjk->pij", a, c)
    tn = lambda a, c: ein("pji,pjk->pik", a, c.astype(BF16))
    XX = jnp.concatenate([Xk, Xr], axis=1)
    Mb = nt(XX, twice(b * ginv))
    Mk = nt(XX, twice(kr * ginv))
    Mab = jnp.where(strict, Mb[:, :PAIR], 0.0)
    Arb = jnp.where(incl, Mb[:, PAIR:], 0.0)
    Mak = jnp.where(strict, Mk[:, :PAIR], 0.0)
    Ark = jnp.where(incl, Mk[:, PAIR:], 0.0)
    Nd = jnp.where(blk, Mab, 0.0)
    No = Mab - Nd
    N2 = mm(Nd, Nd)
    N4 = mm(N2, N2)
    N8 = mm(N4, N4)
    Td = mm(mm(mm(eye - Nd, eye + N2), eye + N4), eye + N8)
    M2 = mm(Td, No)
    Tm = mm(mm(eye - M2, eye + mm(M2, M2)), Td)
    P1 = mm(Tm, Xk)
    P2 = mm(Tm, mm(Mak, Vs))
    Q1s = Xr.astype(F32) - mm(Arb, P1)
    Yis = mm(Ark, Vs) - mm(Arb, P2)
    G = eye * jnp.exp(Lt) - tn(Bs, P1)
    H = tn(Ks, Vs) - tn(Bs, P2)
    return Q1s[:, :CHUNK] + Q1s[:, CHUNK:], Yis[:, :CHUNK] + Yis[:, CHUNK:], G, H


def _rwkv_chunk_kernel(ld_ref, r_ref, v_ref, kk_ref, b_ref, kr_ref, q_o, yi_o, g_o, h_o, *, nch, reverse):
    masks = _pair_masks(reverse)
    group = min(CHUNK_GROUP, nch)

    def body(i, carry):
        c0 = i * group
        rows = [pl.ds(pl.multiple_of((c0 + u) * CHUNK, CHUNK), CHUNK) for u in range(group)]
        get = lambda ref: jnp.stack([ref[0, rows[u], j * PAIR:(j + 1) * PAIR]
                                     for u in range(group) for j in range(N_PAIRS)])
        Q1, Yi, G, H = _pair_chunk_math(get(ld_ref), get(r_ref), get(v_ref), get(kk_ref), get(b_ref), get(kr_ref),
                                        masks)
        for u in range(group):
            for j in range(N_PAIRS):
                q_o[0, rows[u], j * PAIR:(j + 1) * PAIR] = Q1[u * N_PAIRS + j]
                yi_o[0, rows[u], j * PAIR:(j + 1) * PAIR] = Yi[u * N_PAIRS + j]
        mats = pl.ds(c0 * N_PAIRS, group * N_PAIRS)
        g_o[0, mats] = G
        h_o[0, mats] = H
        return carry

    lax.fori_loop(0, nch // group, body, 0)


def rwkv_chunks(ld, r, v, kk, b, kr, reverse):
    B, T, W = ld.shape
    tt = _tile(T, 4 * CHUNK)
    nch = tt // CHUNK
    tok = pl.BlockSpec((1, tt, W), lambda bb, i: (bb, i, 0))
    mat = pl.BlockSpec((1, nch * N_PAIRS, PAIR, PAIR), lambda bb, i: (bb, i, 0, 0))
    mats = jax.ShapeDtypeStruct((B, T // CHUNK * N_PAIRS, PAIR, PAIR), F32)
    return pl.pallas_call(
        functools.partial(_rwkv_chunk_kernel, nch=nch, reverse=reverse),
        grid=(B, T // tt),
        in_specs=[tok] * 6,
        out_specs=[tok, tok, mat, mat],
        out_shape=[jax.ShapeDtypeStruct((B, T, W), F32)] * 2 + [mats, mats],
        compiler_params=_cparams("parallel", "parallel"),
        name="rwkv_chunks_bwd" if reverse else "rwkv_chunks_fwd",
    )(ld, r, v, kk, b, kr)


def _rwkv_seq_kernel(q_ref, yi_ref, g_ref, h_ref, s0_ref, y_o, sf_o, s_scr, *, nch, reverse):
    @pl.when(pl.program_id(1) == 0)
    def _():
        s_scr[...] = s0_ref[0]

    def body(cc, carry):
        c = (nch - 1 - cc) if reverse else cc
        rows = pl.ds(pl.multiple_of(c * CHUNK, CHUNK), CHUNK)
        for j in range(N_PAIRS):
            lanes = slice(j * PAIR, (j + 1) * PAIR)
            S = s_scr[j]
            y_o[0, rows, lanes] = _dot3(q_ref[0, rows, lanes], S) + yi_ref[0, rows, lanes]
            s_scr[j] = _dot3(g_ref[0, c * N_PAIRS + j], S) + h_ref[0, c * N_PAIRS + j]
        return carry

    lax.fori_loop(0, nch, body, 0)
    sf_o[0] = s_scr[...]


def rwkv_sequential(q1, yi, g, hm, s0, reverse):
    B, T, W = q1.shape
    tt = _tile(T, 8 * CHUNK)
    n = T // tt
    nch = tt // CHUNK
    step = (lambda i: n - 1 - i) if reverse else (lambda i: i)
    tok = pl.BlockSpec((1, tt, W), lambda bb, i: (bb, step(i), 0))
    mat = pl.BlockSpec((1, nch * N_PAIRS, PAIR, PAIR), lambda bb, i: (bb, step(i), 0, 0))
    sspec = pl.BlockSpec((1, N_PAIRS, PAIR, PAIR), lambda bb, i: (bb, 0, 0, 0))
    return pl.pallas_call(
        functools.partial(_rwkv_seq_kernel, nch=nch, reverse=reverse),
        grid=(B, n),
        in_specs=[tok, tok, mat, mat, sspec],
        out_specs=[tok, sspec],
        out_shape=[jax.ShapeDtypeStruct((B, T, W), F32), jax.ShapeDtypeStruct((B, N_PAIRS, PAIR, PAIR), F32)],
        scratch_shapes=[pltpu.VMEM((N_PAIRS, PAIR, PAIR), F32)],
        compiler_params=_cparams("parallel", "arbitrary"),
        name="rwkv_seq_bwd" if reverse else "rwkv_seq_fwd",
    )(q1, yi, g, hm, s0)


def _rwkv_readout_kernel(yf_ref, yb_ref, bonus_ref, gate_ref, gw_ref, gb_ref, e_ref, o_ref):
    E = e_ref[...]
    ys = yf_ref[0] + yb_ref[0]
    yc = ys - _dot_split(ys, E) * (1.0 / RW_HEAD)
    var = _dot_split(yc * yc, E) * (1.0 / RW_HEAD)
    yn = yc * lax.rsqrt(var + GN_EPS) * gw_ref[...] + gb_ref[...]
    o_ref[0] = (yn + bonus_ref[0]) * gate_ref[0]


def rwkv_readout(yf, yb, bonus, gate, p):
    B, T, W = yf.shape
    tm = _tile(T, 512)
    spec = pl.BlockSpec((1, tm, W), lambda b, i: (b, i, 0))
    full = lambda a: pl.BlockSpec(a.shape, lambda b, i: (0,) * a.ndim)
    return pl.pallas_call(
        _rwkv_readout_kernel,
        grid=(B, T // tm),
        in_specs=[spec] * 4 + [full(p["gn_w"]), full(p["gn_b"]), full(p["E"])],
        out_specs=spec,
        out_shape=jax.ShapeDtypeStruct((B, T, W), F32),
        compiler_params=_cparams("parallel", "parallel"),
        name="rwkv_readout",
    )(yf, yb, bonus, gate, p["gn_w"], p["gn_b"], p["E"])


def rwkv_branch(rw_lat, rw_ctx, p, emit_ctx):
    prep_l = rwkv_prep(rw_lat, p)
    prep_c = rwkv_prep(rw_ctx, p)
    B = rw_lat.shape[0]
    s_zero = jnp.zeros((B, N_PAIRS, PAIR, PAIR), F32)

    def scans(prep, s0s):
        r, v, kk = prep[:3]
        ys, finals = [], []
        for d in range(2):
            ld, b, kr = prep[3 + 3 * d: 6 + 3 * d]
            q1, yi, g, hm = rwkv_chunks(ld, r, v, kk, b, kr, reverse=bool(d))
            y, sf = rwkv_sequential(q1, yi, g, hm, s0s[d], reverse=bool(d))
            ys.append(y)
            finals.append(sf)
        return ys, finals

    ys_c, fin_c = scans(prep_c, (s_zero, s_zero))
    ys_l, _ = scans(prep_l, fin_c)
    out_l = rwkv_readout(ys_l[0], ys_l[1], prep_l[9], prep_l[10], p)
    out_c = rwkv_readout(ys_c[0], ys_c[1], prep_c[9], prep_c[10], p) if emit_ctx else None
    return out_l, out_c


def _dft_mats(n):
    a = 2.0 * np.pi * np.outer(np.arange(n), np.arange(n)) / n
    return np.cos(a), np.sin(a)


def _fft1_kernel(u_ref, c_ref, s_ref, twc_ref, tws_ref, ar_o, ai_o, *, tn2, ch):
    U = u_ref[0]
    Ar = _dot3(c_ref[...], U)
    Ai = -_dot3(s_ref[...], U)
    twc = twc_ref[0]
    tws = tws_ref[0]
    for j in range(tn2):
        ct = twc[:, j:j + 1]
        st = tws[:, j:j + 1]
        a_r = Ar[:, j * ch:(j + 1) * ch]
        a_i = Ai[:, j * ch:(j + 1) * ch]
        ar_o[0, j] = a_r * ct + a_i * st
        ai_o[0, j] = a_i * ct - a_r * st


def _fft2_kernel(ar_ref, ai_ref, c_ref, s_ref, cc_ref, sc_ref, o_ref):
    Ar = ar_ref[0]
    Ai = ai_ref[0]
    C = c_ref[...]
    S = s_ref[...]
    Yr = _dot3(C, Ar) + _dot3(S, Ai)
    Yi = _dot3(C, Ai) - _dot3(S, Ar)
    Cc = cc_ref[...]
    Sc = sc_ref[...]
    for m in range(Ar.shape[1] // LANES):
        sl = slice(m * LANES, (m + 1) * LANES)
        o_ref[0, :, sl] = _dot3(Yr[:, sl], Cc) + _dot3(Yi[:, sl], Sc)


def fourier_mixer(u):
    B, T, ch = u.shape
    lg = int(round(math.log2(T)))
    assert 1 << lg == T
    N1 = 1 << ((lg + 1) // 2)
    N2 = T // N1
    c1, s1 = _dft_mats(N1)
    c2, s2 = _dft_mats(N2)
    tw = 2.0 * np.pi * np.outer(np.arange(N1), np.arange(N2)) / T
    tn2 = min(SUBLANES, N2)
    nj = N2 // tn2
    twc = np.cos(tw).reshape(N1, nj, tn2).transpose(1, 0, 2)
    tws = np.sin(tw).reshape(N1, nj, tn2).transpose(1, 0, 2)
    cg, sg = _dft_mats(FT_GROUP)
    scale = 1.0 / math.sqrt(T * FT_GROUP)
    eye2 = np.eye(LANES // FT_GROUP)
    cc = np.kron(eye2, cg) * scale
    sc = np.kron(eye2, sg) * scale
    f = lambda a: jnp.asarray(a, F32)
    full2 = lambda n, m: pl.BlockSpec((n, m), lambda b, j: (0, 0))

    ar, ai = pl.pallas_call(
        functools.partial(_fft1_kernel, tn2=tn2, ch=ch),
        grid=(B, nj),
        in_specs=[pl.BlockSpec((1, N1, tn2 * ch), lambda b, j: (b, 0, j)),
                  full2(N1, N1), full2(N1, N1),
                  pl.BlockSpec((1, N1, tn2), lambda b, j: (j, 0, 0)),
                  pl.BlockSpec((1, N1, tn2), lambda b, j: (j, 0, 0))],
        out_specs=[pl.BlockSpec((1, tn2, N1, ch), lambda b, j: (b, j, 0, 0))] * 2,
        out_shape=[jax.ShapeDtypeStruct((B, N2, N1, ch), F32)] * 2,
        compiler_params=_cparams("parallel", "parallel"),
        name="fft_stage1",
    )(u.reshape(B, N1, N2 * ch), f(c1), f(s1), f(twc), f(tws))

    tk1 = min(SUBLANES, N1)
    blk = pl.BlockSpec((1, N2, tk1 * ch), lambda b, j: (b, 0, j))
    out = pl.pallas_call(
        _fft2_kernel,
        grid=(B, N1 // tk1),
        in_specs=[blk, blk, full2(N2, N2), full2(N2, N2), full2(LANES, LANES), full2(LANES, LANES)],
        out_specs=blk,
        out_shape=jax.ShapeDtypeStruct((B, N2, N1 * ch), F32),
        compiler_params=_cparams("parallel", "parallel"),
        name="fft_stage2",
    )(ar.reshape(B, N2, N1 * ch), ai.reshape(B, N2, N1 * ch), f(c2), f(s2), f(cc), f(sc))
    return out.reshape(B, T, ch)


HEAD_SLAB = LANES
ROPE_SHIFT = HEAD_SLAB - QK_ROPE
ONES_ROWS = 16
V_ROWS = V_HEAD + ONES_ROWS


def _rms(x, w):
    return x * lax.rsqrt(jnp.mean(x * x, axis=-1, keepdims=True) + RMS_EPS) * w


def _qproj_kernel(cq_ref, nw_ref, w_ref, ct_ref, st_ref, q_o):
    q = _dot(_rms(cq_ref[0], nw_ref[...]).astype(BF16), w_ref[...])
    ct = ct_ref[...]
    st = st_ref[...]
    for h in range(MLA_HEADS):
        sl = slice(h * HEAD_SLAB, (h + 1) * HEAD_SLAB)
        s = q[:, sl]
        q_o[0, :, sl] = ((s * ct + pltpu.roll(s, ROPE_SHIFT, 1) * st) * Q_SCALE).astype(BF16)


def _kvproj_kernel(ckv_ref, nw_ref, wk_ref, wvt_ref, ct_ref, st_ref, k_o, vt_o):
    x = ckv_ref[0]
    n = _rms(x[:, :KV_RANK], nw_ref[...]).astype(BF16)
    rs = x[:, KV_RANK:KV_RANK + HEAD_SLAB]
    rope = rs * ct_ref[...] + pltpu.roll(rs, ROPE_SHIFT, 1) * st_ref[...]
    kn = _dot(n, wk_ref[...])
    for h in range(MLA_HEADS):
        sl = slice(h * HEAD_SLAB, (h + 1) * HEAD_SLAB)
        k_o[0, :, sl] = (kn[:, sl] + rope).astype(BF16)
    vt = lax.dot_general(wvt_ref[...], n, NT, preferred_element_type=F32)
    row = lax.broadcasted_iota(jnp.int32, vt.shape, 0)
    vt_o[0] = jnp.where(row % V_ROWS >= V_HEAD, 1.0, vt).astype(BF16)


def q_projection(cq, nw, w, ct, st):
    B, T, R = cq.shape
    tm = _tile(T, 256)
    W = MLA_HEADS * HEAD_SLAB
    tab = pl.BlockSpec((tm, HEAD_SLAB), lambda b, i: (i, 0))
    return pl.pallas_call(
        _qproj_kernel,
        grid=(B, T // tm),
        in_specs=[pl.BlockSpec((1, tm, R), lambda b, i: (b, i, 0)),
                  pl.BlockSpec(nw.shape, lambda b, i: (0, 0)),
                  pl.BlockSpec(w.shape, lambda b, i: (0, 0)), tab, tab],
        out_specs=pl.BlockSpec((1, tm, W), lambda b, i: (b, i, 0)),
        out_shape=jax.ShapeDtypeStruct((B, T, W), BF16),
        compiler_params=_cparams("parallel", "parallel"),
        name="q_projection",
    )(cq, nw, w, ct, st)


def kv_projection(ckv, nw, wk, wvt, ct, st):
    B, T, R = ckv.shape
    tm = _tile(T, 256)
    W = MLA_HEADS * HEAD_SLAB
    tab = pl.BlockSpec((tm, HEAD_SLAB), lambda b, i: (i, 0))
    return pl.pallas_call(
        _kvproj_kernel,
        grid=(B, T // tm),
        in_specs=[pl.BlockSpec((1, tm, R), lambda b, i: (b, i, 0)),
                  pl.BlockSpec(nw.shape, lambda b, i: (0, 0)),
                  pl.BlockSpec(wk.shape, lambda b, i: (0, 0)),
                  pl.BlockSpec(wvt.shape, lambda b, i: (0, 0)), tab, tab],
        out_specs=[pl.BlockSpec((1, tm, W), lambda b, i: (b, i, 0)),
                   pl.BlockSpec((1, MLA_HEADS * V_ROWS, tm), lambda b, i: (b, 0, i))],
        out_shape=[jax.ShapeDtypeStruct((B, T, W), BF16), jax.ShapeDtypeStruct((B, MLA_HEADS * V_ROWS, T), BF16)],
        compiler_params=_cparams("parallel", "parallel"),
        name="kv_projection",
    )(ckv, nw, wk, wvt, ct, st)


def _attn_kernel(q_ref, k_ref, vt_ref, o_ref, m_scr, acc_scr, sa_scr, sb_scr, ma_scr, mb_scr, *, tkc, nkc):
    m_scr[...] = jnp.full(m_scr.shape, -jnp.inf, F32)
    acc_scr[...] = jnp.zeros(acc_scr.shape, F32)
    buf_a = (sa_scr, ma_scr)
    buf_b = (sb_scr, mb_scr)

    def chunk(c):
        return pl.ds(pl.multiple_of(c * tkc, tkc), tkc)

    def scores(c, dst, dmax):
        for hh in range(2):
            q = q_ref[0, :, hh * HEAD_SLAB:(hh + 1) * HEAD_SLAB]
            kc = k_ref[0, chunk(c), hh * HEAD_SLAB:(hh + 1) * HEAD_SLAB]
            s = lax.dot_general(kc, q, NT, preferred_element_type=F32)
            dst[hh] = s
            dmax[hh] = jnp.max(s, axis=0, keepdims=True)

    def consume(src, smax, c):
        for hh in range(2):
            m_old = m_scr[hh]
            m_new = jnp.maximum(m_old, smax[hh])
            pr = jnp.exp2((src[hh] - m_new).astype(BF16))
            alpha = jnp.exp2(m_old - m_new)
            m_scr[hh] = m_new
            rows = slice(hh * V_ROWS, (hh + 1) * V_ROWS)
            acc_scr[rows, :] = alpha * acc_scr[rows, :] + _dot(vt_ref[0, rows, chunk(c)], pr)

    scores(0, *buf_a)
    unroll = 2 if nkc > 4 else 1
    ngroups = (nkc - 1) // (2 * unroll)

    def body(i, carry):
        c = 2 * unroll * i
        for _ in range(unroll):
            scores(c + 1, *buf_b)
            consume(*buf_a, c)
            scores(c + 2, *buf_a)
            consume(*buf_b, c + 1)
            c = c + 2
        return carry

    lax.fori_loop(0, ngroups, body, 0)
    done = 2 * unroll * ngroups
    for cc in range(done, nkc):
        src, dst = (buf_a, buf_b) if (cc - done) % 2 == 0 else (buf_b, buf_a)
        if cc + 1 < nkc:
            scores(cc + 1, *dst)
        consume(*src, cc)
    outs = []
    for hh in range(2):
        base = hh * V_ROWS
        outs.append(acc_scr[base:base + V_HEAD, :] / acc_scr[base + V_HEAD:base + V_HEAD + 1, :])
    o_ref[0] = jnp.concatenate(outs, axis=0).T.astype(o_ref.dtype)


def attention(q, k, vt):
    B, T, _ = q.shape
    Tk = k.shape[1]
    tq = _tile(T, 256)
    tkc = next(c for c in (640, 512, 256, 128) if Tk % c == 0)
    hp = MLA_HEADS // 2
    return pl.pallas_call(
        functools.partial(_attn_kernel, tkc=tkc, nkc=Tk // tkc),
        grid=(B, hp, T // tq),
        in_specs=[pl.BlockSpec((1, tq, 2 * HEAD_SLAB), lambda b, h, i: (b, i, h)),
                  pl.BlockSpec((1, Tk, 2 * HEAD_SLAB), lambda b, h, i: (b, 0, h)),
                  pl.BlockSpec((1, 2 * V_ROWS, Tk), lambda b, h, i: (b, h, 0))],
        out_specs=pl.BlockSpec((1, tq, 2 * V_HEAD), lambda b, h, i: (b, i, h)),
        out_shape=jax.ShapeDtypeStruct((B, T, MLA_WIDTH), BF16),
        scratch_shapes=[pltpu.VMEM((2, 1, tq), F32), pltpu.VMEM((2 * V_ROWS, tq), F32),
                        pltpu.VMEM((2, tkc, tq), F32), pltpu.VMEM((2, tkc, tq), F32),
                        pltpu.VMEM((2, 1, tq), F32), pltpu.VMEM((2, 1, tq), F32)],
        compiler_params=_cparams("parallel", "parallel", "arbitrary"),
        name="mla_attention",
    )(q, k, vt)


def _outproj_kernel(rw_ref, ft_ref, att_ref, x_ref, g_ref, lw_ref, lb_ref, w_ref, o_ref, *, alpha):
    w = w_ref
    mix = (_dot(rw_ref[0].astype(BF16), w[0:RW_WIDTH, :])
           + _dot(ft_ref[0].astype(BF16), w[RW_WIDTH:RW_WIDTH + FT_WIDTH, :])
           + _dot(att_ref[0], w[RW_WIDTH + FT_WIDTH:, :]))
    z = alpha * x_ref[0] + g_ref[0] * mix
    o_ref[0] = _standardize(z, LN_EPS) * lw_ref[...] + lb_ref[...]


def out_projection_ln(rw, ft, att, x, gate, ln_w, ln_b, w_out, alpha):
    B, T, D = x.shape
    tm = _tile(T, 256)
    tok = lambda n: pl.BlockSpec((1, tm, n), lambda b, i: (b, i, 0))
    row = pl.BlockSpec((1, D), lambda b, i: (0, 0))
    return pl.pallas_call(
        functools.partial(_outproj_kernel, alpha=alpha),
        grid=(B, T // tm),
        in_specs=[tok(RW_WIDTH), tok(FT_WIDTH), tok(MLA_WIDTH), tok(D),
                  pl.BlockSpec((1, 1, D), lambda b, i: (b, 0, 0)), row, row,
                  pl.BlockSpec(w_out.shape, lambda b, i: (0, 0))],
        out_specs=tok(D),
        out_shape=jax.ShapeDtypeStruct((B, T, D), F32),
        compiler_params=_cparams("parallel", "parallel"),
        name="out_projection_ln",
    )(rw, ft, att, x, gate, ln_w, ln_b, w_out)


def _ffn_kernel(x_ref, sh_ref, sc_ref, g_ref, lw_ref, lb_ref, w1_ref, w3_ref, w2_ref, o_ref, h_scr, acc_scr, *, alpha):
    f = pl.program_id(2)

    @pl.when(f == 0)
    def _():
        h = _standardize(x_ref[0], MOD_EPS) * (1.0 + sc_ref[0]) + sh_ref[0]
        h_scr[...] = h.astype(BF16)
        acc_scr[...] = jnp.zeros(acc_scr.shape, F32)

    hb = h_scr[...]
    a = _dot(hb, w1_ref[...])
    b = _dot(hb, w3_ref[...])
    acc_scr[...] += _dot((a * jax.nn.sigmoid(a) * b).astype(BF16), w2_ref[...])

    @pl.when(f == pl.num_programs(2) - 1)
    def _():
        z = alpha * x_ref[0] + g_ref[0] * acc_scr[...]
        o_ref[0] = _standardize(z, LN_EPS) * lw_ref[...] + lb_ref[...]


def ffn_ln(x, shift, scale, gate, ln_w, ln_b, w1, w3, w2, alpha):
    B, T, D = x.shape
    F = w1.shape[1]
    tm = _tile(T, 512)
    tf = _tile(F, 512)
    tok = pl.BlockSpec((1, tm, D), lambda b, i, f: (b, i, 0))
    vec = pl.BlockSpec((1, 1, D), lambda b, i, f: (b, 0, 0))
    row = pl.BlockSpec((1, D), lambda b, i, f: (0, 0))
    return pl.pallas_call(
        functools.partial(_ffn_kernel, alpha=alpha),
        grid=(B, T // tm, F // tf),
        in_specs=[tok, vec, vec, vec, row, row,
                  pl.BlockSpec((D, tf), lambda b, i, f: (0, f)),
                  pl.BlockSpec((D, tf), lambda b, i, f: (0, f)),
                  pl.BlockSpec((tf, D), lambda b, i, f: (f, 0))],
        out_specs=tok,
        out_shape=jax.ShapeDtypeStruct((B, T, D), F32),
        scratch_shapes=[pltpu.VMEM((tm, D), BF16), pltpu.VMEM((tm, D), F32)],
        compiler_params=_cparams("parallel", "parallel", "arbitrary"),
        name="ffn_ln",
    )(x, shift, scale, gate, ln_w, ln_b, w1, w3, w2)


def _router_kernel(x_ref, sh_ref, sc_ref, wr_ref, h_o, ti_o, tg_o):
    h = _standardize(x_ref[0], MOD_EPS) * (1.0 + sc_ref[0]) + sh_ref[0]
    h_o[0] = h.astype(BF16)
    logits = _dot(h, wr_ref[...], HI)
    lane = lax.broadcasted_iota(jnp.int32, logits.shape, 1)
    neg = jnp.float32(-jnp.inf)
    logits = jnp.where(lane < N_EXPERTS, logits, neg)
    m1 = jnp.max(logits, axis=-1, keepdims=True)
    i1 = jnp.min(jnp.where(logits == m1, lane, LANES), axis=-1, keepdims=True)
    rest = jnp.where(lane == i1, neg, logits)
    m2 = jnp.max(rest, axis=-1, keepdims=True)
    i2 = jnp.min(jnp.where(rest == m2, lane, LANES), axis=-1, keepdims=True)
    e = jnp.exp(m2 - m1)
    g1 = 1.0 / (1.0 + e)
    g2 = e / (1.0 + e)
    ti_o[0] = jnp.where(lane == 0, i1, jnp.where(lane == 1, i2, 0))
    tg_o[0] = jnp.where(lane == 0, g1, jnp.where(lane == 1, g2, 0.0))


def moe_router(x, shift, scale, wr):
    B, T, D = x.shape
    tm = _tile(T, 512)
    tok = lambda n: pl.BlockSpec((1, tm, n), lambda b, i: (b, i, 0))
    vec = pl.BlockSpec((1, 1, D), lambda b, i: (b, 0, 0))
    return pl.pallas_call(
        _router_kernel,
        grid=(B, T // tm),
        in_specs=[tok(D), vec, vec, pl.BlockSpec(wr.shape, lambda b, i: (0, 0))],
        out_specs=[tok(D), tok(LANES), tok(LANES)],
        out_shape=[jax.ShapeDtypeStruct((B, T, D), BF16), jax.ShapeDtypeStruct((B, T, LANES), jnp.int32),
                   jax.ShapeDtypeStruct((B, T, LANES), F32)],
        compiler_params=_cparams("parallel", "parallel"),
        name="moe_router",
    )(x, shift, scale, wr)


def _expert_kernel(be_ref, bv_ref, x_ref, sg_ref, w1_ref, w3_ref, w2_ref, o_ref, acc_scr):
    blk = pl.program_id(0)
    f = pl.program_id(1)
    valid = bv_ref[blk] > 0

    @pl.when(f == 0)
    def _():
        acc_scr[...] = jnp.zeros(acc_scr.shape, F32)

    @pl.when(valid)
    def _():
        xb = x_ref[...]
        a = _dot(xb, w1_ref[0].astype(BF16))
        b = _dot(xb, w3_ref[0].astype(BF16))
        acc_scr[...] += _dot((a * jax.nn.sigmoid(a) * b).astype(BF16), w2_ref[0].astype(BF16))

    @pl.when(f == pl.num_programs(1) - 1)
    def _():
        o_ref[...] = acc_scr[...] * sg_ref[...]


def expert_ffn(xg, slot_gate, blk_expert, blk_valid, w1, w3, w2):
    P, D = xg.shape
    F = w1.shape[2]
    tf = _tile(F, 256)
    nblk = P // MOE_ROWS
    grid_spec = pltpu.PrefetchScalarGridSpec(
        num_scalar_prefetch=2,
        grid=(nblk, F // tf),
        in_specs=[pl.BlockSpec((MOE_ROWS, D), lambda i, f, be, bv: (i, 0)),
                  pl.BlockSpec((MOE_ROWS, 1), lambda i, f, be, bv: (i, 0)),
                  pl.BlockSpec((1, D, tf), lambda i, f, be, bv: (be[i], 0, jnp.where(bv[i] > 0, f, 0))),
                  pl.BlockSpec((1, D, tf), lambda i, f, be, bv: (be[i], 0, jnp.where(bv[i] > 0, f, 0))),
                  pl.BlockSpec((1, tf, D), lambda i, f, be, bv: (be[i], jnp.where(bv[i] > 0, f, 0), 0))],
        out_specs=pl.BlockSpec((MOE_ROWS, D), lambda i, f, be, bv: (i, 0)),
        scratch_shapes=[pltpu.VMEM((MOE_ROWS, D), F32)],
    )
    return pl.pallas_call(
        _expert_kernel,
        grid_spec=grid_spec,
        out_shape=jax.ShapeDtypeStruct((P, D), F32),
        compiler_params=_cparams("parallel", "arbitrary"),
        name="expert_ffn",
    )(blk_expert, blk_valid, xg, slot_gate, w1, w3, w2)


def _residual_ln_kernel(x_ref, y_ref, g_ref, lw_ref, lb_ref, o_ref, *, alpha):
    z = alpha * x_ref[0] + g_ref[0] * y_ref[0]
    o_ref[0] = _standardize(z, LN_EPS) * lw_ref[...] + lb_ref[...]


def residual_ln(x, y, gate, ln_w, ln_b, alpha):
    B, T, D = x.shape
    tm = _tile(T, 512)
    tok = pl.BlockSpec((1, tm, D), lambda b, i: (b, i, 0))
    row = pl.BlockSpec((1, D), lambda b, i: (0, 0))
    return pl.pallas_call(
        functools.partial(_residual_ln_kernel, alpha=alpha),
        grid=(B, T // tm),
        in_specs=[tok, tok, pl.BlockSpec((1, 1, D), lambda b, i: (b, 0, 0)), row, row],
        out_specs=tok,
        out_shape=jax.ShapeDtypeStruct((B, T, D), F32),
        compiler_params=_cparams("parallel", "parallel"),
        name="residual_ln",
    )(x, y, gate, ln_w, ln_b)


def moe_ln(x, shift, scale, gate, ln_w, ln_b, router, w1, w3, w2, alpha):
    B, T, D = x.shape
    N = B * T
    E = router.shape[1]
    wr = jnp.zeros((D, LANES), F32).at[:, :E].set(router)
    h, ti, tg = moe_router(x, shift, scale, wr)
    top_i = ti.reshape(N, LANES)[:, :2]
    top_g = tg.reshape(N, LANES)[:, :2]
    e_flat = top_i.reshape(-1)
    onehot = (e_flat[:, None] == jnp.arange(E, dtype=jnp.int32)[None, :]).astype(jnp.int32)
    rank = jnp.sum((jnp.cumsum(onehot, axis=0) - onehot) * onehot, axis=1)
    counts = jnp.sum(onehot, axis=0)
    padded = (counts + MOE_ROWS - 1) // MOE_ROWS * MOE_ROWS
    p_ends = jnp.cumsum(padded)
    p_starts = p_ends - padded
    dest = p_starts[e_flat] + rank
    P = -(-(2 * N) // MOE_ROWS) * MOE_ROWS + E * MOE_ROWS
    nblk = P // MOE_ROWS
    tok_flat = jnp.repeat(jnp.arange(N, dtype=jnp.int32), 2)
    slot_tok = jnp.zeros((P,), jnp.int32).at[dest].set(tok_flat)
    slot_gate = jnp.zeros((P,), F32).at[dest].set(top_g.reshape(-1))
    blk_start = jnp.arange(nblk, dtype=jnp.int32) * MOE_ROWS
    blk_expert = jnp.minimum(jnp.searchsorted(p_ends, blk_start, side="right"), E - 1).astype(jnp.int32)
    blk_valid = (blk_start < p_ends[-1]).astype(jnp.int32)
    xg = h.reshape(N, D)[slot_tok]
    yg = expert_ffn(xg, slot_gate[:, None], blk_expert, blk_valid, w1, w3, w2)
    d2 = dest.reshape(N, 2)
    y = (yg[d2[:, 0]] + yg[d2[:, 1]]).reshape(B, T, D)
    return residual_ln(x, y, gate, ln_w, ln_b, alpha)


def _rope_partner(w):
    half = ROPE_AXIS // 2
    idx = np.arange(QK_ROPE)
    first = (idx % ROPE_AXIS) < half
    src = np.where(first, idx + half, idx - half)
    sign = np.where(first, -1.0, 1.0).astype(np.float32)
    return w[:, src] * sign


def _layer_params(l, w_in, rw_conv, rw_w0, rw_w_up, rw_a0, rw_a_up, rw_g_up, rw_k_k, rw_k_a, rw_r_k, rw_gn_w,
                  rw_gn_b, mla_q_norm, mla_w_uq, mla_kv_norm, mla_w_ukv, w_out):
    D = w_in.shape[1]
    wi = w_in[l]
    o = np.cumsum([0, 3 * RW_WIDTH, G_RANK, LORA_RANK, LORA_RANK, LORA_RANK, LORA_RANK, FT_WIDTH, Q_RANK, KV_RANK, QK_ROPE])
    piece = lambda i: wi[:, o[i]:o[i + 1]]
    zpad = lambda n: jnp.zeros((D, n), F32)
    lora = [jnp.concatenate([piece(i), zpad(LANES - LORA_RANK)], axis=1) for i in (2, 3, 4, 5)]
    w_rw = jnp.concatenate([piece(0), piece(1)] + lora, axis=1)
    kr = piece(9)
    w_ckv = jnp.concatenate([piece(8), zpad(QK_NOPE), kr, _rope_partner(kr)], axis=1)
    pad_rows = lambda a: jnp.concatenate([a, jnp.zeros((a.shape[0], LANES - LORA_RANK, a.shape[2]), F32)], axis=1)
    head = jnp.arange(RW_WIDTH) // RW_HEAD
    uq = mla_w_uq[l].reshape(Q_RANK, MLA_HEADS, QK_NOPE + QK_ROPE)
    uq_rope = uq[:, :, QK_NOPE:]
    uq_partner = _rope_partner(uq_rope.reshape(Q_RANK * MLA_HEADS, QK_ROPE)).reshape(Q_RANK, MLA_HEADS, QK_ROPE)
    w_q = jnp.concatenate([uq, uq_partner], axis=2).reshape(Q_RANK, MLA_HEADS * HEAD_SLAB)
    ukv = mla_w_ukv[l].reshape(KV_RANK, MLA_HEADS, QK_NOPE + V_HEAD)
    w_k = jnp.concatenate([ukv[:, :, :QK_NOPE], jnp.zeros((KV_RANK, MLA_HEADS, HEAD_SLAB - QK_NOPE), F32)],
                          axis=2).reshape(KV_RANK, MLA_HEADS * HEAD_SLAB)
    w_v = jnp.concatenate([ukv[:, :, QK_NOPE:], jnp.zeros((KV_RANK, MLA_HEADS, ONES_ROWS), F32)],
                          axis=2).reshape(KV_RANK, MLA_HEADS * V_ROWS)
    return {
        "w_in": [w.astype(BF16) for w in (w_rw, piece(6), piece(7), w_ckv)],
        "rw": {"conv": rw_conv[l], "k_k": rw_k_k[l][None], "k_a": rw_k_a[l][None], "r_k": rw_r_k[l].reshape(1, RW_WIDTH),
               "w0": rw_w0[l], "a0": rw_a0[l], "w_up": pad_rows(rw_w_up[l]).astype(BF16), "a_up": pad_rows(rw_a_up[l]).astype(BF16),
               "g_up": rw_g_up[l].astype(BF16), "gn_w": rw_gn_w[l][None], "gn_b": rw_gn_b[l][None],
               "E": (head[:, None] == head[None, :]).astype(BF16)},
        "q_norm": mla_q_norm[l][None], "w_q": w_q.astype(BF16),
        "kv_norm": mla_kv_norm[l][None], "w_k": w_k.astype(BF16), "w_vt": w_v.T.astype(BF16),
        "w_out": w_out[l].astype(BF16),
    }


def _rope_tables(T, use_rope):
    ones = jnp.ones((T, QK_NOPE), F32)
    zeros = jnp.zeros((T, QK_NOPE), F32)
    zpad = jnp.zeros((T, HEAD_SLAB - QK_NOPE - QK_ROPE), F32)
    if use_rope:
        row = jnp.repeat(jnp.arange(T // GRID_W), GRID_W).astype(F32)
        col = (jnp.arange(T) % GRID_W).astype(F32)
        inv = ROPE_THETA ** (-jnp.arange(0, ROPE_AXIS, 2, dtype=F32) / ROPE_AXIS)
        ang = jnp.stack([row[:, None] * inv, col[:, None] * inv], axis=1)
        ang = jnp.broadcast_to(ang[:, :, None, :], (T, 2, 2, ROPE_AXIS // 2)).reshape(T, QK_ROPE)
        cos, sin = jnp.cos(ang), jnp.sin(ang)
    else:
        cos, sin = jnp.ones((T, QK_ROPE), F32), jnp.zeros((T, QK_ROPE), F32)
    return jnp.concatenate([ones, cos, zpad], axis=1), jnp.concatenate([zeros, sin, zpad], axis=1)


def _mixer(h_pieces_lat, h_pieces_ctx, p, tabs_lat, tabs_ctx, emit_ctx):
    rw_l, ft_l, cq_l, ckv_l = h_pieces_lat
    rw_c, ft_c, cq_c, ckv_c = h_pieces_ctx
    rwo_l, rwo_c = rwkv_branch(rw_l, rw_c, p["rw"], emit_ctx)
    fto_l = fourier_mixer(ft_l)
    q_l = q_projection(cq_l, p["q_norm"], p["w_q"], *tabs_lat)
    k_l, vt_l = kv_projection(ckv_l, p["kv_norm"], p["w_k"], p["w_vt"], *tabs_lat)
    k_c, vt_c = kv_projection(ckv_c, p["kv_norm"], p["w_k"], p["w_vt"], *tabs_ctx)
    att_l = attention(q_l, jnp.concatenate([k_l, k_c], axis=1), jnp.concatenate([vt_l, vt_c], axis=2))
    out_c = None
    if emit_ctx:
        fto_c = fourier_mixer(ft_c)
        q_c = q_projection(cq_c, p["q_norm"], p["w_q"], *tabs_ctx)
        att_c = attention(q_c, k_c, vt_c)
        out_c = (rwo_c, fto_c, att_c)
    return (rwo_l, fto_l, att_l), out_c


def kernel(x, c, ctx, c_ctx, ada_w, ada_b, w_in, rw_conv, rw_w0, rw_w_up, rw_a0, rw_a_up, rw_g_up, rw_k_k, rw_k_a,
           rw_r_k, rw_gn_w, rw_gn_b, mla_q_norm, mla_w_uq, mla_kv_norm, mla_w_ukv, w_out, ln1_w, ln1_b, ln2_w, ln2_b,
           ffn_w1, ffn_w3, ffn_w2, moe_router, moe_w1, moe_w3, moe_w2):
    B, T, D = x.shape
    Tc = ctx.shape[1]
    depth = w_in.shape[0]
    alpha = (2 * depth) ** 0.25
    assert B + 1 <= SUBLANES
    cc = jnp.zeros((SUBLANES, D), F32).at[:B].set(c).at[B].set(c_ctx)
    ada = ada_vectors(cc, ada_w, ada_b)
    tabs_lat = _rope_tables(T, True)
    tabs_ctx = _rope_tables(Tc, False)
    for l in range(depth):
        last = l == depth - 1
        p = _layer_params(l, w_in, rw_conv, rw_w0, rw_w_up, rw_a0, rw_a_up, rw_g_up, rw_k_k, rw_k_a, rw_r_k,
                          rw_gn_w, rw_gn_b, mla_q_norm, mla_w_uq, mla_kv_norm, mla_w_ukv, w_out)
        mods = ada[l].reshape(SUBLANES, 6, D)
        lat = [mods[:B, j][:, None, :] for j in range(6)]
        cx = [jnp.broadcast_to(mods[B, j][None, None, :], (B, 1, D)) for j in range(6)]
        sh_m, sc_m, g_m, sh_f, sc_f, g_f = lat
        csh_m, csc_m, cg_m, csh_f, csc_f, cg_f = cx
        ln1 = (ln1_w[l][None], ln1_b[l][None])
        ln2 = (ln2_w[l][None], ln2_b[l][None])

        pieces_l = in_projection(x, sh_m, sc_m, p["w_in"])
        pieces_c = in_projection(ctx, csh_m, csc_m, p["w_in"])
        mix_l, mix_c = _mixer(pieces_l, pieces_c, p, tabs_lat, tabs_ctx, not last)
        x = out_projection_ln(*mix_l, x, g_m, *ln1, p["w_out"], alpha)
        i = l // 2
        if l % 2 == 0:
            dense = (ffn_w1[i].astype(BF16), ffn_w3[i].astype(BF16), ffn_w2[i].astype(BF16))
            x = ffn_ln(x, sh_f, sc_f, g_f, *ln2, *dense, alpha)
        else:
            x = moe_ln(x, sh_f, sc_f, g_f, *ln2, moe_router[i], moe_w1[i], moe_w3[i], moe_w2[i], alpha)
        if not last:
            ctx = out_projection_ln(*mix_c, ctx, cg_m, *ln1, p["w_out"], alpha)
            if l % 2 == 0:
                ctx = ffn_ln(ctx, csh_f, csc_f, cg_f, *ln2, *dense, alpha)
            else:
                ctx = moe_ln(ctx, csh_f, csc_f, cg_f, *ln2, moe_router[i], moe_w1[i], moe_w3[i], moe_w2[i], alpha)
    return x
```

```python
import functools
import math

import numpy as np
import jax
import jax.numpy as jnp
from jax import lax
from jax.experimental import pallas as pl
from jax.experimental.pallas import tpu as pltpu

F32 = jnp.float32
BF16 = jnp.bfloat16
HI = lax.Precision.HIGHEST

LANES = 128
SUBLANES = 8
VMEM_LIMIT = 56 * 1024 * 1024

GRID_W = 64
RW_HEADS = 8
RW_HEAD = 64
RW_WIDTH = RW_HEADS * RW_HEAD
G_RANK = 128
LORA_RANK = 64
GN_EPS = 64e-5
FT_GROUP = 64
FT_WIDTH = 512
MLA_HEADS = 16
QK_NOPE = 64
QK_ROPE = 32
V_HEAD = 64
MLA_WIDTH = MLA_HEADS * V_HEAD
Q_RANK = 512
KV_RANK = 256
ROPE_AXIS = QK_ROPE // 2
ROPE_THETA = 10000.0
ATTN_SCALE = (QK_NOPE + QK_ROPE) ** -0.5
Q_SCALE = ATTN_SCALE * math.log2(math.e)
N_EXPERTS = 8
LN_EPS = 1e-5
MOD_EPS = 1e-6
RMS_EPS = 1e-6
CHUNK = 64
INV_BLOCK = 16
MOE_ROWS = 1024

NT = (((1,), (1,)), ((), ()))
TN = (((0,), (0,)), ((), ()))


def _cparams(*sem):
    return pltpu.CompilerParams(dimension_semantics=sem, vmem_limit_bytes=VMEM_LIMIT)


def _dot(a, b, prec=None):
    return jnp.dot(a, b, precision=prec, preferred_element_type=F32)


def _dot_split(x, w):
    hi = x.astype(BF16)
    r1 = x - hi.astype(F32)
    mid = r1.astype(BF16)
    lo = (r1 - mid.astype(F32)).astype(BF16)
    return _dot(hi, w) + _dot(mid, w) + _dot(lo, w)


def _dot3(a, b):
    a_hi = a.astype(BF16)
    b_hi = b.astype(BF16)
    a_lo = (a - a_hi.astype(F32)).astype(BF16)
    b_lo = (b - b_hi.astype(F32)).astype(BF16)
    return _dot(a_hi, b_hi) + _dot(a_hi, b_lo) + _dot(a_lo, b_hi)


def _standardize(x, eps):
    mu = jnp.mean(x, axis=-1, keepdims=True)
    xc = x - mu
    var = jnp.mean(xc * xc, axis=-1, keepdims=True)
    return xc * lax.rsqrt(var + eps)


def _tile(n, pref):
    t = min(n, pref)
    assert n % t == 0, (n, pref)
    return t


def _ada_kernel(c_ref, w_ref, b_ref, o_ref):
    c = c_ref[...]
    s = c * jax.nn.sigmoid(c)
    o_ref[0] = _dot(s, w_ref[0], HI) + b_ref[0]


def ada_vectors(cc, ada_w, ada_b):
    L, D, N6 = ada_w.shape
    tn = _tile(N6, 1024)
    return pl.pallas_call(
        _ada_kernel,
        grid=(L, N6 // tn),
        in_specs=[pl.BlockSpec((SUBLANES, D), lambda l, j: (0, 0)),
                  pl.BlockSpec((1, D, tn), lambda l, j: (l, 0, j)),
                  pl.BlockSpec((1, 1, tn), lambda l, j: (l, 0, j))],
        out_specs=pl.BlockSpec((1, SUBLANES, tn), lambda l, j: (l, 0, j)),
        out_shape=jax.ShapeDtypeStruct((L, SUBLANES, N6), F32),
        compiler_params=_cparams("parallel", "parallel"),
        name="ada_vectors",
    )(cc, ada_w, ada_b.reshape(L, 1, N6))


def _inproj_kernel(x_ref, sh_ref, sc_ref, *refs):
    nw = len(refs) // 2
    h = _standardize(x_ref[0], MOD_EPS) * (1.0 + sc_ref[0]) + sh_ref[0]
    hb = h.astype(BF16)
    for w_ref, o_ref in zip(refs[:nw], refs[nw:]):
        o_ref[0] = _dot(hb, w_ref[...])


def in_projection(x, shift, scale, weights):
    B, T, D = x.shape
    tm = _tile(T, 256)
    vec = pl.BlockSpec((1, 1, D), lambda b, i: (b, 0, 0))
    return pl.pallas_call(
        _inproj_kernel,
        grid=(B, T // tm),
        in_specs=[pl.BlockSpec((1, tm, D), lambda b, i: (b, i, 0)), vec, vec]
        + [pl.BlockSpec(w.shape, lambda b, i: (0, 0)) for w in weights],
        out_specs=[pl.BlockSpec((1, tm, w.shape[1]), lambda b, i: (b, i, 0)) for w in weights],
        out_shape=[jax.ShapeDtypeStruct((B, T, w.shape[1]), F32) for w in weights],
        compiler_params=_cparams("parallel", "parallel"),
        name="in_projection",
    )(x, shift, scale, *weights)


RW_COLS = 3 * RW_WIDTH + G_RANK + 4 * LANES


def _softplus(z):
    return jnp.maximum(z, 0.0) + jnp.log(1.0 + jnp.exp(-jnp.abs(z)))


def _rwkv_prep_kernel(x_ref, xp_ref, xn_ref, conv_ref, kk_ref, ka_ref, rk_ref, w0_ref, a0_ref,
                      wup_ref, aup_ref, gup_ref, e_ref,
                      r_o, v_o, kk_o, ld0_o, b0_o, kr0_o, ld1_o, b1_o, kr1_o, bonus_o, gate_o):
    i = pl.program_id(1)
    n = pl.num_programs(1)
    x = x_ref[0]
    W3 = 3 * RW_WIDTH
    raw = x[:, :W3]
    tm = raw.shape[0]
    row = lax.broadcasted_iota(jnp.int32, (tm, 1), 0)
    prev_row = jnp.where(i > 0, xp_ref[0, SUBLANES - 1:SUBLANES, :], 0.0)
    next_row = jnp.where(i < n - 1, xn_ref[0, 0:1, :], 0.0)
    xm = jnp.where(row == 0, prev_row, pltpu.roll(raw, 1, 0))
    xq = jnp.where(row == tm - 1, next_row, pltpu.roll(raw, tm - 1, 0))
    cw = conv_ref[...]
    y = xm * cw[0:1] + raw * cw[1:2] + xq * cw[2:3]
    r = y[:, :RW_WIDTH]
    k = y[:, RW_WIDTH:2 * RW_WIDTH]
    v = y[:, 2 * RW_WIDTH:W3]
    E = e_ref[...]
    kkv = k * kk_ref[...]
    kk = kkv / jnp.maximum(jnp.sqrt(_dot_split(kkv * kkv, E)), 1e-12)
    r_o[0] = r
    v_o[0] = v
    kk_o[0] = kk
    g_dn = x[:, W3:W3 + G_RANK]
    gate_o[0] = _dot(jax.nn.sigmoid(g_dn).astype(BF16), gup_ref[...])
    bonus = jnp.zeros_like(r)
    outs = ((ld0_o, b0_o, kr0_o), (ld1_o, b1_o, kr1_o))
    for d in range(2):
        base = W3 + G_RANK
        w_dn = x[:, base + d * LANES: base + (d + 1) * LANES]
        a_dn = x[:, base + (2 + d) * LANES: base + (3 + d) * LANES]
        z = w0_ref[d:d + 1, :] + _dot(jnp.tanh(w_dn).astype(BF16), wup_ref[d])
        logw = -_softplus(-z) - 0.5
        a = jax.nn.sigmoid(a0_ref[d:d + 1, :] + _dot(a_dn.astype(BF16), aup_ref[d]))
        kr = k * (1.0 + (a - 1.0) * ka_ref[...])
        ld_o, b_o, kr_o = outs[d]
        ld_o[0] = -jnp.exp(logw)
        b_o[0] = a * kk
        kr_o[0] = kr
        bonus = bonus + _dot_split(r * kr * rk_ref[...], E) * v
    bonus_o[0] = bonus


def rwkv_prep(rw, p):
    B, T, _ = rw.shape
    tm = _tile(T, 256)
    nh = tm // SUBLANES
    last = T // SUBLANES - 1
    W3 = 3 * RW_WIDTH
    full = lambda a: pl.BlockSpec(a.shape, lambda b, i: (0,) * a.ndim)
    params = [p["conv"], p["k_k"], p["k_a"], p["r_k"], p["w0"], p["a0"], p["w_up"], p["a_up"], p["g_up"], p["E"]]
    outs = pl.pallas_call(
        _rwkv_prep_kernel,
        grid=(B, T // tm),
        in_specs=[pl.BlockSpec((1, tm, RW_COLS), lambda b, i: (b, i, 0)),
                  pl.BlockSpec((1, SUBLANES, W3), lambda b, i: (b, jnp.maximum(i * nh - 1, 0), 0)),
                  pl.BlockSpec((1, SUBLANES, W3), lambda b, i: (b, jnp.minimum((i + 1) * nh, last), 0))]
        + [full(a) for a in params],
        out_specs=[pl.BlockSpec((1, tm, RW_WIDTH), lambda b, i: (b, i, 0))] * 11,
        out_shape=[jax.ShapeDtypeStruct((B, T, RW_WIDTH), F32)] * 11,
        compiler_params=_cparams("parallel", "parallel"),
        name="rwkv_prep",
    )(rw, rw, rw, *params)
    return outs


PAIR = 2 * RW_HEAD
N_PAIRS = RW_HEADS // 2
CHUNK_GROUP = 4


def _pair_masks(reverse):
    i = lax.broadcasted_iota(jnp.int32, (PAIR, PAIR), 0)
    j = lax.broadcasted_iota(jnp.int32, (PAIR, PAIR), 1)
    same = (i // CHUNK) == (j // CHUNK)
    strict = same & ((j > i) if reverse else (j < i))
    incl = same & ((j >= i) if reverse else (j <= i))
    blk = (i // INV_BLOCK) == (j // INV_BLOCK)
    eye = jnp.where(i == j, 1.0, 0.0).astype(F32)
    t = lax.broadcasted_iota(jnp.int32, (CHUNK, CHUNK), 0)
    u = lax.broadcasted_iota(jnp.int32, (CHUNK, CHUNK), 1)
    tri = jnp.where((u >= t) if reverse else (u <= t), 1.0, 0.0).astype(F32)
    first = lax.broadcasted_iota(jnp.int32, (1, PAIR), 1) < RW_HEAD
    return strict, incl, blk, eye, tri, first


def _pair_chunk_math(ld, r, v, kk, b, kr, masks):
    strict, incl, blk, eye, tri, first = masks
    P = ld.shape[0]
    ein = lambda spec, a, c, prec=None: jnp.einsum(spec, a, c, precision=prec, preferred_element_type=F32)
    Lc = ein("pct,ptk->pck", jnp.broadcast_to(tri, (P, CHUNK, CHUNK)), ld, HI)
    Lx = Lc - ld
    Lt = jnp.sum(ld, axis=1, keepdims=True)
    ginv = jnp.exp(-Lc)
    gout = jnp.exp(Lt - Lc)
    stack = lambda x: jnp.concatenate([jnp.where(first, x, 0.0), jnp.where(first, 0.0, x)], axis=1).astype(BF16)
    twice = lambda x: jnp.concatenate([x, x], axis=1).astype(BF16)
    Xk = stack(kk * jnp.exp(Lx))
    Xr = stack(r * jnp.exp(Lc))
    Vs = stack(v)
    Bs = stack(b * gout)
    Ks = stack(kr * gout)
    mm = lambda a, c: ein("pij,pjk->pik", a.astype(BF16), c.astype(BF16))
    nt = lambda a, c: ein("pik,pjk->pij", a, c)
    tn = lambda a, c: ein("pji,pjk->pik", a, c.astype(BF16))
    XX = jnp.concatenate([Xk, Xr], axis=1)
    Mb = nt(XX, twice(b * ginv))
    Mk = nt(XX, twice(kr * ginv))
    Mab = jnp.where(strict, Mb[:, :PAIR], 0.0)
    Arb = jnp.where(incl, Mb[:, PAIR:], 0.0)
    Mak = jnp.where(strict, Mk[:, :PAIR], 0.0)
    Ark = jnp.where(incl, Mk[:, PAIR:], 0.0)
    Nd = jnp.where(blk, Mab, 0.0)
    No = Mab - Nd
    N2 = mm(Nd, Nd)
    N4 = mm(N2, N2)
    N8 = mm(N4, N4)
    Td = mm(mm(mm(eye - Nd, eye + N2), eye + N4), eye + N8)
    M2 = mm(Td, No)
    Tm = mm(mm(eye - M2, eye + mm(M2, M2)), Td)
    P1 = mm(Tm, Xk)
    P2 = mm(Tm, mm(Mak, Vs))
    Q1s = Xr.astype(F32) - mm(Arb, P1)
    Yis = mm(Ark, Vs) - mm(Arb, P2)
    G = eye * jnp.exp(Lt) - tn(Bs, P1)
    H = tn(Ks, Vs) - tn(Bs, P2)
    return Q1s[:, :CHUNK] + Q1s[:, CHUNK:], Yis[:, :CHUNK] + Yis[:, CHUNK:], G, H


def _rwkv_chunk_kernel(ld_ref, r_ref, v_ref, kk_ref, b_ref, kr_ref, q_o, yi_o, g_o, h_o, *, nch, reverse):
    masks = _pair_masks(reverse)
    group = min(CHUNK_GROUP, nch)

    def body(i, carry):
        c0 = i * group
        rows = [pl.ds(pl.multiple_of((c0 + u) * CHUNK, CHUNK), CHUNK) for u in range(group)]
        get = lambda ref: jnp.stack([ref[0, rows[u], j * PAIR:(j + 1) * PAIR]
                                     for u in range(group) for j in range(N_PAIRS)])
        Q1, Yi, G, H = _pair_chunk_math(get(ld_ref), get(r_ref), get(v_ref), get(kk_ref), get(b_ref), get(kr_ref),
                                        masks)
        for u in range(group):
            for j in range(N_PAIRS):
                q_o[0, rows[u], j * PAIR:(j + 1) * PAIR] = Q1[u * N_PAIRS + j]
                yi_o[0, rows[u], j * PAIR:(j + 1) * PAIR] = Yi[u * N_PAIRS + j]
        mats = pl.ds(c0 * N_PAIRS, group * N_PAIRS)
        g_o[0, mats] = G
        h_o[0, mats] = H
        return carry

    lax.fori_loop(0, nch // group, body, 0)


def rwkv_chunks(ld, r, v, kk, b, kr, reverse):
    B, T, W = ld.shape
    tt = _tile(T, 4 * CHUNK)
    nch = tt // CHUNK
    tok = pl.BlockSpec((1, tt, W), lambda bb, i: (bb, i, 0))
    mat = pl.BlockSpec((1, nch * N_PAIRS, PAIR, PAIR), lambda bb, i: (bb, i, 0, 0))
    mats = jax.ShapeDtypeStruct((B, T // CHUNK * N_PAIRS, PAIR, PAIR), F32)
    return pl.pallas_call(
        functools.partial(_rwkv_chunk_kernel, nch=nch, reverse=reverse),
        grid=(B, T // tt),
        in_specs=[tok] * 6,
        out_specs=[tok, tok, mat, mat],
        out_shape=[jax.ShapeDtypeStruct((B, T, W), F32)] * 2 + [mats, mats],
        compiler_params=_cparams("parallel", "parallel"),
        name="rwkv_chunks_bwd" if reverse else "rwkv_chunks_fwd",
    )(ld, r, v, kk, b, kr)


def _rwkv_seq_kernel(q_ref, yi_ref, g_ref, h_ref, s0_ref, y_o, sf_o, s_scr, *, nch, reverse):
    @pl.when(pl.program_id(1) == 0)
    def _():
        s_scr[...] = s0_ref[0]

    def body(cc, carry):
        c = (nch - 1 - cc) if reverse else cc
        rows = pl.ds(pl.multiple_of(c * CHUNK, CHUNK), CHUNK)
        for j in range(N_PAIRS):
            lanes = slice(j * PAIR, (j + 1) * PAIR)
            S = s_scr[j]
            y_o[0, rows, lanes] = _dot3(q_ref[0, rows, lanes], S) + yi_ref[0, rows, lanes]
            s_scr[j] = _dot3(g_ref[0, c * N_PAIRS + j], S) + h_ref[0, c * N_PAIRS + j]
        return carry

    lax.fori_loop(0, nch, body, 0)
    sf_o[0] = s_scr[...]


def rwkv_sequential(q1, yi, g, hm, s0, reverse):
    B, T, W = q1.shape
    tt = _tile(T, 8 * CHUNK)
    n = T // tt
    nch = tt // CHUNK
    step = (lambda i: n - 1 - i) if reverse else (lambda i: i)
    tok = pl.BlockSpec((1, tt, W), lambda bb, i: (bb, step(i), 0))
    mat = pl.BlockSpec((1, nch * N_PAIRS, PAIR, PAIR), lambda bb, i: (bb, step(i), 0, 0))
    sspec = pl.BlockSpec((1, N_PAIRS, PAIR, PAIR), lambda bb, i: (bb, 0, 0, 0))
    return pl.pallas_call(
        functools.partial(_rwkv_seq_kernel, nch=nch, reverse=reverse),
        grid=(B, n),
        in_specs=[tok, tok, mat, mat, sspec],
        out_specs=[tok, sspec],
        out_shape=[jax.ShapeDtypeStruct((B, T, W), F32), jax.ShapeDtypeStruct((B, N_PAIRS, PAIR, PAIR), F32)],
        scratch_shapes=[pltpu.VMEM((N_PAIRS, PAIR, PAIR), F32)],
        compiler_params=_cparams("parallel", "arbitrary"),
        name="rwkv_seq_bwd" if reverse else "rwkv_seq_fwd",
    )(q1, yi, g, hm, s0)


def _rwkv_readout_kernel(yf_ref, yb_ref, bonus_ref, gate_ref, gw_ref, gb_ref, e_ref, o_ref):
    E = e_ref[...]
    ys = yf_ref[0] + yb_ref[0]
    yc = ys - _dot_split(ys, E) * (1.0 / RW_HEAD)
    var = _dot_split(yc * yc, E) * (1.0 / RW_HEAD)
    yn = yc * lax.rsqrt(var + GN_EPS) * gw_ref[...] + gb_ref[...]
    o_ref[0] = (yn + bonus_ref[0]) * gate_ref[0]


def rwkv_readout(yf, yb, bonus, gate, p):
    B, T, W = yf.shape
    tm = _tile(T, 512)
    spec = pl.BlockSpec((1, tm, W), lambda b, i: (b, i, 0))
    full = lambda a: pl.BlockSpec(a.shape, lambda b, i: (0,) * a.ndim)
    return pl.pallas_call(
        _rwkv_readout_kernel,
        grid=(B, T // tm),
        in_specs=[spec] * 4 + [full(p["gn_w"]), full(p["gn_b"]), full(p["E"])],
        out_specs=spec,
        out_shape=jax.ShapeDtypeStruct((B, T, W), F32),
        compiler_params=_cparams("parallel", "parallel"),
        name="rwkv_readout",
    )(yf, yb, bonus, gate, p["gn_w"], p["gn_b"], p["E"])


def rwkv_branch(rw_lat, rw_ctx, p, emit_ctx):
    prep_l = rwkv_prep(rw_lat, p)
    prep_c = rwkv_prep(rw_ctx, p)
    B = rw_lat.shape[0]
    s_zero = jnp.zeros((B, N_PAIRS, PAIR, PAIR), F32)

    def scans(prep, s0s):
        r, v, kk = prep[:3]
        ys, finals = [], []
        for d in range(2):
            ld, b, kr = prep[3 + 3 * d: 6 + 3 * d]
            q1, yi, g, hm = rwkv_chunks(ld, r, v, kk, b, kr, reverse=bool(d))
            y, sf = rwkv_sequential(q1, yi, g, hm, s0s[d], reverse=bool(d))
            ys.append(y)
            finals.append(sf)
        return ys, finals

    ys_c, fin_c = scans(prep_c, (s_zero, s_zero))
    ys_l, _ = scans(prep_l, fin_c)
    out_l = rwkv_readout(ys_l[0], ys_l[1], prep_l[9], prep_l[10], p)
    out_c = rwkv_readout(ys_c[0], ys_c[1], prep_c[9], prep_c[10], p) if emit_ctx else None
    return out_l, out_c


def _dft_mats(n):
    a = 2.0 * np.pi * np.outer(np.arange(n), np.arange(n)) / n
    return np.cos(a), np.sin(a)


def _fft1_kernel(u_ref, c_ref, s_ref, twc_ref, tws_ref, ar_o, ai_o, *, tn2, ch):
    U = u_ref[0]
    Ar = _dot3(c_ref[...], U)
    Ai = -_dot3(s_ref[...], U)
    twc = twc_ref[0]
    tws = tws_ref[0]
    for j in range(tn2):
        ct = twc[:, j:j + 1]
        st = tws[:, j:j + 1]
        a_r = Ar[:, j * ch:(j + 1) * ch]
        a_i = Ai[:, j * ch:(j + 1) * ch]
        ar_o[0, j] = a_r * ct + a_i * st
        ai_o[0, j] = a_i * ct - a_r * st


def _fft2_kernel(ar_ref, ai_ref, c_ref, s_ref, cc_ref, sc_ref, o_ref):
    Ar = ar_ref[0]
    Ai = ai_ref[0]
    C = c_ref[...]
    S = s_ref[...]
    Yr = _dot3(C, Ar) + _dot3(S, Ai)
    Yi = _dot3(C, Ai) - _dot3(S, Ar)
    Cc = cc_ref[...]
    Sc = sc_ref[...]
    for m in range(Ar.shape[1] // LANES):
        sl = slice(m * LANES, (m + 1) * LANES)
        o_ref[0, :, sl] = _dot3(Yr[:, sl], Cc) + _dot3(Yi[:, sl], Sc)


def fourier_mixer(u):
    B, T, ch = u.shape
    lg = int(round(math.log2(T)))
    assert 1 << lg == T
    N1 = 1 << ((lg + 1) // 2)
    N2 = T // N1
    c1, s1 = _dft_mats(N1)
    c2, s2 = _dft_mats(N2)
    tw = 2.0 * np.pi * np.outer(np.arange(N1), np.arange(N2)) / T
    tn2 = min(SUBLANES, N2)
    nj = N2 // tn2
    twc = np.cos(tw).reshape(N1, nj, tn2).transpose(1, 0, 2)
    tws = np.sin(tw).reshape(N1, nj, tn2).transpose(1, 0, 2)
    cg, sg = _dft_mats(FT_GROUP)
    scale = 1.0 / math.sqrt(T * FT_GROUP)
    eye2 = np.eye(LANES // FT_GROUP)
    cc = np.kron(eye2, cg) * scale
    sc = np.kron(eye2, sg) * scale
    f = lambda a: jnp.asarray(a, F32)
    full2 = lambda n, m: pl.BlockSpec((n, m), lambda b, j: (0, 0))

    ar, ai = pl.pallas_call(
        functools.partial(_fft1_kernel, tn2=tn2, ch=ch),
        grid=(B, nj),
        in_specs=[pl.BlockSpec((1, N1, tn2 * ch), lambda b, j: (b, 0, j)),
                  full2(N1, N1), full2(N1, N1),
                  pl.BlockSpec((1, N1, tn2), lambda b, j: (j, 0, 0)),
                  pl.BlockSpec((1, N1, tn2), lambda b, j: (j, 0, 0))],
        out_specs=[pl.BlockSpec((1, tn2, N1, ch), lambda b, j: (b, j, 0, 0))] * 2,
        out_shape=[jax.ShapeDtypeStruct((B, N2, N1, ch), F32)] * 2,
        compiler_params=_cparams("parallel", "parallel"),
        name="fft_stage1",
    )(u.reshape(B, N1, N2 * ch), f(c1), f(s1), f(twc), f(tws))

    tk1 = min(SUBLANES, N1)
    blk = pl.BlockSpec((1, N2, tk1 * ch), lambda b, j: (b, 0, j))
    out = pl.pallas_call(
        _fft2_kernel,
        grid=(B, N1 // tk1),
        in_specs=[blk, blk, full2(N2, N2), full2(N2, N2), full2(LANES, LANES), full2(LANES, LANES)],
        out_specs=blk,
        out_shape=jax.ShapeDtypeStruct((B, N2, N1 * ch), F32),
        compiler_params=_cparams("parallel", "parallel"),
        name="fft_stage2",
    )(ar.reshape(B, N2, N1 * ch), ai.reshape(B, N2, N1 * ch), f(c2), f(s2), f(cc), f(sc))
    return out.reshape(B, T, ch)


HEAD_SLAB = LANES
ROPE_SHIFT = HEAD_SLAB - QK_ROPE
ONES_ROWS = 16
V_ROWS = V_HEAD + ONES_ROWS


def _rms(x, w):
    return x * lax.rsqrt(jnp.mean(x * x, axis=-1, keepdims=True) + RMS_EPS) * w


def _qproj_kernel(cq_ref, nw_ref, w_ref, ct_ref, st_ref, q_o):
    q = _dot(_rms(cq_ref[0], nw_ref[...]).astype(BF16), w_ref[...])
    ct = ct_ref[...]
    st = st_ref[...]
    for h in range(MLA_HEADS):
        sl = slice(h * HEAD_SLAB, (h + 1) * HEAD_SLAB)
        s = q[:, sl]
        q_o[0, :, sl] = ((s * ct + pltpu.roll(s, ROPE_SHIFT, 1) * st) * Q_SCALE).astype(BF16)


def _kvproj_kernel(ckv_ref, nw_ref, wk_ref, wvt_ref, ct_ref, st_ref, k_o, vt_o):
    x = ckv_ref[0]
    n = _rms(x[:, :KV_RANK], nw_ref[...]).astype(BF16)
    rs = x[:, KV_RANK:KV_RANK + HEAD_SLAB]
    rope = rs * ct_ref[...] + pltpu.roll(rs, ROPE_SHIFT, 1) * st_ref[...]
    kn = _dot(n, wk_ref[...])
    for h in range(MLA_HEADS):
        sl = slice(h * HEAD_SLAB, (h + 1) * HEAD_SLAB)
        k_o[0, :, sl] = (kn[:, sl] + rope).astype(BF16)
    vt = lax.dot_general(wvt_ref[...], n, NT, preferred_element_type=F32)
    row = lax.broadcasted_iota(jnp.int32, vt.shape, 0)
    vt_o[0] = jnp.where(row % V_ROWS >= V_HEAD, 1.0, vt).astype(BF16)


def q_projection(cq, nw, w, ct, st):
    B, T, R = cq.shape
    tm = _tile(T, 256)
    W = MLA_HEADS * HEAD_SLAB
    tab = pl.BlockSpec((tm, HEAD_SLAB), lambda b, i: (i, 0))
    return pl.pallas_call(
        _qproj_kernel,
        grid=(B, T // tm),
        in_specs=[pl.BlockSpec((1, tm, R), lambda b, i: (b, i, 0)),
                  pl.BlockSpec(nw.shape, lambda b, i: (0, 0)),
                  pl.BlockSpec(w.shape, lambda b, i: (0, 0)), tab, tab],
        out_specs=pl.BlockSpec((1, tm, W), lambda b, i: (b, i, 0)),
        out_shape=jax.ShapeDtypeStruct((B, T, W), BF16),
        compiler_params=_cparams("parallel", "parallel"),
        name="q_projection",
    )(cq, nw, w, ct, st)


def kv_projection(ckv, nw, wk, wvt, ct, st):
    B, T, R = ckv.shape
    tm = _tile(T, 256)
    W = MLA_HEADS * HEAD_SLAB
    tab = pl.BlockSpec((tm, HEAD_SLAB), lambda b, i: (i, 0))
    return pl.pallas_call(
        _kvproj_kernel,
        grid=(B, T // tm),
        in_specs=[pl.BlockSpec((1, tm, R), lambda b, i: (b, i, 0)),
                  pl.BlockSpec(nw.shape, lambda b, i: (0, 0)),
                  pl.BlockSpec(wk.shape, lambda b, i: (0, 0)),
                  pl.BlockSpec(wvt.shape, lambda b, i: (0, 0)), tab, tab],
        out_specs=[pl.BlockSpec((1, tm, W), lambda b, i: (b, i, 0)),
                   pl.BlockSpec((1, MLA_HEADS * V_ROWS, tm), lambda b, i: (b, 0, i))],
        out_shape=[jax.ShapeDtypeStruct((B, T, W), BF16), jax.ShapeDtypeStruct((B, MLA_HEADS * V_ROWS, T), BF16)],
        compiler_params=_cparams("parallel", "parallel"),
        name="kv_projection",
    )(ckv, nw, wk, wvt, ct, st)


def _attn_kernel(q_ref, k_ref, vt_ref, o_ref, m_scr, acc_scr, sa_scr, sb_scr, ma_scr, mb_scr, *, tkc, nkc):
    m_scr[...] = jnp.full(m_scr.shape, -jnp.inf, F32)
    acc_scr[...] = jnp.zeros(acc_scr.shape, F32)
    buf_a = (sa_scr, ma_scr)
    buf_b = (sb_scr, mb_scr)

    def chunk(c):
        return pl.ds(pl.multiple_of(c * tkc, tkc), tkc)

    def scores(c, dst, dmax):
        for hh in range(2):
            q = q_ref[0, :, hh * HEAD_SLAB:(hh + 1) * HEAD_SLAB]
            kc = k_ref[0, chunk(c), hh * HEAD_SLAB:(hh + 1) * HEAD_SLAB]
            s = lax.dot_general(kc, q, NT, preferred_element_type=F32)
            dst[hh] = s
            dmax[hh] = jnp.max(s, axis=0, keepdims=True)

    def consume(src, smax, c):
        for hh in range(2):
            m_old = m_scr[hh]
            m_new = jnp.maximum(m_old, smax[hh])
            pr = jnp.exp2(src[hh] - m_new).astype(BF16)
            alpha = jnp.exp2(m_old - m_new)
            m_scr[hh] = m_new
            rows = slice(hh * V_ROWS, (hh + 1) * V_ROWS)
            acc_scr[rows, :] = alpha * acc_scr[rows, :] + _dot(vt_ref[0, rows, chunk(c)], pr)

    scores(0, *buf_a)
    unroll = 2 if nkc > 4 else 1
    ngroups = (nkc - 1) // (2 * unroll)

    def body(i, carry):
        c = 2 * unroll * i
        for _ in range(unroll):
            scores(c + 1, *buf_b)
            consume(*buf_a, c)
            scores(c + 2, *buf_a)
            consume(*buf_b, c + 1)
            c = c + 2
        return carry

    lax.fori_loop(0, ngroups, body, 0)
    done = 2 * unroll * ngroups
    for cc in range(done, nkc):
        src, dst = (buf_a, buf_b) if (cc - done) % 2 == 0 else (buf_b, buf_a)
        if cc + 1 < nkc:
            scores(cc + 1, *dst)
        consume(*src, cc)
    outs = []
    for hh in range(2):
        base = hh * V_ROWS
        outs.append(acc_scr[base:base + V_HEAD, :] / acc_scr[base + V_HEAD:base + V_HEAD + 1, :])
    o_ref[0] = jnp.concatenate(outs, axis=0).T.astype(o_ref.dtype)


def attention(q, k, vt):
    B, T, _ = q.shape
    Tk = k.shape[1]
    tq = _tile(T, 256)
    tkc = next(c for c in (640, 512, 256, 128) if Tk % c == 0)
    hp = MLA_HEADS // 2
    return pl.pallas_call(
        functools.partial(_attn_kernel, tkc=tkc, nkc=Tk // tkc),
        grid=(B, hp, T // tq),
        in_specs=[pl.BlockSpec((1, tq, 2 * HEAD_SLAB), lambda b, h, i: (b, i, h)),
                  pl.BlockSpec((1, Tk, 2 * HEAD_SLAB), lambda b, h, i: (b, 0, h)),
                  pl.BlockSpec((1, 2 * V_ROWS, Tk), lambda b, h, i: (b, h, 0))],
        out_specs=pl.BlockSpec((1, tq, 2 * V_HEAD), lambda b, h, i: (b, i, h)),
        out_shape=jax.ShapeDtypeStruct((B, T, MLA_WIDTH), BF16),
        scratch_shapes=[pltpu.VMEM((2, 1, tq), F32), pltpu.VMEM((2 * V_ROWS, tq), F32),
                        pltpu.VMEM((2, tkc, tq), F32), pltpu.VMEM((2, tkc, tq), F32),
                        pltpu.VMEM((2, 1, tq), F32), pltpu.VMEM((2, 1, tq), F32)],
        compiler_params=_cparams("parallel", "parallel", "arbitrary"),
        name="mla_attention",
    )(q, k, vt)


def _outproj_kernel(rw_ref, ft_ref, att_ref, x_ref, g_ref, lw_ref, lb_ref, w_ref, o_ref, *, alpha):
    w = w_ref
    mix = (_dot(rw_ref[0].astype(BF16), w[0:RW_WIDTH, :])
           + _dot(ft_ref[0].astype(BF16), w[RW_WIDTH:RW_WIDTH + FT_WIDTH, :])
           + _dot(att_ref[0], w[RW_WIDTH + FT_WIDTH:, :]))
    z = alpha * x_ref[0] + g_ref[0] * mix
    o_ref[0] = _standardize(z, LN_EPS) * lw_ref[...] + lb_ref[...]


def out_projection_ln(rw, ft, att, x, gate, ln_w, ln_b, w_out, alpha):
    B, T, D = x.shape
    tm = _tile(T, 256)
    tok = lambda n: pl.BlockSpec((1, tm, n), lambda b, i: (b, i, 0))
    row = pl.BlockSpec((1, D), lambda b, i: (0, 0))
    return pl.pallas_call(
        functools.partial(_outproj_kernel, alpha=alpha),
        grid=(B, T // tm),
        in_specs=[tok(RW_WIDTH), tok(FT_WIDTH), tok(MLA_WIDTH), tok(D),
                  pl.BlockSpec((1, 1, D), lambda b, i: (b, 0, 0)), row, row,
                  pl.BlockSpec(w_out.shape, lambda b, i: (0, 0))],
        out_specs=tok(D),
        out_shape=jax.ShapeDtypeStruct((B, T, D), F32),
        compiler_params=_cparams("parallel", "parallel"),
        name="out_projection_ln",
    )(rw, ft, att, x, gate, ln_w, ln_b, w_out)


def _ffn_kernel(x_ref, sh_ref, sc_ref, g_ref, lw_ref, lb_ref, w1_ref, w3_ref, w2_ref, o_ref, h_scr, acc_scr, *, alpha):
    f = pl.program_id(2)

    @pl.when(f == 0)
    def _():
        h = _standardize(x_ref[0], MOD_EPS) * (1.0 + sc_ref[0]) + sh_ref[0]
        h_scr[...] = h.astype(BF16)
        acc_scr[...] = jnp.zeros(acc_scr.shape, F32)

    hb = h_scr[...]
    a = _dot(hb, w1_ref[...])
    b = _dot(hb, w3_ref[...])
    acc_scr[...] += _dot((a * jax.nn.sigmoid(a) * b).astype(BF16), w2_ref[...])

    @pl.when(f == pl.num_programs(2) - 1)
    def _():
        z = alpha * x_ref[0] + g_ref[0] * acc_scr[...]
        o_ref[0] = _standardize(z, LN_EPS) * lw_ref[...] + lb_ref[...]


def ffn_ln(x, shift, scale, gate, ln_w, ln_b, w1, w3, w2, alpha):
    B, T, D = x.shape
    F = w1.shape[1]
    tm = _tile(T, 512)
    tf = _tile(F, 512)
    tok = pl.BlockSpec((1, tm, D), lambda b, i, f: (b, i, 0))
    vec = pl.BlockSpec((1, 1, D), lambda b, i, f: (b, 0, 0))
    row = pl.BlockSpec((1, D), lambda b, i, f: (0, 0))
    return pl.pallas_call(
        functools.partial(_ffn_kernel, alpha=alpha),
        grid=(B, T // tm, F // tf),
        in_specs=[tok, vec, vec, vec, row, row,
                  pl.BlockSpec((D, tf), lambda b, i, f: (0, f)),
                  pl.BlockSpec((D, tf), lambda b, i, f: (0, f)),
                  pl.BlockSpec((tf, D), lambda b, i, f: (f, 0))],
        out_specs=tok,
        out_shape=jax.ShapeDtypeStruct((B, T, D), F32),
        scratch_shapes=[pltpu.VMEM((tm, D), BF16), pltpu.VMEM((tm, D), F32)],
        compiler_params=_cparams("parallel", "parallel", "arbitrary"),
        name="ffn_ln",
    )(x, shift, scale, gate, ln_w, ln_b, w1, w3, w2)


def _router_kernel(x_ref, sh_ref, sc_ref, wr_ref, h_o, ti_o, tg_o):
    h = _standardize(x_ref[0], MOD_EPS) * (1.0 + sc_ref[0]) + sh_ref[0]
    h_o[0] = h.astype(BF16)
    logits = _dot(h, wr_ref[...], HI)
    lane = lax.broadcasted_iota(jnp.int32, logits.shape, 1)
    neg = jnp.float32(-jnp.inf)
    logits = jnp.where(lane < N_EXPERTS, logits, neg)
    m1 = jnp.max(logits, axis=-1, keepdims=True)
    i1 = jnp.min(jnp.where(logits == m1, lane, LANES), axis=-1, keepdims=True)
    rest = jnp.where(lane == i1, neg, logits)
    m2 = jnp.max(rest, axis=-1, keepdims=True)
    i2 = jnp.min(jnp.where(rest == m2, lane, LANES), axis=-1, keepdims=True)
    e = jnp.exp(m2 - m1)
    g1 = 1.0 / (1.0 + e)
    g2 = e / (1.0 + e)
    ti_o[0] = jnp.where(lane == 0, i1, jnp.where(lane == 1, i2, 0))
    tg_o[0] = jnp.where(lane == 0, g1, jnp.where(lane == 1, g2, 0.0))


def moe_router(x, shift, scale, wr):
    B, T, D = x.shape
    tm = _tile(T, 512)
    tok = lambda n: pl.BlockSpec((1, tm, n), lambda b, i: (b, i, 0))
    vec = pl.BlockSpec((1, 1, D), lambda b, i: (b, 0, 0))
    return pl.pallas_call(
        _router_kernel,
        grid=(B, T // tm),
        in_specs=[tok(D), vec, vec, pl.BlockSpec(wr.shape, lambda b, i: (0, 0))],
        out_specs=[tok(D), tok(LANES), tok(LANES)],
        out_shape=[jax.ShapeDtypeStruct((B, T, D), BF16), jax.ShapeDtypeStruct((B, T, LANES), jnp.int32),
                   jax.ShapeDtypeStruct((B, T, LANES), F32)],
        compiler_params=_cparams("parallel", "parallel"),
        name="moe_router",
    )(x, shift, scale, wr)


def _expert_kernel(be_ref, bv_ref, x_ref, w1_ref, w3_ref, w2_ref, o_ref, acc_scr):
    blk = pl.program_id(0)
    f = pl.program_id(1)
    valid = bv_ref[blk] > 0

    @pl.when(f == 0)
    def _():
        acc_scr[...] = jnp.zeros(acc_scr.shape, F32)

    @pl.when(valid)
    def _():
        xb = x_ref[...]
        a = _dot(xb, w1_ref[0].astype(BF16))
        b = _dot(xb, w3_ref[0].astype(BF16))
        acc_scr[...] += _dot((a * jax.nn.sigmoid(a) * b).astype(BF16), w2_ref[0].astype(BF16))

    @pl.when(f == pl.num_programs(1) - 1)
    def _():
        o_ref[...] = acc_scr[...]


def expert_ffn(xg, blk_expert, blk_valid, w1, w3, w2):
    P, D = xg.shape
    F = w1.shape[2]
    tf = _tile(F, 256)
    nblk = P // MOE_ROWS
    grid_spec = pltpu.PrefetchScalarGridSpec(
        num_scalar_prefetch=2,
        grid=(nblk, F // tf),
        in_specs=[pl.BlockSpec((MOE_ROWS, D), lambda i, f, be, bv: (i, 0)),
                  pl.BlockSpec((1, D, tf), lambda i, f, be, bv: (be[i], 0, jnp.where(bv[i] > 0, f, 0))),
                  pl.BlockSpec((1, D, tf), lambda i, f, be, bv: (be[i], 0, jnp.where(bv[i] > 0, f, 0))),
                  pl.BlockSpec((1, tf, D), lambda i, f, be, bv: (be[i], jnp.where(bv[i] > 0, f, 0), 0))],
        out_specs=pl.BlockSpec((MOE_ROWS, D), lambda i, f, be, bv: (i, 0)),
        scratch_shapes=[pltpu.VMEM((MOE_ROWS, D), F32)],
    )
    return pl.pallas_call(
        _expert_kernel,
        grid_spec=grid_spec,
        out_shape=jax.ShapeDtypeStruct((P, D), F32),
        compiler_params=_cparams("parallel", "arbitrary"),
        name="expert_ffn",
    )(blk_expert, blk_valid, xg, w1, w3, w2)


def _combine_ln_kernel(x_ref, y0_ref, y1_ref, tg_ref, g_ref, lw_ref, lb_ref, o_ref, *, alpha):
    tg = tg_ref[0]
    y = tg[:, 0:1] * y0_ref[0] + tg[:, 1:2] * y1_ref[0]
    z = alpha * x_ref[0] + g_ref[0] * y
    o_ref[0] = _standardize(z, LN_EPS) * lw_ref[...] + lb_ref[...]


def combine_ln(x, y0, y1, tg, gate, ln_w, ln_b, alpha):
    B, T, D = x.shape
    tm = _tile(T, 512)
    tok = pl.BlockSpec((1, tm, D), lambda b, i: (b, i, 0))
    row = pl.BlockSpec((1, D), lambda b, i: (0, 0))
    return pl.pallas_call(
        functools.partial(_combine_ln_kernel, alpha=alpha),
        grid=(B, T // tm),
        in_specs=[tok, tok, tok, pl.BlockSpec((1, tm, LANES), lambda b, i: (b, i, 0)),
                  pl.BlockSpec((1, 1, D), lambda b, i: (b, 0, 0)), row, row],
        out_specs=tok,
        out_shape=jax.ShapeDtypeStruct((B, T, D), F32),
        compiler_params=_cparams("parallel", "parallel"),
        name="combine_ln",
    )(x, y0, y1, tg, gate, ln_w, ln_b)


def moe_ln(x, shift, scale, gate, ln_w, ln_b, router, w1, w3, w2, alpha):
    B, T, D = x.shape
    N = B * T
    E = router.shape[1]
    wr = jnp.zeros((D, LANES), F32).at[:, :E].set(router)
    h, ti, tg = moe_router(x, shift, scale, wr)
    top_i = ti.reshape(N, LANES)[:, :2]
    e_flat = top_i.reshape(-1)
    onehot = (e_flat[:, None] == jnp.arange(E, dtype=jnp.int32)[None, :]).astype(jnp.int32)
    rank = jnp.sum((jnp.cumsum(onehot, axis=0) - onehot) * onehot, axis=1)
    counts = jnp.sum(onehot, axis=0)
    padded = (counts + MOE_ROWS - 1) // MOE_ROWS * MOE_ROWS
    p_ends = jnp.cumsum(padded)
    p_starts = p_ends - padded
    dest = p_starts[e_flat] + rank
    P = -(-(2 * N) // MOE_ROWS) * MOE_ROWS + E * MOE_ROWS
    nblk = P // MOE_ROWS
    tok_flat = jnp.repeat(jnp.arange(N, dtype=jnp.int32), 2)
    slot_tok = jnp.zeros((P,), jnp.int32).at[dest].set(tok_flat)
    blk_start = jnp.arange(nblk, dtype=jnp.int32) * MOE_ROWS
    blk_expert = jnp.minimum(jnp.searchsorted(p_ends, blk_start, side="right"), E - 1).astype(jnp.int32)
    blk_valid = (blk_start < p_ends[-1]).astype(jnp.int32)
    xg = h.reshape(N, D)[slot_tok]
    yg = expert_ffn(xg, blk_expert, blk_valid, w1, w3, w2)
    d2 = dest.reshape(N, 2)
    y0 = yg[d2[:, 0]].reshape(B, T, D)
    y1 = yg[d2[:, 1]].reshape(B, T, D)
    return combine_ln(x, y0, y1, tg, gate, ln_w, ln_b, alpha)


def _rope_partner(w):
    half = ROPE_AXIS // 2
    idx = np.arange(QK_ROPE)
    first = (idx % ROPE_AXIS) < half
    src = np.where(first, idx + half, idx - half)
    sign = np.where(first, -1.0, 1.0).astype(np.float32)
    return w[:, src] * sign


def _layer_params(l, w_in, rw_conv, rw_w0, rw_w_up, rw_a0, rw_a_up, rw_g_up, rw_k_k, rw_k_a, rw_r_k, rw_gn_w,
                  rw_gn_b, mla_q_norm, mla_w_uq, mla_kv_norm, mla_w_ukv, w_out):
    D = w_in.shape[1]
    wi = w_in[l]
    o = np.cumsum([0, 3 * RW_WIDTH, G_RANK, LORA_RANK, LORA_RANK, LORA_RANK, LORA_RANK, FT_WIDTH, Q_RANK, KV_RANK, QK_ROPE])
    piece = lambda i: wi[:, o[i]:o[i + 1]]
    zpad = lambda n: jnp.zeros((D, n), F32)
    lora = [jnp.concatenate([piece(i), zpad(LANES - LORA_RANK)], axis=1) for i in (2, 3, 4, 5)]
    w_rw = jnp.concatenate([piece(0), piece(1)] + lora, axis=1)
    kr = piece(9)
    w_ckv = jnp.concatenate([piece(8), zpad(QK_NOPE), kr, _rope_partner(kr)], axis=1)
    pad_rows = lambda a: jnp.concatenate([a, jnp.zeros((a.shape[0], LANES - LORA_RANK, a.shape[2]), F32)], axis=1)
    head = jnp.arange(RW_WIDTH) // RW_HEAD
    uq = mla_w_uq[l].reshape(Q_RANK, MLA_HEADS, QK_NOPE + QK_ROPE)
    uq_rope = uq[:, :, QK_NOPE:]
    uq_partner = _rope_partner(uq_rope.reshape(Q_RANK * MLA_HEADS, QK_ROPE)).reshape(Q_RANK, MLA_HEADS, QK_ROPE)
    w_q = jnp.concatenate([uq, uq_partner], axis=2).reshape(Q_RANK, MLA_HEADS * HEAD_SLAB)
    ukv = mla_w_ukv[l].reshape(KV_RANK, MLA_HEADS, QK_NOPE + V_HEAD)
    w_k = jnp.concatenate([ukv[:, :, :QK_NOPE], jnp.zeros((KV_RANK, MLA_HEADS, HEAD_SLAB - QK_NOPE), F32)],
                          axis=2).reshape(KV_RANK, MLA_HEADS * HEAD_SLAB)
    w_v = jnp.concatenate([ukv[:, :, QK_NOPE:], jnp.zeros((KV_RANK, MLA_HEADS, ONES_ROWS), F32)],
                          axis=2).reshape(KV_RANK, MLA_HEADS * V_ROWS)
    return {
        "w_in": [w.astype(BF16) for w in (w_rw, piece(6), piece(7), w_ckv)],
        "rw": {"conv": rw_conv[l], "k_k": rw_k_k[l][None], "k_a": rw_k_a[l][None], "r_k": rw_r_k[l].reshape(1, RW_WIDTH),
               "w0": rw_w0[l], "a0": rw_a0[l], "w_up": pad_rows(rw_w_up[l]).astype(BF16), "a_up": pad_rows(rw_a_up[l]).astype(BF16),
               "g_up": rw_g_up[l].astype(BF16), "gn_w": rw_gn_w[l][None], "gn_b": rw_gn_b[l][None],
               "E": (head[:, None] == head[None, :]).astype(BF16)},
        "q_norm": mla_q_norm[l][None], "w_q": w_q.astype(BF16),
        "kv_norm": mla_kv_norm[l][None], "w_k": w_k.astype(BF16), "w_vt": w_v.T.astype(BF16),
        "w_out": w_out[l].astype(BF16),
    }


def _rope_tables(T, use_rope):
    ones = jnp.ones((T, QK_NOPE), F32)
    zeros = jnp.zeros((T, QK_NOPE), F32)
    zpad = jnp.zeros((T, HEAD_SLAB - QK_NOPE - QK_ROPE), F32)
    if use_rope:
        row = jnp.repeat(jnp.arange(T // GRID_W), GRID_W).astype(F32)
        col = (jnp.arange(T) % GRID_W).astype(F32)
        inv = ROPE_THETA ** (-jnp.arange(0, ROPE_AXIS, 2, dtype=F32) / ROPE_AXIS)
        ang = jnp.stack([row[:, None] * inv, col[:, None] * inv], axis=1)
        ang = jnp.broadcast_to(ang[:, :, None, :], (T, 2, 2, ROPE_AXIS // 2)).reshape(T, QK_ROPE)
        cos, sin = jnp.cos(ang), jnp.sin(ang)
    else:
        cos, sin = jnp.ones((T, QK_ROPE), F32), jnp.zeros((T, QK_ROPE), F32)
    return jnp.concatenate([ones, cos, zpad], axis=1), jnp.concatenate([zeros, sin, zpad], axis=1)


def _mixer(h_pieces_lat, h_pieces_ctx, p, tabs_lat, tabs_ctx, emit_ctx):
    rw_l, ft_l, cq_l, ckv_l = h_pieces_lat
    rw_c, ft_c, cq_c, ckv_c = h_pieces_ctx
    rwo_l, rwo_c = rwkv_branch(rw_l, rw_c, p["rw"], emit_ctx)
    fto_l = fourier_mixer(ft_l)
    q_l = q_projection(cq_l, p["q_norm"], p["w_q"], *tabs_lat)
    k_l, vt_l = kv_projection(ckv_l, p["kv_norm"], p["w_k"], p["w_vt"], *tabs_lat)
    k_c, vt_c = kv_projection(ckv_c, p["kv_norm"], p["w_k"], p["w_vt"], *tabs_ctx)
    att_l = attention(q_l, jnp.concatenate([k_l, k_c], axis=1), jnp.concatenate([vt_l, vt_c], axis=2))
    out_c = None
    if emit_ctx:
        fto_c = fourier_mixer(ft_c)
        q_c = q_projection(cq_c, p["q_norm"], p["w_q"], *tabs_ctx)
        att_c = attention(q_c, k_c, vt_c)
        out_c = (rwo_c, fto_c, att_c)
    return (rwo_l, fto_l, att_l), out_c


def kernel(x, c, ctx, c_ctx, ada_w, ada_b, w_in, rw_conv, rw_w0, rw_w_up, rw_a0, rw_a_up, rw_g_up, rw_k_k, rw_k_a,
           rw_r_k, rw_gn_w, rw_gn_b, mla_q_norm, mla_w_uq, mla_kv_norm, mla_w_ukv, w_out, ln1_w, ln1_b, ln2_w, ln2_b,
           ffn_w1, ffn_w3, ffn_w2, moe_router, moe_w1, moe_w3, moe_w2):
    B, T, D = x.shape
    Tc = ctx.shape[1]
    depth = w_in.shape[0]
    alpha = (2 * depth) ** 0.25
    assert B + 1 <= SUBLANES
    cc = jnp.zeros((SUBLANES, D), F32).at[:B].set(c).at[B].set(c_ctx)
    ada = ada_vectors(cc, ada_w, ada_b)
    tabs_lat = _rope_tables(T, True)
    tabs_ctx = _rope_tables(Tc, False)
    for l in range(depth):
        last = l == depth - 1
        p = _layer_params(l, w_in, rw_conv, rw_w0, rw_w_up, rw_a0, rw_a_up, rw_g_up, rw_k_k, rw_k_a, rw_r_k,
                          rw_gn_w, rw_gn_b, mla_q_norm, mla_w_uq, mla_kv_norm, mla_w_ukv, w_out)
        mods = ada[l].reshape(SUBLANES, 6, D)
        lat = [mods[:B, j][:, None, :] for j in range(6)]
        cx = [jnp.broadcast_to(mods[B, j][None, None, :], (B, 1, D)) for j in range(6)]
        sh_m, sc_m, g_m, sh_f, sc_f, g_f = lat
        csh_m, csc_m, cg_m, csh_f, csc_f, cg_f = cx
        ln1 = (ln1_w[l][None], ln1_b[l][None])
        ln2 = (ln2_w[l][None], ln2_b[l][None])

        pieces_l = in_projection(x, sh_m, sc_m, p["w_in"])
        pieces_c = in_projection(ctx, csh_m, csc_m, p["w_in"])
        mix_l, mix_c = _mixer(pieces_l, pieces_c, p, tabs_lat, tabs_ctx, not last)
        x = out_projection_ln(*mix_l, x, g_m, *ln1, p["w_out"], alpha)
        i = l // 2
        if l % 2 == 0:
            dense = (ffn_w1[i].astype(BF16), ffn_w3[i].astype(BF16), ffn_w2[i].astype(BF16))
            x = ffn_ln(x, sh_f, sc_f, g_f, *ln2, *dense, alpha)
        else:
            x = moe_ln(x, sh_f, sc_f, g_f, *ln2, moe_router[i], moe_w1[i], moe_w3[i], moe_w2[i], alpha)
        if not last:
            ctx = out_projection_ln(*mix_c, ctx, cg_m, *ln1, p["w_out"], alpha)
            if l % 2 == 0:
                ctx = ffn_ln(ctx, csh_f, csc_f, cg_f, *ln2, *dense, alpha)
            else:
                ctx = moe_ln(ctx, csh_f, csc_f, cg_f, *ln2, moe_router[i], moe_w1[i], moe_w3[i], moe_w2[i], alpha)
    return x
```

```python
import functools
import math

import numpy as np
import jax
import jax.numpy as jnp
from jax import lax
from jax.experimental import pallas as pl
from jax.experimental.pallas import tpu as pltpu

F32 = jnp.float32
BF16 = jnp.bfloat16
HI = lax.Precision.HIGHEST

LANES = 128
SUBLANES = 8
VMEM_LIMIT = 56 * 1024 * 1024

GRID_W = 64
RW_HEADS = 8
RW_HEAD = 64
RW_WIDTH = RW_HEADS * RW_HEAD
G_RANK = 128
LORA_RANK = 64
GN_EPS = 64e-5
FT_GROUP = 64
FT_WIDTH = 512
MLA_HEADS = 16
QK_NOPE = 64
QK_ROPE = 32
V_HEAD = 64
MLA_WIDTH = MLA_HEADS * V_HEAD
Q_RANK = 512
KV_RANK = 256
ROPE_AXIS = QK_ROPE // 2
ROPE_THETA = 10000.0
ATTN_SCALE = (QK_NOPE + QK_ROPE) ** -0.5
Q_SCALE = ATTN_SCALE * math.log2(math.e)
N_EXPERTS = 8
LN_EPS = 1e-5
MOD_EPS = 1e-6
RMS_EPS = 1e-6
CHUNK = 64
INV_BLOCK = 16
MOE_ROWS = 1024

NT = (((1,), (1,)), ((), ()))
TN = (((0,), (0,)), ((), ()))


def _cparams(*sem):
    return pltpu.CompilerParams(dimension_semantics=sem, vmem_limit_bytes=VMEM_LIMIT)


def _dot(a, b, prec=None):
    return jnp.dot(a, b, precision=prec, preferred_element_type=F32)


def _dot_split(x, w):
    hi = x.astype(BF16)
    r1 = x - hi.astype(F32)
    mid = r1.astype(BF16)
    lo = (r1 - mid.astype(F32)).astype(BF16)
    return _dot(hi, w) + _dot(mid, w) + _dot(lo, w)


def _dot3(a, b):
    a_hi = a.astype(BF16)
    b_hi = b.astype(BF16)
    a_lo = (a - a_hi.astype(F32)).astype(BF16)
    b_lo = (b - b_hi.astype(F32)).astype(BF16)
    return _dot(a_hi, b_hi) + _dot(a_hi, b_lo) + _dot(a_lo, b_hi)


def _standardize(x, eps):
    mu = jnp.mean(x, axis=-1, keepdims=True)
    xc = x - mu
    var = jnp.mean(xc * xc, axis=-1, keepdims=True)
    return xc * lax.rsqrt(var + eps)


def _tile(n, pref):
    t = min(n, pref)
    assert n % t == 0, (n, pref)
    return t


def _ada_kernel(c_ref, w_ref, b_ref, o_ref):
    c = c_ref[...]
    s = c * jax.nn.sigmoid(c)
    o_ref[0] = _dot(s, w_ref[0], HI) + b_ref[0]


def ada_vectors(cc, ada_w, ada_b):
    L, D, N6 = ada_w.shape
    tn = _tile(N6, 1024)
    return pl.pallas_call(
        _ada_kernel,
        grid=(L, N6 // tn),
        in_specs=[pl.BlockSpec((SUBLANES, D), lambda l, j: (0, 0)),
                  pl.BlockSpec((1, D, tn), lambda l, j: (l, 0, j)),
                  pl.BlockSpec((1, 1, tn), lambda l, j: (l, 0, j))],
        out_specs=pl.BlockSpec((1, SUBLANES, tn), lambda l, j: (l, 0, j)),
        out_shape=jax.ShapeDtypeStruct((L, SUBLANES, N6), F32),
        compiler_params=_cparams("parallel", "parallel"),
        name="ada_vectors",
    )(cc, ada_w, ada_b.reshape(L, 1, N6))


def _inproj_kernel(x_ref, sh_ref, sc_ref, *refs):
    nw = len(refs) // 2
    h = _standardize(x_ref[0], MOD_EPS) * (1.0 + sc_ref[0]) + sh_ref[0]
    hb = h.astype(BF16)
    for w_ref, o_ref in zip(refs[:nw], refs[nw:]):
        o_ref[0] = _dot(hb, w_ref[...])


def in_projection(x, shift, scale, weights):
    B, T, D = x.shape
    tm = _tile(T, 256)
    vec = pl.BlockSpec((1, 1, D), lambda b, i: (b, 0, 0))
    return pl.pallas_call(
        _inproj_kernel,
        grid=(B, T // tm),
        in_specs=[pl.BlockSpec((1, tm, D), lambda b, i: (b, i, 0)), vec, vec]
        + [pl.BlockSpec(w.shape, lambda b, i: (0, 0)) for w in weights],
        out_specs=[pl.BlockSpec((1, tm, w.shape[1]), lambda b, i: (b, i, 0)) for w in weights],
        out_shape=[jax.ShapeDtypeStruct((B, T, w.shape[1]), F32) for w in weights],
        compiler_params=_cparams("parallel", "parallel"),
        name="in_projection",
    )(x, shift, scale, *weights)


RW_COLS = 3 * RW_WIDTH + G_RANK + 4 * LANES


def _softplus(z):
    return jnp.maximum(z, 0.0) + jnp.log(1.0 + jnp.exp(-jnp.abs(z)))


def _rwkv_prep_kernel(x_ref, xp_ref, xn_ref, conv_ref, kk_ref, ka_ref, rk_ref, w0_ref, a0_ref,
                      wup_ref, aup_ref, gup_ref, e_ref,
                      r_o, v_o, kk_o, ld0_o, b0_o, kr0_o, ld1_o, b1_o, kr1_o, bonus_o, gate_o):
    i = pl.program_id(1)
    n = pl.num_programs(1)
    x = x_ref[0]
    W3 = 3 * RW_WIDTH
    raw = x[:, :W3]
    tm = raw.shape[0]
    row = lax.broadcasted_iota(jnp.int32, (tm, 1), 0)
    prev_row = jnp.where(i > 0, xp_ref[0, SUBLANES - 1:SUBLANES, :], 0.0)
    next_row = jnp.where(i < n - 1, xn_ref[0, 0:1, :], 0.0)
    xm = jnp.where(row == 0, prev_row, pltpu.roll(raw, 1, 0))
    xq = jnp.where(row == tm - 1, next_row, pltpu.roll(raw, tm - 1, 0))
    cw = conv_ref[...]
    y = xm * cw[0:1] + raw * cw[1:2] + xq * cw[2:3]
    r = y[:, :RW_WIDTH]
    k = y[:, RW_WIDTH:2 * RW_WIDTH]
    v = y[:, 2 * RW_WIDTH:W3]
    E = e_ref[...]
    kkv = k * kk_ref[...]
    kk = kkv / jnp.maximum(jnp.sqrt(_dot_split(kkv * kkv, E)), 1e-12)
    r_o[0] = r
    v_o[0] = v
    kk_o[0] = kk
    g_dn = x[:, W3:W3 + G_RANK]
    gate_o[0] = _dot(jax.nn.sigmoid(g_dn).astype(BF16), gup_ref[...])
    kr_sum = jnp.zeros_like(r)
    outs = ((ld0_o, b0_o, kr0_o), (ld1_o, b1_o, kr1_o))
    for d in range(2):
        base = W3 + G_RANK
        w_dn = x[:, base + d * LANES: base + (d + 1) * LANES]
        a_dn = x[:, base + (2 + d) * LANES: base + (3 + d) * LANES]
        z = w0_ref[d:d + 1, :] + _dot(jnp.tanh(w_dn).astype(BF16), wup_ref[d])
        logw = -_softplus(-z) - 0.5
        a = jax.nn.sigmoid(a0_ref[d:d + 1, :] + _dot(a_dn.astype(BF16), aup_ref[d]))
        kr = k * (1.0 + (a - 1.0) * ka_ref[...])
        ld_o, b_o, kr_o = outs[d]
        ld_o[0] = -jnp.exp(logw)
        b_o[0] = a * kk
        kr_o[0] = kr
        kr_sum = kr_sum + kr
    bonus_o[0] = _dot_split(r * kr_sum * rk_ref[...], E) * v


def rwkv_prep(rw, p):
    B, T, _ = rw.shape
    tm = _tile(T, 256)
    nh = tm // SUBLANES
    last = T // SUBLANES - 1
    W3 = 3 * RW_WIDTH
    full = lambda a: pl.BlockSpec(a.shape, lambda b, i: (0,) * a.ndim)
    params = [p["conv"], p["k_k"], p["k_a"], p["r_k"], p["w0"], p["a0"], p["w_up"], p["a_up"], p["g_up"], p["E"]]
    outs = pl.pallas_call(
        _rwkv_prep_kernel,
        grid=(B, T // tm),
        in_specs=[pl.BlockSpec((1, tm, RW_COLS), lambda b, i: (b, i, 0)),
                  pl.BlockSpec((1, SUBLANES, W3), lambda b, i: (b, jnp.maximum(i * nh - 1, 0), 0)),
                  pl.BlockSpec((1, SUBLANES, W3), lambda b, i: (b, jnp.minimum((i + 1) * nh, last), 0))]
        + [full(a) for a in params],
        out_specs=[pl.BlockSpec((1, tm, RW_WIDTH), lambda b, i: (b, i, 0))] * 11,
        out_shape=[jax.ShapeDtypeStruct((B, T, RW_WIDTH), F32)] * 11,
        compiler_params=_cparams("parallel", "parallel"),
        name="rwkv_prep",
    )(rw, rw, rw, *params)
    return outs


PAIR = 2 * RW_HEAD
N_PAIRS = RW_HEADS // 2
CHUNK_GROUP = 4


def _pair_masks(reverse):
    i = lax.broadcasted_iota(jnp.int32, (PAIR, PAIR), 0)
    j = lax.broadcasted_iota(jnp.int32, (PAIR, PAIR), 1)
    same = (i // CHUNK) == (j // CHUNK)
    strict = same & ((j > i) if reverse else (j < i))
    incl = same & ((j >= i) if reverse else (j <= i))
    blk = (i // INV_BLOCK) == (j // INV_BLOCK)
    eye = jnp.where(i == j, 1.0, 0.0).astype(F32)
    t = lax.broadcasted_iota(jnp.int32, (CHUNK, CHUNK), 0)
    u = lax.broadcasted_iota(jnp.int32, (CHUNK, CHUNK), 1)
    tri = jnp.where((u >= t) if reverse else (u <= t), 1.0, 0.0).astype(F32)
    first = lax.broadcasted_iota(jnp.int32, (1, PAIR), 1) < RW_HEAD
    return strict, incl, blk, eye, tri, first


def _pair_chunk_math(ld, r, v, kk, b, kr, masks):
    strict, incl, blk, eye, tri, first = masks
    P = ld.shape[0]
    ein = lambda spec, a, c, prec=None: jnp.einsum(spec, a, c, precision=prec, preferred_element_type=F32)
    tri_b = jnp.broadcast_to(tri.astype(BF16), (P, CHUNK, CHUNK))
    ld_hi = ld.astype(BF16)
    ld_r = ld - ld_hi.astype(F32)
    ld_mid = ld_r.astype(BF16)
    ld_lo = (ld_r - ld_mid.astype(F32)).astype(BF16)
    Lc = ein("pct,ptk->pck", tri_b, ld_hi) + ein("pct,ptk->pck", tri_b, ld_mid) + ein("pct,ptk->pck", tri_b, ld_lo)
    Lx = Lc - ld
    Lt = jnp.sum(ld, axis=1, keepdims=True)
    ginv = jnp.exp(-Lc)
    gout = jnp.exp(Lt - Lc)
    stack = lambda x: jnp.concatenate([jnp.where(first, x, 0.0), jnp.where(first, 0.0, x)], axis=1).astype(BF16)
    twice = lambda x: jnp.concatenate([x, x], axis=1).astype(BF16)
    Xk = stack(kk * jnp.exp(Lx))
    Xr = stack(r * jnp.exp(Lc))
    Vs = stack(v)
    Bs = stack(b * gout)
    Ks = stack(kr * gout)
    mm = lambda a, c: ein("pij,pjk->pik", a.astype(BF16), c.astype(BF16))
    nt = lambda a, c: ein("pik,pjk->pij", a, c)
    tn = lambda a, c: ein("pji,pjk->pik", a, c.astype(BF16))
    XX = jnp.concatenate([Xk, Xr], axis=1)
    Mb = nt(XX, twice(b * ginv))
    Mk = nt(XX, twice(kr * ginv))
    Mab = jnp.where(strict, Mb[:, :PAIR], 0.0)
    Arb = jnp.where(incl, Mb[:, PAIR:], 0.0)
    Mak = jnp.where(strict, Mk[:, :PAIR], 0.0)
    Ark = jnp.where(incl, Mk[:, PAIR:], 0.0)
    Nd = jnp.where(blk, Mab, 0.0)
    No = Mab - Nd
    N2 = mm(Nd, Nd)
    N4 = mm(N2, N2)
    N8 = mm(N4, N4)
    Td = mm(mm(mm(eye - Nd, eye + N2), eye + N4), eye + N8)
    M2 = mm(Td, No)
    Tm = mm(mm(eye - M2, eye + mm(M2, M2)), Td)
    P1 = mm(Tm, Xk)
    P2 = mm(Tm, mm(Mak, Vs))
    Q1s = Xr.astype(F32) - mm(Arb, P1)
    Yis = mm(Ark, Vs) - mm(Arb, P2)
    G = eye * jnp.exp(Lt) - tn(Bs, P1)
    H = tn(Ks, Vs) - tn(Bs, P2)
    return Q1s[:, :CHUNK] + Q1s[:, CHUNK:], Yis[:, :CHUNK] + Yis[:, CHUNK:], G, H


def _rwkv_chunk_kernel(ld_ref, r_ref, v_ref, kk_ref, b_ref, kr_ref, q_o, yi_o, g_o, h_o, *, nch, reverse):
    masks = _pair_masks(reverse)
    group = min(CHUNK_GROUP, nch)

    def body(i, carry):
        c0 = i * group
        rows = [pl.ds(pl.multiple_of((c0 + u) * CHUNK, CHUNK), CHUNK) for u in range(group)]
        get = lambda ref: jnp.stack([ref[0, rows[u], j * PAIR:(j + 1) * PAIR]
                                     for u in range(group) for j in range(N_PAIRS)])
        Q1, Yi, G, H = _pair_chunk_math(get(ld_ref), get(r_ref), get(v_ref), get(kk_ref), get(b_ref), get(kr_ref),
                                        masks)
        for u in range(group):
            for j in range(N_PAIRS):
                q_o[0, rows[u], j * PAIR:(j + 1) * PAIR] = Q1[u * N_PAIRS + j]
                yi_o[0, rows[u], j * PAIR:(j + 1) * PAIR] = Yi[u * N_PAIRS + j]
        mats = pl.ds(c0 * N_PAIRS, group * N_PAIRS)
        g_o[0, mats] = G
        h_o[0, mats] = H
        return carry

    lax.fori_loop(0, nch // group, body, 0)


def rwkv_chunks(ld, r, v, kk, b, kr, reverse):
    B, T, W = ld.shape
    tt = _tile(T, 4 * CHUNK)
    nch = tt // CHUNK
    tok = pl.BlockSpec((1, tt, W), lambda bb, i: (bb, i, 0))
    mat = pl.BlockSpec((1, nch * N_PAIRS, PAIR, PAIR), lambda bb, i: (bb, i, 0, 0))
    mats = jax.ShapeDtypeStruct((B, T // CHUNK * N_PAIRS, PAIR, PAIR), F32)
    return pl.pallas_call(
        functools.partial(_rwkv_chunk_kernel, nch=nch, reverse=reverse),
        grid=(B, T // tt),
        in_specs=[tok] * 6,
        out_specs=[tok, tok, mat, mat],
        out_shape=[jax.ShapeDtypeStruct((B, T, W), F32)] * 2 + [mats, mats],
        compiler_params=_cparams("parallel", "parallel"),
        name="rwkv_chunks_bwd" if reverse else "rwkv_chunks_fwd",
    )(ld, r, v, kk, b, kr)


def _rwkv_seq_kernel(q_ref, yi_ref, g_ref, h_ref, s0_ref, y_o, sf_o, s_scr, *, nch, reverse):
    @pl.when(pl.program_id(1) == 0)
    def _():
        s_scr[...] = s0_ref[0]

    def body(cc, carry):
        c = (nch - 1 - cc) if reverse else cc
        rows = pl.ds(pl.multiple_of(c * CHUNK, CHUNK), CHUNK)
        for j in range(N_PAIRS):
            lanes = slice(j * PAIR, (j + 1) * PAIR)
            S = s_scr[j]
            y_o[0, rows, lanes] = _dot3(q_ref[0, rows, lanes], S) + yi_ref[0, rows, lanes]
            s_scr[j] = _dot3(g_ref[0, c * N_PAIRS + j], S) + h_ref[0, c * N_PAIRS + j]
        return carry

    lax.fori_loop(0, nch, body, 0)
    sf_o[0] = s_scr[...]


def rwkv_sequential(q1, yi, g, hm, s0, reverse):
    B, T, W = q1.shape
    tt = _tile(T, 8 * CHUNK)
    n = T // tt
    nch = tt // CHUNK
    step = (lambda i: n - 1 - i) if reverse else (lambda i: i)
    tok = pl.BlockSpec((1, tt, W), lambda bb, i: (bb, step(i), 0))
    mat = pl.BlockSpec((1, nch * N_PAIRS, PAIR, PAIR), lambda bb, i: (bb, step(i), 0, 0))
    sspec = pl.BlockSpec((1, N_PAIRS, PAIR, PAIR), lambda bb, i: (bb, 0, 0, 0))
    return pl.pallas_call(
        functools.partial(_rwkv_seq_kernel, nch=nch, reverse=reverse),
        grid=(B, n),
        in_specs=[tok, tok, mat, mat, sspec],
        out_specs=[tok, sspec],
        out_shape=[jax.ShapeDtypeStruct((B, T, W), F32), jax.ShapeDtypeStruct((B, N_PAIRS, PAIR, PAIR), F32)],
        scratch_shapes=[pltpu.VMEM((N_PAIRS, PAIR, PAIR), F32)],
        compiler_params=_cparams("parallel", "arbitrary"),
        name="rwkv_seq_bwd" if reverse else "rwkv_seq_fwd",
    )(q1, yi, g, hm, s0)


def _rwkv_readout_kernel(yf_ref, yb_ref, bonus_ref, gate_ref, gw_ref, gb_ref, e_ref, o_ref):
    E = e_ref[...]
    ys = yf_ref[0] + yb_ref[0]
    yc = ys - _dot_split(ys, E) * (1.0 / RW_HEAD)
    var = _dot_split(yc * yc, E) * (1.0 / RW_HEAD)
    yn = yc * lax.rsqrt(var + GN_EPS) * gw_ref[...] + gb_ref[...]
    o_ref[0] = (yn + bonus_ref[0]) * gate_ref[0]


def rwkv_readout(yf, yb, bonus, gate, p):
    B, T, W = yf.shape
    tm = _tile(T, 512)
    spec = pl.BlockSpec((1, tm, W), lambda b, i: (b, i, 0))
    full = lambda a: pl.BlockSpec(a.shape, lambda b, i: (0,) * a.ndim)
    return pl.pallas_call(
        _rwkv_readout_kernel,
        grid=(B, T // tm),
        in_specs=[spec] * 4 + [full(p["gn_w"]), full(p["gn_b"]), full(p["E"])],
        out_specs=spec,
        out_shape=jax.ShapeDtypeStruct((B, T, W), F32),
        compiler_params=_cparams("parallel", "parallel"),
        name="rwkv_readout",
    )(yf, yb, bonus, gate, p["gn_w"], p["gn_b"], p["E"])


def rwkv_branch(rw_lat, rw_ctx, p, emit_ctx):
    prep_l = rwkv_prep(rw_lat, p)
    prep_c = rwkv_prep(rw_ctx, p)
    B = rw_lat.shape[0]
    s_zero = jnp.zeros((B, N_PAIRS, PAIR, PAIR), F32)

    def scans(prep, s0s):
        r, v, kk = prep[:3]
        ys, finals = [], []
        for d in range(2):
            ld, b, kr = prep[3 + 3 * d: 6 + 3 * d]
            q1, yi, g, hm = rwkv_chunks(ld, r, v, kk, b, kr, reverse=bool(d))
            y, sf = rwkv_sequential(q1, yi, g, hm, s0s[d], reverse=bool(d))
            ys.append(y)
            finals.append(sf)
        return ys, finals

    ys_c, fin_c = scans(prep_c, (s_zero, s_zero))
    ys_l, _ = scans(prep_l, fin_c)
    out_l = rwkv_readout(ys_l[0], ys_l[1], prep_l[9], prep_l[10], p)
    out_c = rwkv_readout(ys_c[0], ys_c[1], prep_c[9], prep_c[10], p) if emit_ctx else None
    return out_l, out_c


def _dft_mats(n):
    a = 2.0 * np.pi * np.outer(np.arange(n), np.arange(n)) / n
    return np.cos(a), np.sin(a)


def _fft1_kernel(u_ref, c_ref, s_ref, twc_ref, tws_ref, ar_o, ai_o, *, tn2, ch):
    U = u_ref[0]
    Ar = _dot3(c_ref[...], U)
    Ai = -_dot3(s_ref[...], U)
    twc = twc_ref[0]
    tws = tws_ref[0]
    for j in range(tn2):
        ct = twc[:, j:j + 1]
        st = tws[:, j:j + 1]
        a_r = Ar[:, j * ch:(j + 1) * ch]
        a_i = Ai[:, j * ch:(j + 1) * ch]
        ar_o[0, j] = a_r * ct + a_i * st
        ai_o[0, j] = a_i * ct - a_r * st


def _fft2_kernel(ar_ref, ai_ref, c_ref, s_ref, cc_ref, sc_ref, o_ref):
    Ar = ar_ref[0]
    Ai = ai_ref[0]
    C = c_ref[...]
    S = s_ref[...]
    Yr = _dot3(C, Ar) + _dot3(S, Ai)
    Yi = _dot3(C, Ai) - _dot3(S, Ar)
    Cc = cc_ref[...]
    Sc = sc_ref[...]
    for m in range(Ar.shape[1] // LANES):
        sl = slice(m * LANES, (m + 1) * LANES)
        o_ref[0, :, sl] = _dot3(Yr[:, sl], Cc) + _dot3(Yi[:, sl], Sc)


def fourier_mixer(u):
    B, T, ch = u.shape
    lg = int(round(math.log2(T)))
    assert 1 << lg == T
    N1 = 1 << ((lg + 1) // 2)
    N2 = T // N1
    c1, s1 = _dft_mats(N1)
    c2, s2 = _dft_mats(N2)
    tw = 2.0 * np.pi * np.outer(np.arange(N1), np.arange(N2)) / T
    tn2 = min(SUBLANES, N2)
    nj = N2 // tn2
    twc = np.cos(tw).reshape(N1, nj, tn2).transpose(1, 0, 2)
    tws = np.sin(tw).reshape(N1, nj, tn2).transpose(1, 0, 2)
    cg, sg = _dft_mats(FT_GROUP)
    scale = 1.0 / math.sqrt(T * FT_GROUP)
    eye2 = np.eye(LANES // FT_GROUP)
    cc = np.kron(eye2, cg) * scale
    sc = np.kron(eye2, sg) * scale
    f = lambda a: jnp.asarray(a, F32)
    full2 = lambda n, m: pl.BlockSpec((n, m), lambda b, j: (0, 0))

    ar, ai = pl.pallas_call(
        functools.partial(_fft1_kernel, tn2=tn2, ch=ch),
        grid=(B, nj),
        in_specs=[pl.BlockSpec((1, N1, tn2 * ch), lambda b, j: (b, 0, j)),
                  full2(N1, N1), full2(N1, N1),
                  pl.BlockSpec((1, N1, tn2), lambda b, j: (j, 0, 0)),
                  pl.BlockSpec((1, N1, tn2), lambda b, j: (j, 0, 0))],
        out_specs=[pl.BlockSpec((1, tn2, N1, ch), lambda b, j: (b, j, 0, 0))] * 2,
        out_shape=[jax.ShapeDtypeStruct((B, N2, N1, ch), F32)] * 2,
        compiler_params=_cparams("parallel", "parallel"),
        name="fft_stage1",
    )(u.reshape(B, N1, N2 * ch), f(c1), f(s1), f(twc), f(tws))

    tk1 = min(SUBLANES, N1)
    blk = pl.BlockSpec((1, N2, tk1 * ch), lambda b, j: (b, 0, j))
    out = pl.pallas_call(
        _fft2_kernel,
        grid=(B, N1 // tk1),
        in_specs=[blk, blk, full2(N2, N2), full2(N2, N2), full2(LANES, LANES), full2(LANES, LANES)],
        out_specs=blk,
        out_shape=jax.ShapeDtypeStruct((B, N2, N1 * ch), F32),
        compiler_params=_cparams("parallel", "parallel"),
        name="fft_stage2",
    )(ar.reshape(B, N2, N1 * ch), ai.reshape(B, N2, N1 * ch), f(c2), f(s2), f(cc), f(sc))
    return out.reshape(B, T, ch)


HEAD_SLAB = LANES
ROPE_SHIFT = HEAD_SLAB - QK_ROPE
ONES_ROWS = 16
V_ROWS = V_HEAD + ONES_ROWS
ATTN_UNROLL = 6


def _rms(x, w):
    return x * lax.rsqrt(jnp.mean(x * x, axis=-1, keepdims=True) + RMS_EPS) * w


def _qproj_kernel(cq_ref, nw_ref, w_ref, ct_ref, st_ref, q_o):
    q = _dot(_rms(cq_ref[0], nw_ref[...]).astype(BF16), w_ref[...])
    ct = ct_ref[...]
    st = st_ref[...]
    for h in range(MLA_HEADS):
        sl = slice(h * HEAD_SLAB, (h + 1) * HEAD_SLAB)
        s = q[:, sl]
        q_o[0, :, sl] = ((s * ct + pltpu.roll(s, ROPE_SHIFT, 1) * st) * Q_SCALE).astype(BF16)


def _kvproj_kernel(ckv_ref, nw_ref, wk_ref, wvt_ref, ct_ref, st_ref, k_o, vt_o):
    x = ckv_ref[0]
    n = _rms(x[:, :KV_RANK], nw_ref[...]).astype(BF16)
    rs = x[:, KV_RANK:KV_RANK + HEAD_SLAB]
    rope = rs * ct_ref[...] + pltpu.roll(rs, ROPE_SHIFT, 1) * st_ref[...]
    kn = _dot(n, wk_ref[...])
    for h in range(MLA_HEADS):
        sl = slice(h * HEAD_SLAB, (h + 1) * HEAD_SLAB)
        k_o[0, :, sl] = (kn[:, sl] + rope).astype(BF16)
    vt = lax.dot_general(wvt_ref[...], n, NT, preferred_element_type=F32)
    row = lax.broadcasted_iota(jnp.int32, vt.shape, 0)
    vt_o[0] = jnp.where(row % V_ROWS >= V_HEAD, 1.0, vt).astype(BF16)


def q_projection(cq, nw, w, ct, st):
    B, T, R = cq.shape
    tm = _tile(T, 256)
    W = MLA_HEADS * HEAD_SLAB
    tab = pl.BlockSpec((tm, HEAD_SLAB), lambda b, i: (i, 0))
    return pl.pallas_call(
        _qproj_kernel,
        grid=(B, T // tm),
        in_specs=[pl.BlockSpec((1, tm, R), lambda b, i: (b, i, 0)),
                  pl.BlockSpec(nw.shape, lambda b, i: (0, 0)),
                  pl.BlockSpec(w.shape, lambda b, i: (0, 0)), tab, tab],
        out_specs=pl.BlockSpec((1, tm, W), lambda b, i: (b, i, 0)),
        out_shape=jax.ShapeDtypeStruct((B, T, W), BF16),
        compiler_params=_cparams("parallel", "parallel"),
        name="q_projection",
    )(cq, nw, w, ct, st)


def kv_projection(ckv, nw, wk, wvt, ct, st):
    B, T, R = ckv.shape
    tm = _tile(T, 256)
    W = MLA_HEADS * HEAD_SLAB
    tab = pl.BlockSpec((tm, HEAD_SLAB), lambda b, i: (i, 0))
    return pl.pallas_call(
        _kvproj_kernel,
        grid=(B, T // tm),
        in_specs=[pl.BlockSpec((1, tm, R), lambda b, i: (b, i, 0)),
                  pl.BlockSpec(nw.shape, lambda b, i: (0, 0)),
                  pl.BlockSpec(wk.shape, lambda b, i: (0, 0)),
                  pl.BlockSpec(wvt.shape, lambda b, i: (0, 0)), tab, tab],
        out_specs=[pl.BlockSpec((1, tm, W), lambda b, i: (b, i, 0)),
                   pl.BlockSpec((1, MLA_HEADS * V_ROWS, tm), lambda b, i: (b, 0, i))],
        out_shape=[jax.ShapeDtypeStruct((B, T, W), BF16), jax.ShapeDtypeStruct((B, MLA_HEADS * V_ROWS, T), BF16)],
        compiler_params=_cparams("parallel", "parallel"),
        name="kv_projection",
    )(ckv, nw, wk, wvt, ct, st)


def _attn_kernel(q_ref, k_ref, vt_ref, o_ref, m_scr, acc_scr, sa_scr, sb_scr, ma_scr, mb_scr, *, tkc, nkc):
    m_scr[...] = jnp.full(m_scr.shape, -jnp.inf, F32)
    acc_scr[...] = jnp.zeros(acc_scr.shape, F32)
    buf_a = (sa_scr, ma_scr)
    buf_b = (sb_scr, mb_scr)

    def chunk(c):
        return pl.ds(pl.multiple_of(c * tkc, tkc), tkc)

    def scores(c, dst, dmax):
        for hh in range(2):
            q = q_ref[0, :, hh * HEAD_SLAB:(hh + 1) * HEAD_SLAB]
            kc = k_ref[0, chunk(c), hh * HEAD_SLAB:(hh + 1) * HEAD_SLAB]
            s = lax.dot_general(kc, q, NT, preferred_element_type=F32)
            dst[hh] = s
            dmax[hh] = jnp.max(s, axis=0, keepdims=True)

    def consume(src, smax, c):
        for hh in range(2):
            m_old = m_scr[hh]
            m_new = jnp.maximum(m_old, smax[hh])
            pr = jnp.exp2(src[hh] - m_new).astype(BF16)
            alpha = jnp.exp2(m_old - m_new)
            m_scr[hh] = m_new
            rows = slice(hh * V_ROWS, (hh + 1) * V_ROWS)
            acc_scr[rows, :] = alpha * acc_scr[rows, :] + _dot(vt_ref[0, rows, chunk(c)], pr)

    scores(0, *buf_a)
    unroll = ATTN_UNROLL if nkc > 2 * ATTN_UNROLL else 1
    ngroups = (nkc - 1) // (2 * unroll)

    def body(i, carry):
        c = 2 * unroll * i
        for _ in range(unroll):
            scores(c + 1, *buf_b)
            consume(*buf_a, c)
            scores(c + 2, *buf_a)
            consume(*buf_b, c + 1)
            c = c + 2
        return carry

    lax.fori_loop(0, ngroups, body, 0)
    done = 2 * unroll * ngroups
    for cc in range(done, nkc):
        src, dst = (buf_a, buf_b) if (cc - done) % 2 == 0 else (buf_b, buf_a)
        if cc + 1 < nkc:
            scores(cc + 1, *dst)
        consume(*src, cc)
    outs = []
    for hh in range(2):
        base = hh * V_ROWS
        outs.append(acc_scr[base:base + V_HEAD, :] / acc_scr[base + V_HEAD:base + V_HEAD + 1, :])
    o_ref[0] = jnp.concatenate(outs, axis=0).T.astype(o_ref.dtype)


def attention(q, k, vt):
    B, T, _ = q.shape
    Tk = k.shape[1]
    tq = _tile(T, 256)
    tkc = next(c for c in (640, 512, 256, 128) if Tk % c == 0)
    hp = MLA_HEADS // 2
    return pl.pallas_call(
        functools.partial(_attn_kernel, tkc=tkc, nkc=Tk // tkc),
        grid=(B, hp, T // tq),
        in_specs=[pl.BlockSpec((1, tq, 2 * HEAD_SLAB), lambda b, h, i: (b, i, h)),
                  pl.BlockSpec((1, Tk, 2 * HEAD_SLAB), lambda b, h, i: (b, 0, h)),
                  pl.BlockSpec((1, 2 * V_ROWS, Tk), lambda b, h, i: (b, h, 0))],
        out_specs=pl.BlockSpec((1, tq, 2 * V_HEAD), lambda b, h, i: (b, i, h)),
        out_shape=jax.ShapeDtypeStruct((B, T, MLA_WIDTH), BF16),
        scratch_shapes=[pltpu.VMEM((2, 1, tq), F32), pltpu.VMEM((2 * V_ROWS, tq), F32),
                        pltpu.VMEM((2, tkc, tq), F32), pltpu.VMEM((2, tkc, tq), F32),
                        pltpu.VMEM((2, 1, tq), F32), pltpu.VMEM((2, 1, tq), F32)],
        compiler_params=_cparams("parallel", "parallel", "arbitrary"),
        name="mla_attention",
    )(q, k, vt)


def _outproj_kernel(rw_ref, ft_ref, att_ref, x_ref, g_ref, lw_ref, lb_ref, w_ref, o_ref, *, alpha):
    w = w_ref
    mix = (_dot(rw_ref[0].astype(BF16), w[0:RW_WIDTH, :])
           + _dot(ft_ref[0].astype(BF16), w[RW_WIDTH:RW_WIDTH + FT_WIDTH, :])
           + _dot(att_ref[0], w[RW_WIDTH + FT_WIDTH:, :]))
    z = alpha * x_ref[0] + g_ref[0] * mix
    o_ref[0] = _standardize(z, LN_EPS) * lw_ref[...] + lb_ref[...]


def out_projection_ln(rw, ft, att, x, gate, ln_w, ln_b, w_out, alpha):
    B, T, D = x.shape
    tm = _tile(T, 256)
    tok = lambda n: pl.BlockSpec((1, tm, n), lambda b, i: (b, i, 0))
    row = pl.BlockSpec((1, D), lambda b, i: (0, 0))
    return pl.pallas_call(
        functools.partial(_outproj_kernel, alpha=alpha),
        grid=(B, T // tm),
        in_specs=[tok(RW_WIDTH), tok(FT_WIDTH), tok(MLA_WIDTH), tok(D),
                  pl.BlockSpec((1, 1, D), lambda b, i: (b, 0, 0)), row, row,
                  pl.BlockSpec(w_out.shape, lambda b, i: (0, 0))],
        out_specs=tok(D),
        out_shape=jax.ShapeDtypeStruct((B, T, D), F32),
        compiler_params=_cparams("parallel", "parallel"),
        name="out_projection_ln",
    )(rw, ft, att, x, gate, ln_w, ln_b, w_out)


def _ffn_kernel(x_ref, sh_ref, sc_ref, g_ref, lw_ref, lb_ref, w1_ref, w3_ref, w2_ref, o_ref, h_scr, acc_scr, *, alpha):
    f = pl.program_id(2)

    @pl.when(f == 0)
    def _():
        h = _standardize(x_ref[0], MOD_EPS) * (1.0 + sc_ref[0]) + sh_ref[0]
        h_scr[...] = h.astype(BF16)
        acc_scr[...] = jnp.zeros(acc_scr.shape, F32)

    hb = h_scr[...]
    a = _dot(hb, w1_ref[...])
    b = _dot(hb, w3_ref[...])
    acc_scr[...] += _dot((a * jax.nn.sigmoid(a) * b).astype(BF16), w2_ref[...])

    @pl.when(f == pl.num_programs(2) - 1)
    def _():
        z = alpha * x_ref[0] + g_ref[0] * acc_scr[...]
        o_ref[0] = _standardize(z, LN_EPS) * lw_ref[...] + lb_ref[...]


def ffn_ln(x, shift, scale, gate, ln_w, ln_b, w1, w3, w2, alpha):
    B, T, D = x.shape
    F = w1.shape[1]
    tm = _tile(T, 512)
    tf = _tile(F, 512)
    tok = pl.BlockSpec((1, tm, D), lambda b, i, f: (b, i, 0))
    vec = pl.BlockSpec((1, 1, D), lambda b, i, f: (b, 0, 0))
    row = pl.BlockSpec((1, D), lambda b, i, f: (0, 0))
    return pl.pallas_call(
        functools.partial(_ffn_kernel, alpha=alpha),
        grid=(B, T // tm, F // tf),
        in_specs=[tok, vec, vec, vec, row, row,
                  pl.BlockSpec((D, tf), lambda b, i, f: (0, f)),
                  pl.BlockSpec((D, tf), lambda b, i, f: (0, f)),
                  pl.BlockSpec((tf, D), lambda b, i, f: (f, 0))],
        out_specs=tok,
        out_shape=jax.ShapeDtypeStruct((B, T, D), F32),
        scratch_shapes=[pltpu.VMEM((tm, D), BF16), pltpu.VMEM((tm, D), F32)],
        compiler_params=_cparams("parallel", "parallel", "arbitrary"),
        name="ffn_ln",
    )(x, shift, scale, gate, ln_w, ln_b, w1, w3, w2)


def _router_kernel(x_ref, sh_ref, sc_ref, wr_ref, h_o, ti_o, tg_o):
    h = _standardize(x_ref[0], MOD_EPS) * (1.0 + sc_ref[0]) + sh_ref[0]
    h_o[0] = h.astype(BF16)
    logits = _dot(h, wr_ref[...], HI)
    lane = lax.broadcasted_iota(jnp.int32, logits.shape, 1)
    neg = jnp.float32(-jnp.inf)
    logits = jnp.where(lane < N_EXPERTS, logits, neg)
    m1 = jnp.max(logits, axis=-1, keepdims=True)
    i1 = jnp.min(jnp.where(logits == m1, lane, LANES), axis=-1, keepdims=True)
    rest = jnp.where(lane == i1, neg, logits)
    m2 = jnp.max(rest, axis=-1, keepdims=True)
    i2 = jnp.min(jnp.where(rest == m2, lane, LANES), axis=-1, keepdims=True)
    e = jnp.exp(m2 - m1)
    g1 = 1.0 / (1.0 + e)
    g2 = e / (1.0 + e)
    ti_o[0] = jnp.where(lane == 0, i1, jnp.where(lane == 1, i2, 0))
    tg_o[0] = jnp.where(lane == 0, g1, jnp.where(lane == 1, g2, 0.0))


def moe_router(x, shift, scale, wr):
    B, T, D = x.shape
    tm = _tile(T, 512)
    tok = lambda n: pl.BlockSpec((1, tm, n), lambda b, i: (b, i, 0))
    vec = pl.BlockSpec((1, 1, D), lambda b, i: (b, 0, 0))
    return pl.pallas_call(
        _router_kernel,
        grid=(B, T // tm),
        in_specs=[tok(D), vec, vec, pl.BlockSpec(wr.shape, lambda b, i: (0, 0))],
        out_specs=[tok(D), tok(LANES), tok(LANES)],
        out_shape=[jax.ShapeDtypeStruct((B, T, D), BF16), jax.ShapeDtypeStruct((B, T, LANES), jnp.int32),
                   jax.ShapeDtypeStruct((B, T, LANES), F32)],
        compiler_params=_cparams("parallel", "parallel"),
        name="moe_router",
    )(x, shift, scale, wr)


def _expert_kernel(be_ref, bv_ref, x_ref, w1_ref, w3_ref, w2_ref, o_ref, acc_scr):
    blk = pl.program_id(0)
    f = pl.program_id(1)
    valid = bv_ref[blk] > 0

    @pl.when(f == 0)
    def _():
        acc_scr[...] = jnp.zeros(acc_scr.shape, F32)

    @pl.when(valid)
    def _():
        xb = x_ref[...]
        a = _dot(xb, w1_ref[0].astype(BF16))
        b = _dot(xb, w3_ref[0].astype(BF16))
        acc_scr[...] += _dot((a * jax.nn.sigmoid(a) * b).astype(BF16), w2_ref[0].astype(BF16))

    @pl.when(f == pl.num_programs(1) - 1)
    def _():
        o_ref[...] = acc_scr[...]


def expert_ffn(xg, blk_expert, blk_valid, w1, w3, w2):
    P, D = xg.shape
    F = w1.shape[2]
    tf = _tile(F, 256)
    nblk = P // MOE_ROWS
    grid_spec = pltpu.PrefetchScalarGridSpec(
        num_scalar_prefetch=2,
        grid=(nblk, F // tf),
        in_specs=[pl.BlockSpec((MOE_ROWS, D), lambda i, f, be, bv: (i, 0)),
                  pl.BlockSpec((1, D, tf), lambda i, f, be, bv: (be[i], 0, jnp.where(bv[i] > 0, f, 0))),
                  pl.BlockSpec((1, D, tf), lambda i, f, be, bv: (be[i], 0, jnp.where(bv[i] > 0, f, 0))),
                  pl.BlockSpec((1, tf, D), lambda i, f, be, bv: (be[i], jnp.where(bv[i] > 0, f, 0), 0))],
        out_specs=pl.BlockSpec((MOE_ROWS, D), lambda i, f, be, bv: (i, 0)),
        scratch_shapes=[pltpu.VMEM((MOE_ROWS, D), F32)],
    )
    return pl.pallas_call(
        _expert_kernel,
        grid_spec=grid_spec,
        out_shape=jax.ShapeDtypeStruct((P, D), F32),
        compiler_params=_cparams("parallel", "arbitrary"),
        name="expert_ffn",
    )(blk_expert, blk_valid, xg, w1, w3, w2)


def _combine_ln_kernel(x_ref, y0_ref, y1_ref, tg_ref, g_ref, lw_ref, lb_ref, o_ref, *, alpha):
    tg = tg_ref[0]
    y = tg[:, 0:1] * y0_ref[0] + tg[:, 1:2] * y1_ref[0]
    z = alpha * x_ref[0] + g_ref[0] * y
    o_ref[0] = _standardize(z, LN_EPS) * lw_ref[...] + lb_ref[...]


def combine_ln(x, y0, y1, tg, gate, ln_w, ln_b, alpha):
    B, T, D = x.shape
    tm = _tile(T, 512)
    tok = pl.BlockSpec((1, tm, D), lambda b, i: (b, i, 0))
    row = pl.BlockSpec((1, D), lambda b, i: (0, 0))
    return pl.pallas_call(
        functools.partial(_combine_ln_kernel, alpha=alpha),
        grid=(B, T // tm),
        in_specs=[tok, tok, tok, pl.BlockSpec((1, tm, LANES), lambda b, i: (b, i, 0)),
                  pl.BlockSpec((1, 1, D), lambda b, i: (b, 0, 0)), row, row],
        out_specs=tok,
        out_shape=jax.ShapeDtypeStruct((B, T, D), F32),
        compiler_params=_cparams("parallel", "parallel"),
        name="combine_ln",
    )(x, y0, y1, tg, gate, ln_w, ln_b)


def moe_ln(x, shift, scale, gate, ln_w, ln_b, router, w1, w3, w2, alpha):
    B, T, D = x.shape
    N = B * T
    E = router.shape[1]
    wr = jnp.zeros((D, LANES), F32).at[:, :E].set(router)
    h, ti, tg = moe_router(x, shift, scale, wr)
    top_i = ti.reshape(N, LANES)[:, :2]
    e_flat = top_i.reshape(-1)
    onehot = (e_flat[:, None] == jnp.arange(E, dtype=jnp.int32)[None, :]).astype(jnp.int32)
    rank = jnp.sum((jnp.cumsum(onehot, axis=0) - onehot) * onehot, axis=1)
    counts = jnp.sum(onehot, axis=0)
    padded = (counts + MOE_ROWS - 1) // MOE_ROWS * MOE_ROWS
    p_ends = jnp.cumsum(padded)
    p_starts = p_ends - padded
    dest = p_starts[e_flat] + rank
    P = -(-(2 * N) // MOE_ROWS) * MOE_ROWS + E * MOE_ROWS
    nblk = P // MOE_ROWS
    tok_flat = jnp.repeat(jnp.arange(N, dtype=jnp.int32), 2)
    slot_tok = jnp.zeros((P,), jnp.int32).at[dest].set(tok_flat)
    blk_start = jnp.arange(nblk, dtype=jnp.int32) * MOE_ROWS
    blk_expert = jnp.minimum(jnp.searchsorted(p_ends, blk_start, side="right"), E - 1).astype(jnp.int32)
    blk_valid = (blk_start < p_ends[-1]).astype(jnp.int32)
    xg = h.reshape(N, D)[slot_tok]
    yg = expert_ffn(xg, blk_expert, blk_valid, w1, w3, w2)
    d2 = dest.reshape(N, 2)
    y0 = yg[d2[:, 0]].reshape(B, T, D)
    y1 = yg[d2[:, 1]].reshape(B, T, D)
    return combine_ln(x, y0, y1, tg, gate, ln_w, ln_b, alpha)


def _rope_partner(w):
    half = ROPE_AXIS // 2
    idx = np.arange(QK_ROPE)
    first = (idx % ROPE_AXIS) < half
    src = np.where(first, idx + half, idx - half)
    sign = np.where(first, -1.0, 1.0).astype(np.float32)
    return w[:, src] * sign


def _layer_params(l, w_in, rw_conv, rw_w0, rw_w_up, rw_a0, rw_a_up, rw_g_up, rw_k_k, rw_k_a, rw_r_k, rw_gn_w,
                  rw_gn_b, mla_q_norm, mla_w_uq, mla_kv_norm, mla_w_ukv, w_out):
    D = w_in.shape[1]
    wi = w_in[l]
    o = np.cumsum([0, 3 * RW_WIDTH, G_RANK, LORA_RANK, LORA_RANK, LORA_RANK, LORA_RANK, FT_WIDTH, Q_RANK, KV_RANK, QK_ROPE])
    piece = lambda i: wi[:, o[i]:o[i + 1]]
    zpad = lambda n: jnp.zeros((D, n), F32)
    lora = [jnp.concatenate([piece(i), zpad(LANES - LORA_RANK)], axis=1) for i in (2, 3, 4, 5)]
    w_rw = jnp.concatenate([piece(0), piece(1)] + lora, axis=1)
    kr = piece(9)
    w_ckv = jnp.concatenate([piece(8), zpad(QK_NOPE), kr, _rope_partner(kr)], axis=1)
    pad_rows = lambda a: jnp.concatenate([a, jnp.zeros((a.shape[0], LANES - LORA_RANK, a.shape[2]), F32)], axis=1)
    head = jnp.arange(RW_WIDTH) // RW_HEAD
    uq = mla_w_uq[l].reshape(Q_RANK, MLA_HEADS, QK_NOPE + QK_ROPE)
    uq_rope = uq[:, :, QK_NOPE:]
    uq_partner = _rope_partner(uq_rope.reshape(Q_RANK * MLA_HEADS, QK_ROPE)).reshape(Q_RANK, MLA_HEADS, QK_ROPE)
    w_q = jnp.concatenate([uq, uq_partner], axis=2).reshape(Q_RANK, MLA_HEADS * HEAD_SLAB)
    ukv = mla_w_ukv[l].reshape(KV_RANK, MLA_HEADS, QK_NOPE + V_HEAD)
    w_k = jnp.concatenate([ukv[:, :, :QK_NOPE], jnp.zeros((KV_RANK, MLA_HEADS, HEAD_SLAB - QK_NOPE), F32)],
                          axis=2).reshape(KV_RANK, MLA_HEADS * HEAD_SLAB)
    w_v = jnp.concatenate([ukv[:, :, QK_NOPE:], jnp.zeros((KV_RANK, MLA_HEADS, ONES_ROWS), F32)],
                          axis=2).reshape(KV_RANK, MLA_HEADS * V_ROWS)
    return {
        "w_in": [w.astype(BF16) for w in (w_rw, piece(6), piece(7), w_ckv)],
        "rw": {"conv": rw_conv[l], "k_k": rw_k_k[l][None], "k_a": rw_k_a[l][None], "r_k": rw_r_k[l].reshape(1, RW_WIDTH),
               "w0": rw_w0[l], "a0": rw_a0[l], "w_up": pad_rows(rw_w_up[l]).astype(BF16), "a_up": pad_rows(rw_a_up[l]).astype(BF16),
               "g_up": rw_g_up[l].astype(BF16), "gn_w": rw_gn_w[l][None], "gn_b": rw_gn_b[l][None],
               "E": (head[:, None] == head[None, :]).astype(BF16)},
        "q_norm": mla_q_norm[l][None], "w_q": w_q.astype(BF16),
        "kv_norm": mla_kv_norm[l][None], "w_k": w_k.astype(BF16), "w_vt": w_v.T.astype(BF16),
        "w_out": w_out[l].astype(BF16),
    }


def _rope_tables(T, use_rope):
    ones = jnp.ones((T, QK_NOPE), F32)
    zeros = jnp.zeros((T, QK_NOPE), F32)
    zpad = jnp.zeros((T, HEAD_SLAB - QK_NOPE - QK_ROPE), F32)
    if use_rope:
        row = jnp.repeat(jnp.arange(T // GRID_W), GRID_W).astype(F32)
        col = (jnp.arange(T) % GRID_W).astype(F32)
        inv = ROPE_THETA ** (-jnp.arange(0, ROPE_AXIS, 2, dtype=F32) / ROPE_AXIS)
        ang = jnp.stack([row[:, None] * inv, col[:, None] * inv], axis=1)
        ang = jnp.broadcast_to(ang[:, :, None, :], (T, 2, 2, ROPE_AXIS // 2)).reshape(T, QK_ROPE)
        cos, sin = jnp.cos(ang), jnp.sin(ang)
    else:
        cos, sin = jnp.ones((T, QK_ROPE), F32), jnp.zeros((T, QK_ROPE), F32)
    return jnp.concatenate([ones, cos, zpad], axis=1), jnp.concatenate([zeros, sin, zpad], axis=1)


def _mixer(h_pieces_lat, h_pieces_ctx, p, tabs_lat, tabs_ctx, emit_ctx):
    rw_l, ft_l, cq_l, ckv_l = h_pieces_lat
    rw_c, ft_c, cq_c, ckv_c = h_pieces_ctx
    rwo_l, rwo_c = rwkv_branch(rw_l, rw_c, p["rw"], emit_ctx)
    fto_l = fourier_mixer(ft_l)
    q_l = q_projection(cq_l, p["q_norm"], p["w_q"], *tabs_lat)
    k_l, vt_l = kv_projection(ckv_l, p["kv_norm"], p["w_k"], p["w_vt"], *tabs_lat)
    k_c, vt_c = kv_projection(ckv_c, p["kv_norm"], p["w_k"], p["w_vt"], *tabs_ctx)
    att_l = attention(q_l, jnp.concatenate([k_l, k_c], axis=1), jnp.concatenate([vt_l, vt_c], axis=2))
    out_c = None
    if emit_ctx:
        fto_c = fourier_mixer(ft_c)
        q_c = q_projection(cq_c, p["q_norm"], p["w_q"], *tabs_ctx)
        att_c = attention(q_c, k_c, vt_c)
        out_c = (rwo_c, fto_c, att_c)
    return (rwo_l, fto_l, att_l), out_c


def kernel(x, c, ctx, c_ctx, ada_w, ada_b, w_in, rw_conv, rw_w0, rw_w_up, rw_a0, rw_a_up, rw_g_up, rw_k_k, rw_k_a,
           rw_r_k, rw_gn_w, rw_gn_b, mla_q_norm, mla_w_uq, mla_kv_norm, mla_w_ukv, w_out, ln1_w, ln1_b, ln2_w, ln2_b,
           ffn_w1, ffn_w3, ffn_w2, moe_router, moe_w1, moe_w3, moe_w2):
    B, T, D = x.shape
    Tc = ctx.shape[1]
    depth = w_in.shape[0]
    alpha = (2 * depth) ** 0.25
    assert B + 1 <= SUBLANES
    cc = jnp.zeros((SUBLANES, D), F32).at[:B].set(c).at[B].set(c_ctx)
    ada = ada_vectors(cc, ada_w, ada_b)
    tabs_lat = _rope_tables(T, True)
    tabs_ctx = _rope_tables(Tc, False)
    for l in range(depth):
        last = l == depth - 1
        p = _layer_params(l, w_in, rw_conv, rw_w0, rw_w_up, rw_a0, rw_a_up, rw_g_up, rw_k_k, rw_k_a, rw_r_k,
                          rw_gn_w, rw_gn_b, mla_q_norm, mla_w_uq, mla_kv_norm, mla_w_ukv, w_out)
        mods = ada[l].reshape(SUBLANES, 6, D)
        lat = [mods[:B, j][:, None, :] for j in range(6)]
        cx = [jnp.broadcast_to(mods[B, j][None, None, :], (B, 1, D)) for j in range(6)]
        sh_m, sc_m, g_m, sh_f, sc_f, g_f = lat
        csh_m, csc_m, cg_m, csh_f, csc_f, cg_f = cx
        ln1 = (ln1_w[l][None], ln1_b[l][None])
        ln2 = (ln2_w[l][None], ln2_b[l][None])

        pieces_l = in_projection(x, sh_m, sc_m, p["w_in"])
        pieces_c = in_projection(ctx, csh_m, csc_m, p["w_in"])
        mix_l, mix_c = _mixer(pieces_l, pieces_c, p, tabs_lat, tabs_ctx, not last)
        x = out_projection_ln(*mix_l, x, g_m, *ln1, p["w_out"], alpha)
        i = l // 2
        if l % 2 == 0:
            dense = (ffn_w1[i].astype(BF16), ffn_w3[i].astype(BF16), ffn_w2[i].astype(BF16))
            x = ffn_ln(x, sh_f, sc_f, g_f, *ln2, *dense, alpha)
        else:
            x = moe_ln(x, sh_f, sc_f, g_f, *ln2, moe_router[i], moe_w1[i], moe_w3[i], moe_w2[i], alpha)
        if not last:
            ctx = out_projection_ln(*mix_c, ctx, cg_m, *ln1, p["w_out"], alpha)
            if l % 2 == 0:
                ctx = ffn_ln(ctx, csh_f, csc_f, cg_f, *ln2, *dense, alpha)
            else:
                ctx = moe_ln(ctx, csh_f, csc_f, cg_f, *ln2, moe_router[i], moe_w1[i], moe_w3[i], moe_w2[i], alpha)
    return x
```

```python
import functools
import math

import numpy as np
import jax
import jax.numpy as jnp
from jax import lax
from jax.experimental import pallas as pl
from jax.experimental.pallas import tpu as pltpu

F32 = jnp.float32
BF16 = jnp.bfloat16
HI = lax.Precision.HIGHEST

LANES = 128
SUBLANES = 8
VMEM_LIMIT = 56 * 1024 * 1024

GRID_W = 64
RW_HEADS = 8
RW_HEAD = 64
RW_WIDTH = RW_HEADS * RW_HEAD
G_RANK = 128
LORA_RANK = 64
GN_EPS = 64e-5
FT_GROUP = 64
FT_WIDTH = 512
MLA_HEADS = 16
QK_NOPE = 64
QK_ROPE = 32
V_HEAD = 64
MLA_WIDTH = MLA_HEADS * V_HEAD
Q_RANK = 512
KV_RANK = 256
ROPE_AXIS = QK_ROPE // 2
ROPE_THETA = 10000.0
ATTN_SCALE = (QK_NOPE + QK_ROPE) ** -0.5
Q_SCALE = ATTN_SCALE * math.log2(math.e)
N_EXPERTS = 8
LN_EPS = 1e-5
MOD_EPS = 1e-6
RMS_EPS = 1e-6
CHUNK = 64
INV_BLOCK = 16
MOE_ROWS = 1024

NT = (((1,), (1,)), ((), ()))
TN = (((0,), (0,)), ((), ()))


def _cparams(*sem):
    return pltpu.CompilerParams(dimension_semantics=sem, vmem_limit_bytes=VMEM_LIMIT)


def _dot(a, b, prec=None):
    return jnp.dot(a, b, precision=prec, preferred_element_type=F32)


def _dot_split(x, w):
    hi = x.astype(BF16)
    r1 = x - hi.astype(F32)
    mid = r1.astype(BF16)
    lo = (r1 - mid.astype(F32)).astype(BF16)
    return _dot(hi, w) + _dot(mid, w) + _dot(lo, w)


def _dot3(a, b):
    a_hi = a.astype(BF16)
    b_hi = b.astype(BF16)
    a_lo = (a - a_hi.astype(F32)).astype(BF16)
    b_lo = (b - b_hi.astype(F32)).astype(BF16)
    return _dot(a_hi, b_hi) + _dot(a_hi, b_lo) + _dot(a_lo, b_hi)


def _standardize(x, eps):
    mu = jnp.mean(x, axis=-1, keepdims=True)
    xc = x - mu
    var = jnp.mean(xc * xc, axis=-1, keepdims=True)
    return xc * lax.rsqrt(var + eps)


def _tile(n, pref):
    t = min(n, pref)
    assert n % t == 0, (n, pref)
    return t


def _ada_kernel(c_ref, w_ref, b_ref, o_ref):
    c = c_ref[...]
    s = c * jax.nn.sigmoid(c)
    o_ref[0] = _dot(s, w_ref[0], HI) + b_ref[0]


def ada_vectors(cc, ada_w, ada_b):
    L, D, N6 = ada_w.shape
    tn = _tile(N6, 1024)
    return pl.pallas_call(
        _ada_kernel,
        grid=(L, N6 // tn),
        in_specs=[pl.BlockSpec((SUBLANES, D), lambda l, j: (0, 0)),
                  pl.BlockSpec((1, D, tn), lambda l, j: (l, 0, j)),
                  pl.BlockSpec((1, 1, tn), lambda l, j: (l, 0, j))],
        out_specs=pl.BlockSpec((1, SUBLANES, tn), lambda l, j: (l, 0, j)),
        out_shape=jax.ShapeDtypeStruct((L, SUBLANES, N6), F32),
        compiler_params=_cparams("parallel", "parallel"),
        name="ada_vectors",
    )(cc, ada_w, ada_b.reshape(L, 1, N6))


def _inproj_kernel(x_ref, sh_ref, sc_ref, *refs):
    nw = len(refs) // 2
    h = _standardize(x_ref[0], MOD_EPS) * (1.0 + sc_ref[0]) + sh_ref[0]
    hb = h.astype(BF16)
    for w_ref, o_ref in zip(refs[:nw], refs[nw:]):
        o_ref[0] = _dot(hb, w_ref[...])


def in_projection(x, shift, scale, weights):
    B, T, D = x.shape
    tm = _tile(T, 256)
    vec = pl.BlockSpec((1, 1, D), lambda b, i: (b, 0, 0))
    return pl.pallas_call(
        _inproj_kernel,
        grid=(B, T // tm),
        in_specs=[pl.BlockSpec((1, tm, D), lambda b, i: (b, i, 0)), vec, vec]
        + [pl.BlockSpec(w.shape, lambda b, i: (0, 0)) for w in weights],
        out_specs=[pl.BlockSpec((1, tm, w.shape[1]), lambda b, i: (b, i, 0)) for w in weights],
        out_shape=[jax.ShapeDtypeStruct((B, T, w.shape[1]), F32) for w in weights],
        compiler_params=_cparams("parallel", "parallel"),
        name="in_projection",
    )(x, shift, scale, *weights)


RW_COLS = 3 * RW_WIDTH + G_RANK + 4 * LANES


def _softplus(z):
    return jnp.maximum(z, 0.0) + jnp.log(1.0 + jnp.exp(-jnp.abs(z)))


def _rwkv_prep_kernel(x_ref, xp_ref, xn_ref, conv_ref, kk_ref, ka_ref, rk_ref, w0_ref, a0_ref,
                      wup_ref, aup_ref, gup_ref, e_ref,
                      r_o, v_o, kk_o, ld0_o, b0_o, kr0_o, ld1_o, b1_o, kr1_o, bonus_o, gate_o):
    i = pl.program_id(1)
    n = pl.num_programs(1)
    x = x_ref[0]
    W3 = 3 * RW_WIDTH
    raw = x[:, :W3]
    tm = raw.shape[0]
    row = lax.broadcasted_iota(jnp.int32, (tm, 1), 0)
    prev_row = jnp.where(i > 0, xp_ref[0, SUBLANES - 1:SUBLANES, :], 0.0)
    next_row = jnp.where(i < n - 1, xn_ref[0, 0:1, :], 0.0)
    xm = jnp.where(row == 0, prev_row, pltpu.roll(raw, 1, 0))
    xq = jnp.where(row == tm - 1, next_row, pltpu.roll(raw, tm - 1, 0))
    cw = conv_ref[...]
    y = xm * cw[0:1] + raw * cw[1:2] + xq * cw[2:3]
    r = y[:, :RW_WIDTH]
    k = y[:, RW_WIDTH:2 * RW_WIDTH]
    v = y[:, 2 * RW_WIDTH:W3]
    E = e_ref[...]
    kkv = k * kk_ref[...]
    kk = kkv / jnp.maximum(jnp.sqrt(_dot_split(kkv * kkv, E)), 1e-12)
    r_o[0] = r
    v_o[0] = v
    kk_o[0] = kk
    g_dn = x[:, W3:W3 + G_RANK]
    gate_o[0] = _dot(jax.nn.sigmoid(g_dn).astype(BF16), gup_ref[...])
    kr_sum = jnp.zeros_like(r)
    outs = ((ld0_o, b0_o, kr0_o), (ld1_o, b1_o, kr1_o))
    for d in range(2):
        base = W3 + G_RANK
        w_dn = x[:, base + d * LANES: base + (d + 1) * LANES]
        a_dn = x[:, base + (2 + d) * LANES: base + (3 + d) * LANES]
        z = w0_ref[d:d + 1, :] + _dot(jnp.tanh(w_dn).astype(BF16), wup_ref[d])
        logw = -_softplus(-z) - 0.5
        a = jax.nn.sigmoid(a0_ref[d:d + 1, :] + _dot(a_dn.astype(BF16), aup_ref[d]))
        kr = k * (1.0 + (a - 1.0) * ka_ref[...])
        ld_o, b_o, kr_o = outs[d]
        ld_o[0] = -jnp.exp(logw)
        b_o[0] = a * kk
        kr_o[0] = kr
        kr_sum = kr_sum + kr
    bonus_o[0] = _dot_split(r * kr_sum * rk_ref[...], E) * v


def rwkv_prep(rw, p):
    B, T, _ = rw.shape
    tm = _tile(T, 256)
    nh = tm // SUBLANES
    last = T // SUBLANES - 1
    W3 = 3 * RW_WIDTH
    full = lambda a: pl.BlockSpec(a.shape, lambda b, i: (0,) * a.ndim)
    params = [p["conv"], p["k_k"], p["k_a"], p["r_k"], p["w0"], p["a0"], p["w_up"], p["a_up"], p["g_up"], p["E"]]
    outs = pl.pallas_call(
        _rwkv_prep_kernel,
        grid=(B, T // tm),
        in_specs=[pl.BlockSpec((1, tm, RW_COLS), lambda b, i: (b, i, 0)),
                  pl.BlockSpec((1, SUBLANES, W3), lambda b, i: (b, jnp.maximum(i * nh - 1, 0), 0)),
                  pl.BlockSpec((1, SUBLANES, W3), lambda b, i: (b, jnp.minimum((i + 1) * nh, last), 0))]
        + [full(a) for a in params],
        out_specs=[pl.BlockSpec((1, tm, RW_WIDTH), lambda b, i: (b, i, 0))] * 11,
        out_shape=[jax.ShapeDtypeStruct((B, T, RW_WIDTH), F32)] * 11,
        compiler_params=_cparams("parallel", "parallel"),
        name="rwkv_prep",
    )(rw, rw, rw, *params)
    return outs


PAIR = 2 * RW_HEAD
N_PAIRS = RW_HEADS // 2
CHUNK_GROUP = 4


def _pair_masks(reverse):
    i = lax.broadcasted_iota(jnp.int32, (PAIR, PAIR), 0)
    j = lax.broadcasted_iota(jnp.int32, (PAIR, PAIR), 1)
    same = (i // CHUNK) == (j // CHUNK)
    strict = same & ((j > i) if reverse else (j < i))
    incl = same & ((j >= i) if reverse else (j <= i))
    blk = (i // INV_BLOCK) == (j // INV_BLOCK)
    eye = jnp.where(i == j, 1.0, 0.0).astype(F32)
    t = lax.broadcasted_iota(jnp.int32, (CHUNK, CHUNK), 0)
    u = lax.broadcasted_iota(jnp.int32, (CHUNK, CHUNK), 1)
    tri = jnp.where((u >= t) if reverse else (u <= t), 1.0, 0.0).astype(F32)
    first = lax.broadcasted_iota(jnp.int32, (1, PAIR), 1) < RW_HEAD
    return strict, incl, blk, eye, tri, first


def _pair_chunk_math(ld, r, v, kk, b, kr, masks):
    strict, incl, blk, eye, tri, first = masks
    P = ld.shape[0]
    ein = lambda spec, a, c, prec=None: jnp.einsum(spec, a, c, precision=prec, preferred_element_type=F32)
    tri_b = jnp.broadcast_to(tri.astype(BF16), (P, CHUNK, CHUNK))
    ld_hi = ld.astype(BF16)
    ld_r = ld - ld_hi.astype(F32)
    ld_mid = ld_r.astype(BF16)
    ld_lo = (ld_r - ld_mid.astype(F32)).astype(BF16)
    Lc = ein("pct,ptk->pck", tri_b, ld_hi) + ein("pct,ptk->pck", tri_b, ld_mid) + ein("pct,ptk->pck", tri_b, ld_lo)
    Lx = Lc - ld
    Lt = jnp.sum(ld, axis=1, keepdims=True)
    ginv = jnp.exp(-Lc)
    gout = jnp.exp(Lt - Lc)
    stack = lambda x: jnp.concatenate([jnp.where(first, x, 0.0), jnp.where(first, 0.0, x)], axis=1).astype(BF16)
    twice = lambda x: jnp.concatenate([x, x], axis=1).astype(BF16)
    Xk = stack(kk * jnp.exp(Lx))
    Xr = stack(r * jnp.exp(Lc))
    Vs = stack(v)
    Bs = stack(b * gout)
    Ks = stack(kr * gout)
    mm = lambda a, c: ein("pij,pjk->pik", a.astype(BF16), c.astype(BF16))
    nt = lambda a, c: ein("pik,pjk->pij", a, c)
    tn = lambda a, c: ein("pji,pjk->pik", a, c.astype(BF16))
    XX = jnp.concatenate([Xk, Xr], axis=1)
    Mb = nt(XX, twice(b * ginv))
    Mk = nt(XX, twice(kr * ginv))
    Mab = jnp.where(strict, Mb[:, :PAIR], 0.0)
    Arb = jnp.where(incl, Mb[:, PAIR:], 0.0)
    Mak = jnp.where(strict, Mk[:, :PAIR], 0.0)
    Ark = jnp.where(incl, Mk[:, PAIR:], 0.0)
    Nd = jnp.where(blk, Mab, 0.0)
    No = Mab - Nd
    N2 = mm(Nd, Nd)
    N4 = mm(N2, N2)
    N8 = mm(N4, N4)
    Td = mm(mm(mm(eye - Nd, eye + N2), eye + N4), eye + N8)
    M2 = mm(Td, No)
    Tm = mm(mm(eye - M2, eye + mm(M2, M2)), Td)
    P1 = mm(Tm, Xk)
    P2 = mm(Tm, mm(Mak, Vs))
    Q1s = Xr.astype(F32) - mm(Arb, P1)
    Yis = mm(Ark, Vs) - mm(Arb, P2)
    G = eye * jnp.exp(Lt) - tn(Bs, P1)
    H = tn(Ks, Vs) - tn(Bs, P2)
    return Q1s[:, :CHUNK] + Q1s[:, CHUNK:], Yis[:, :CHUNK] + Yis[:, CHUNK:], G, H


def _rwkv_chunk_kernel(ld_ref, r_ref, v_ref, kk_ref, b_ref, kr_ref, q_o, yi_o, g_o, h_o, *, nch, reverse):
    masks = _pair_masks(reverse)
    group = min(CHUNK_GROUP, nch)

    def body(i, carry):
        c0 = i * group
        rows = [pl.ds(pl.multiple_of((c0 + u) * CHUNK, CHUNK), CHUNK) for u in range(group)]
        get = lambda ref: jnp.stack([ref[0, rows[u], j * PAIR:(j + 1) * PAIR]
                                     for u in range(group) for j in range(N_PAIRS)])
        Q1, Yi, G, H = _pair_chunk_math(get(ld_ref), get(r_ref), get(v_ref), get(kk_ref), get(b_ref), get(kr_ref),
                                        masks)
        for u in range(group):
            for j in range(N_PAIRS):
                q_o[0, rows[u], j * PAIR:(j + 1) * PAIR] = Q1[u * N_PAIRS + j]
                yi_o[0, rows[u], j * PAIR:(j + 1) * PAIR] = Yi[u * N_PAIRS + j]
        mats = pl.ds(c0 * N_PAIRS, group * N_PAIRS)
        g_o[0, mats] = G
        h_o[0, mats] = H
        return carry

    lax.fori_loop(0, nch // group, body, 0)


def rwkv_chunks(ld, r, v, kk, b, kr, reverse):
    B, T, W = ld.shape
    tt = _tile(T, 4 * CHUNK)
    nch = tt // CHUNK
    tok = pl.BlockSpec((1, tt, W), lambda bb, i: (bb, i, 0))
    mat = pl.BlockSpec((1, nch * N_PAIRS, PAIR, PAIR), lambda bb, i: (bb, i, 0, 0))
    mats = jax.ShapeDtypeStruct((B, T // CHUNK * N_PAIRS, PAIR, PAIR), F32)
    return pl.pallas_call(
        functools.partial(_rwkv_chunk_kernel, nch=nch, reverse=reverse),
        grid=(B, T // tt),
        in_specs=[tok] * 6,
        out_specs=[tok, tok, mat, mat],
        out_shape=[jax.ShapeDtypeStruct((B, T, W), F32)] * 2 + [mats, mats],
        compiler_params=_cparams("parallel", "parallel"),
        name="rwkv_chunks_bwd" if reverse else "rwkv_chunks_fwd",
    )(ld, r, v, kk, b, kr)


def _rwkv_seq_kernel(q_ref, yi_ref, g_ref, h_ref, s0_ref, y_o, sf_o, s_scr, *, nch, reverse):
    @pl.when(pl.program_id(1) == 0)
    def _():
        s_scr[...] = s0_ref[0]

    def body(cc, carry):
        c = (nch - 1 - cc) if reverse else cc
        rows = pl.ds(pl.multiple_of(c * CHUNK, CHUNK), CHUNK)
        for j in range(N_PAIRS):
            lanes = slice(j * PAIR, (j + 1) * PAIR)
            S = s_scr[j]
            y_o[0, rows, lanes] = _dot3(q_ref[0, rows, lanes], S) + yi_ref[0, rows, lanes]
            s_scr[j] = _dot3(g_ref[0, c * N_PAIRS + j], S) + h_ref[0, c * N_PAIRS + j]
        return carry

    lax.fori_loop(0, nch, body, 0)
    sf_o[0] = s_scr[...]


def rwkv_sequential(q1, yi, g, hm, s0, reverse):
    B, T, W = q1.shape
    tt = _tile(T, 8 * CHUNK)
    n = T // tt
    nch = tt // CHUNK
    step = (lambda i: n - 1 - i) if reverse else (lambda i: i)
    tok = pl.BlockSpec((1, tt, W), lambda bb, i: (bb, step(i), 0))
    mat = pl.BlockSpec((1, nch * N_PAIRS, PAIR, PAIR), lambda bb, i: (bb, step(i), 0, 0))
    sspec = pl.BlockSpec((1, N_PAIRS, PAIR, PAIR), lambda bb, i: (bb, 0, 0, 0))
    return pl.pallas_call(
        functools.partial(_rwkv_seq_kernel, nch=nch, reverse=reverse),
        grid=(B, n),
        in_specs=[tok, tok, mat, mat, sspec],
        out_specs=[tok, sspec],
        out_shape=[jax.ShapeDtypeStruct((B, T, W), F32), jax.ShapeDtypeStruct((B, N_PAIRS, PAIR, PAIR), F32)],
        scratch_shapes=[pltpu.VMEM((N_PAIRS, PAIR, PAIR), F32)],
        compiler_params=_cparams("parallel", "arbitrary"),
        name="rwkv_seq_bwd" if reverse else "rwkv_seq_fwd",
    )(q1, yi, g, hm, s0)


def _rwkv_readout_kernel(yf_ref, yb_ref, bonus_ref, gate_ref, gw_ref, gb_ref, e_ref, o_ref):
    E = e_ref[...]
    ys = yf_ref[0] + yb_ref[0]
    yc = ys - _dot_split(ys, E) * (1.0 / RW_HEAD)
    var = _dot_split(yc * yc, E) * (1.0 / RW_HEAD)
    yn = yc * lax.rsqrt(var + GN_EPS) * gw_ref[...] + gb_ref[...]
    o_ref[0] = (yn + bonus_ref[0]) * gate_ref[0]


def rwkv_readout(yf, yb, bonus, gate, p):
    B, T, W = yf.shape
    tm = _tile(T, 512)
    spec = pl.BlockSpec((1, tm, W), lambda b, i: (b, i, 0))
    full = lambda a: pl.BlockSpec(a.shape, lambda b, i: (0,) * a.ndim)
    return pl.pallas_call(
        _rwkv_readout_kernel,
        grid=(B, T // tm),
        in_specs=[spec] * 4 + [full(p["gn_w"]), full(p["gn_b"]), full(p["E"])],
        out_specs=spec,
        out_shape=jax.ShapeDtypeStruct((B, T, W), F32),
        compiler_params=_cparams("parallel", "parallel"),
        name="rwkv_readout",
    )(yf, yb, bonus, gate, p["gn_w"], p["gn_b"], p["E"])


def rwkv_branch(rw_lat, rw_ctx, p, emit_ctx):
    prep_l = rwkv_prep(rw_lat, p)
    prep_c = rwkv_prep(rw_ctx, p)
    B = rw_lat.shape[0]
    s_zero = jnp.zeros((B, N_PAIRS, PAIR, PAIR), F32)

    def scans(prep, s0s):
        r, v, kk = prep[:3]
        ys, finals = [], []
        for d in range(2):
            ld, b, kr = prep[3 + 3 * d: 6 + 3 * d]
            q1, yi, g, hm = rwkv_chunks(ld, r, v, kk, b, kr, reverse=bool(d))
            y, sf = rwkv_sequential(q1, yi, g, hm, s0s[d], reverse=bool(d))
            ys.append(y)
            finals.append(sf)
        return ys, finals

    ys_c, fin_c = scans(prep_c, (s_zero, s_zero))
    ys_l, _ = scans(prep_l, fin_c)
    out_l = rwkv_readout(ys_l[0], ys_l[1], prep_l[9], prep_l[10], p)
    out_c = rwkv_readout(ys_c[0], ys_c[1], prep_c[9], prep_c[10], p) if emit_ctx else None
    return out_l, out_c


def _dft_mats(n):
    a = 2.0 * np.pi * np.outer(np.arange(n), np.arange(n)) / n
    return np.cos(a), np.sin(a)


def _fft1_kernel(u_ref, c_ref, s_ref, twc_ref, tws_ref, ar_o, ai_o, *, tn2, ch):
    U = u_ref[0]
    Ar = _dot3(c_ref[...], U)
    Ai = -_dot3(s_ref[...], U)
    twc = twc_ref[0]
    tws = tws_ref[0]
    for j in range(tn2):
        ct = twc[:, j:j + 1]
        st = tws[:, j:j + 1]
        a_r = Ar[:, j * ch:(j + 1) * ch]
        a_i = Ai[:, j * ch:(j + 1) * ch]
        ar_o[0, j] = a_r * ct + a_i * st
        ai_o[0, j] = a_i * ct - a_r * st


def _fft2_kernel(ar_ref, ai_ref, c_ref, s_ref, cc_ref, sc_ref, o_ref):
    Ar = ar_ref[0]
    Ai = ai_ref[0]
    C = c_ref[...]
    S = s_ref[...]
    Yr = _dot3(C, Ar) + _dot3(S, Ai)
    Yi = _dot3(C, Ai) - _dot3(S, Ar)
    Cc = cc_ref[...]
    Sc = sc_ref[...]
    for m in range(Ar.shape[1] // LANES):
        sl = slice(m * LANES, (m + 1) * LANES)
        o_ref[0, :, sl] = _dot3(Yr[:, sl], Cc) + _dot3(Yi[:, sl], Sc)


def fourier_mixer(u):
    B, T, ch = u.shape
    lg = int(round(math.log2(T)))
    assert 1 << lg == T
    N1 = 1 << ((lg + 1) // 2)
    N2 = T // N1
    c1, s1 = _dft_mats(N1)
    c2, s2 = _dft_mats(N2)
    tw = 2.0 * np.pi * np.outer(np.arange(N1), np.arange(N2)) / T
    tn2 = min(SUBLANES, N2)
    nj = N2 // tn2
    twc = np.cos(tw).reshape(N1, nj, tn2).transpose(1, 0, 2)
    tws = np.sin(tw).reshape(N1, nj, tn2).transpose(1, 0, 2)
    cg, sg = _dft_mats(FT_GROUP)
    scale = 1.0 / math.sqrt(T * FT_GROUP)
    eye2 = np.eye(LANES // FT_GROUP)
    cc = np.kron(eye2, cg) * scale
    sc = np.kron(eye2, sg) * scale
    f = lambda a: jnp.asarray(a, F32)
    full2 = lambda n, m: pl.BlockSpec((n, m), lambda b, j: (0, 0))

    ar, ai = pl.pallas_call(
        functools.partial(_fft1_kernel, tn2=tn2, ch=ch),
        grid=(B, nj),
        in_specs=[pl.BlockSpec((1, N1, tn2 * ch), lambda b, j: (b, 0, j)),
                  full2(N1, N1), full2(N1, N1),
                  pl.BlockSpec((1, N1, tn2), lambda b, j: (j, 0, 0)),
                  pl.BlockSpec((1, N1, tn2), lambda b, j: (j, 0, 0))],
        out_specs=[pl.BlockSpec((1, tn2, N1, ch), lambda b, j: (b, j, 0, 0))] * 2,
        out_shape=[jax.ShapeDtypeStruct((B, N2, N1, ch), F32)] * 2,
        compiler_params=_cparams("parallel", "parallel"),
        name="fft_stage1",
    )(u.reshape(B, N1, N2 * ch), f(c1), f(s1), f(twc), f(tws))

    tk1 = min(SUBLANES, N1)
    blk = pl.BlockSpec((1, N2, tk1 * ch), lambda b, j: (b, 0, j))
    out = pl.pallas_call(
        _fft2_kernel,
        grid=(B, N1 // tk1),
        in_specs=[blk, blk, full2(N2, N2), full2(N2, N2), full2(LANES, LANES), full2(LANES, LANES)],
        out_specs=blk,
        out_shape=jax.ShapeDtypeStruct((B, N2, N1 * ch), F32),
        compiler_params=_cparams("parallel", "parallel"),
        name="fft_stage2",
    )(ar.reshape(B, N2, N1 * ch), ai.reshape(B, N2, N1 * ch), f(c2), f(s2), f(cc), f(sc))
    return out.reshape(B, T, ch)


HEAD_SLAB = LANES
ROPE_SHIFT = HEAD_SLAB - QK_ROPE
ONES_ROWS = 16
V_ROWS = V_HEAD + ONES_ROWS
ATTN_UNROLL = 6


def _rms(x, w):
    return x * lax.rsqrt(jnp.mean(x * x, axis=-1, keepdims=True) + RMS_EPS) * w


def _qproj_kernel(cq_ref, nw_ref, w_ref, ct_ref, st_ref, q_o):
    q = _dot(_rms(cq_ref[0], nw_ref[...]).astype(BF16), w_ref[...])
    ct = ct_ref[...]
    st = st_ref[...]
    for h in range(MLA_HEADS):
        sl = slice(h * HEAD_SLAB, (h + 1) * HEAD_SLAB)
        s = q[:, sl]
        q_o[0, :, sl] = ((s * ct + pltpu.roll(s, ROPE_SHIFT, 1) * st) * Q_SCALE).astype(BF16)


def _kvproj_kernel(ckv_ref, nw_ref, wk_ref, wvt_ref, ct_ref, st_ref, k_o, vt_o):
    x = ckv_ref[0]
    n = _rms(x[:, :KV_RANK], nw_ref[...]).astype(BF16)
    rs = x[:, KV_RANK:KV_RANK + HEAD_SLAB]
    rope = rs * ct_ref[...] + pltpu.roll(rs, ROPE_SHIFT, 1) * st_ref[...]
    kn = _dot(n, wk_ref[...])
    for h in range(MLA_HEADS):
        sl = slice(h * HEAD_SLAB, (h + 1) * HEAD_SLAB)
        k_o[0, :, sl] = (kn[:, sl] + rope).astype(BF16)
    vt = lax.dot_general(wvt_ref[...], n, NT, preferred_element_type=F32)
    row = lax.broadcasted_iota(jnp.int32, vt.shape, 0)
    vt_o[0] = jnp.where(row % V_ROWS >= V_HEAD, 1.0, vt).astype(BF16)


def q_projection(cq, nw, w, ct, st):
    B, T, R = cq.shape
    tm = _tile(T, 256)
    W = MLA_HEADS * HEAD_SLAB
    tab = pl.BlockSpec((tm, HEAD_SLAB), lambda b, i: (i, 0))
    return pl.pallas_call(
        _qproj_kernel,
        grid=(B, T // tm),
        in_specs=[pl.BlockSpec((1, tm, R), lambda b, i: (b, i, 0)),
                  pl.BlockSpec(nw.shape, lambda b, i: (0, 0)),
                  pl.BlockSpec(w.shape, lambda b, i: (0, 0)), tab, tab],
        out_specs=pl.BlockSpec((1, tm, W), lambda b, i: (b, i, 0)),
        out_shape=jax.ShapeDtypeStruct((B, T, W), BF16),
        compiler_params=_cparams("parallel", "parallel"),
        name="q_projection",
    )(cq, nw, w, ct, st)


def kv_projection(ckv, nw, wk, wvt, ct, st):
    B, T, R = ckv.shape
    tm = _tile(T, 256)
    W = MLA_HEADS * HEAD_SLAB
    tab = pl.BlockSpec((tm, HEAD_SLAB), lambda b, i: (i, 0))
    return pl.pallas_call(
        _kvproj_kernel,
        grid=(B, T // tm),
        in_specs=[pl.BlockSpec((1, tm, R), lambda b, i: (b, i, 0)),
                  pl.BlockSpec(nw.shape, lambda b, i: (0, 0)),
                  pl.BlockSpec(wk.shape, lambda b, i: (0, 0)),
                  pl.BlockSpec(wvt.shape, lambda b, i: (0, 0)), tab, tab],
        out_specs=[pl.BlockSpec((1, tm, W), lambda b, i: (b, i, 0)),
                   pl.BlockSpec((1, MLA_HEADS * V_ROWS, tm), lambda b, i: (b, 0, i))],
        out_shape=[jax.ShapeDtypeStruct((B, T, W), BF16), jax.ShapeDtypeStruct((B, MLA_HEADS * V_ROWS, T), BF16)],
        compiler_params=_cparams("parallel", "parallel"),
        name="kv_projection",
    )(ckv, nw, wk, wvt, ct, st)


def _attn_kernel(q_ref, k_ref, vt_ref, o_ref, m_scr, acc_scr, sa_scr, sb_scr, ma_scr, mb_scr, *, tkc, nkc):
    m_scr[...] = jnp.full(m_scr.shape, -jnp.inf, F32)
    acc_scr[...] = jnp.zeros(acc_scr.shape, F32)
    buf_a = (sa_scr, ma_scr)
    buf_b = (sb_scr, mb_scr)

    def chunk(c):
        return pl.ds(pl.multiple_of(c * tkc, tkc), tkc)

    def scores(c, dst, dmax):
        for hh in range(2):
            q = q_ref[0, :, hh * HEAD_SLAB:(hh + 1) * HEAD_SLAB]
            kc = k_ref[0, chunk(c), hh * HEAD_SLAB:(hh + 1) * HEAD_SLAB]
            s = lax.dot_general(kc, q, NT, preferred_element_type=F32)
            dst[hh] = s
            dmax[hh] = jnp.max(s, axis=0, keepdims=True)

    def consume(src, smax, c):
        for hh in range(2):
            m_old = m_scr[hh]
            m_new = jnp.maximum(m_old, smax[hh])
            pr = jnp.exp2(src[hh] - m_new).astype(BF16)
            alpha = jnp.exp2(m_old - m_new)
            m_scr[hh] = m_new
            rows = slice(hh * V_ROWS, (hh + 1) * V_ROWS)
            acc_scr[rows, :] = alpha * acc_scr[rows, :] + _dot(vt_ref[0, rows, chunk(c)], pr)

    scores(0, *buf_a)
    unroll = ATTN_UNROLL if nkc > 2 * ATTN_UNROLL else 1
    ngroups = (nkc - 1) // (2 * unroll)

    def body(i, carry):
        c = 2 * unroll * i
        for _ in range(unroll):
            scores(c + 1, *buf_b)
            consume(*buf_a, c)
            scores(c + 2, *buf_a)
            consume(*buf_b, c + 1)
            c = c + 2
        return carry

    lax.fori_loop(0, ngroups, body, 0)
    done = 2 * unroll * ngroups
    for cc in range(done, nkc):
        src, dst = (buf_a, buf_b) if (cc - done) % 2 == 0 else (buf_b, buf_a)
        if cc + 1 < nkc:
            scores(cc + 1, *dst)
        consume(*src, cc)
    outs = []
    for hh in range(2):
        base = hh * V_ROWS
        outs.append(acc_scr[base:base + V_HEAD, :] / acc_scr[base + V_HEAD:base + V_HEAD + 1, :])
    o_ref[0] = jnp.concatenate(outs, axis=0).T.astype(o_ref.dtype)


def attention(q, k, vt):
    B, T, _ = q.shape
    Tk = k.shape[1]
    tq = _tile(T, 256)
    tkc = next(c for c in (640, 512, 256, 128) if Tk % c == 0)
    hp = MLA_HEADS // 2
    return pl.pallas_call(
        functools.partial(_attn_kernel, tkc=tkc, nkc=Tk // tkc),
        grid=(B, hp, T // tq),
        in_specs=[pl.BlockSpec((1, tq, 2 * HEAD_SLAB), lambda b, h, i: (b, i, h)),
                  pl.BlockSpec((1, Tk, 2 * HEAD_SLAB), lambda b, h, i: (b, 0, h)),
                  pl.BlockSpec((1, 2 * V_ROWS, Tk), lambda b, h, i: (b, h, 0))],
        out_specs=pl.BlockSpec((1, tq, 2 * V_HEAD), lambda b, h, i: (b, i, h)),
        out_shape=jax.ShapeDtypeStruct((B, T, MLA_WIDTH), BF16),
        scratch_shapes=[pltpu.VMEM((2, 1, tq), F32), pltpu.VMEM((2 * V_ROWS, tq), F32),
                        pltpu.VMEM((2, tkc, tq), F32), pltpu.VMEM((2, tkc, tq), F32),
                        pltpu.VMEM((2, 1, tq), F32), pltpu.VMEM((2, 1, tq), F32)],
        compiler_params=_cparams("parallel", "parallel", "arbitrary"),
        name="mla_attention",
    )(q, k, vt)


def _outproj_kernel(rw_ref, ft_ref, att_ref, x_ref, g_ref, lw_ref, lb_ref, w_ref, o_ref, *, alpha):
    w = w_ref
    mix = (_dot(rw_ref[0].astype(BF16), w[0:RW_WIDTH, :])
           + _dot(ft_ref[0].astype(BF16), w[RW_WIDTH:RW_WIDTH + FT_WIDTH, :])
           + _dot(att_ref[0], w[RW_WIDTH + FT_WIDTH:, :]))
    z = alpha * x_ref[0] + g_ref[0] * mix
    o_ref[0] = _standardize(z, LN_EPS) * lw_ref[...] + lb_ref[...]


def out_projection_ln(rw, ft, att, x, gate, ln_w, ln_b, w_out, alpha):
    B, T, D = x.shape
    tm = _tile(T, 256)
    tok = lambda n: pl.BlockSpec((1, tm, n), lambda b, i: (b, i, 0))
    row = pl.BlockSpec((1, D), lambda b, i: (0, 0))
    return pl.pallas_call(
        functools.partial(_outproj_kernel, alpha=alpha),
        grid=(B, T // tm),
        in_specs=[tok(RW_WIDTH), tok(FT_WIDTH), tok(MLA_WIDTH), tok(D),
                  pl.BlockSpec((1, 1, D), lambda b, i: (b, 0, 0)), row, row,
                  pl.BlockSpec(w_out.shape, lambda b, i: (0, 0))],
        out_specs=tok(D),
        out_shape=jax.ShapeDtypeStruct((B, T, D), F32),
        compiler_params=_cparams("parallel", "parallel"),
        name="out_projection_ln",
    )(rw, ft, att, x, gate, ln_w, ln_b, w_out)


def _ffn_kernel(x_ref, sh_ref, sc_ref, g_ref, lw_ref, lb_ref, w1_ref, w3_ref, w2_ref, o_ref, h_scr, *, alpha):
    f = pl.program_id(2)

    @pl.when(f == 0)
    def _():
        h = _standardize(x_ref[0], MOD_EPS) * (1.0 + sc_ref[0]) + sh_ref[0]
        h_scr[...] = h.astype(BF16)
        o_ref[0] = jnp.zeros(o_ref.shape[1:], F32)

    hb = h_scr[...]
    a = _dot(hb, w1_ref[...])
    b = _dot(hb, w3_ref[...])
    o_ref[0] += _dot((a * jax.nn.sigmoid(a) * b).astype(BF16), w2_ref[...])

    @pl.when(f == pl.num_programs(2) - 1)
    def _():
        z = alpha * x_ref[0] + g_ref[0] * o_ref[0]
        o_ref[0] = _standardize(z, LN_EPS) * lw_ref[...] + lb_ref[...]


def ffn_ln(x, shift, scale, gate, ln_w, ln_b, w1, w3, w2, alpha):
    B, T, D = x.shape
    F = w1.shape[1]
    tm = _tile(T, 1024)
    tf = _tile(F, 256)
    tok = pl.BlockSpec((1, tm, D), lambda b, i, f: (b, i, 0))
    vec = pl.BlockSpec((1, 1, D), lambda b, i, f: (b, 0, 0))
    row = pl.BlockSpec((1, D), lambda b, i, f: (0, 0))
    return pl.pallas_call(
        functools.partial(_ffn_kernel, alpha=alpha),
        grid=(B, T // tm, F // tf),
        in_specs=[tok, vec, vec, vec, row, row,
                  pl.BlockSpec((D, tf), lambda b, i, f: (0, f)),
                  pl.BlockSpec((D, tf), lambda b, i, f: (0, f)),
                  pl.BlockSpec((tf, D), lambda b, i, f: (f, 0))],
        out_specs=tok,
        out_shape=jax.ShapeDtypeStruct((B, T, D), F32),
        scratch_shapes=[pltpu.VMEM((tm, D), BF16)],
        compiler_params=_cparams("parallel", "parallel", "arbitrary"),
        name="ffn_ln",
    )(x, shift, scale, gate, ln_w, ln_b, w1, w3, w2)


def _router_kernel(x_ref, sh_ref, sc_ref, wr_ref, h_o, ti_o, tg_o):
    h = _standardize(x_ref[0], MOD_EPS) * (1.0 + sc_ref[0]) + sh_ref[0]
    h_o[0] = h.astype(BF16)
    logits = _dot(h, wr_ref[...], HI)
    lane = lax.broadcasted_iota(jnp.int32, logits.shape, 1)
    neg = jnp.float32(-jnp.inf)
    logits = jnp.where(lane < N_EXPERTS, logits, neg)
    m1 = jnp.max(logits, axis=-1, keepdims=True)
    i1 = jnp.min(jnp.where(logits == m1, lane, LANES), axis=-1, keepdims=True)
    rest = jnp.where(lane == i1, neg, logits)
    m2 = jnp.max(rest, axis=-1, keepdims=True)
    i2 = jnp.min(jnp.where(rest == m2, lane, LANES), axis=-1, keepdims=True)
    e = jnp.exp(m2 - m1)
    g1 = 1.0 / (1.0 + e)
    g2 = e / (1.0 + e)
    ti_o[0] = jnp.where(lane == 0, i1, jnp.where(lane == 1, i2, 0))
    tg_o[0] = jnp.where(lane == 0, g1, jnp.where(lane == 1, g2, 0.0))


def moe_router(x, shift, scale, wr):
    B, T, D = x.shape
    tm = _tile(T, 512)
    tok = lambda n: pl.BlockSpec((1, tm, n), lambda b, i: (b, i, 0))
    vec = pl.BlockSpec((1, 1, D), lambda b, i: (b, 0, 0))
    return pl.pallas_call(
        _router_kernel,
        grid=(B, T // tm),
        in_specs=[tok(D), vec, vec, pl.BlockSpec(wr.shape, lambda b, i: (0, 0))],
        out_specs=[tok(D), tok(LANES), tok(LANES)],
        out_shape=[jax.ShapeDtypeStruct((B, T, D), BF16), jax.ShapeDtypeStruct((B, T, LANES), jnp.int32),
                   jax.ShapeDtypeStruct((B, T, LANES), F32)],
        compiler_params=_cparams("parallel", "parallel"),
        name="moe_router",
    )(x, shift, scale, wr)


def _expert_kernel(be_ref, bv_ref, x_ref, w1_ref, w3_ref, w2_ref, o_ref):
    blk = pl.program_id(0)
    f = pl.program_id(1)
    valid = bv_ref[blk] > 0

    @pl.when(f == 0)
    def _():
        o_ref[...] = jnp.zeros(o_ref.shape, F32)

    @pl.when(valid)
    def _():
        xb = x_ref[...]
        a = _dot(xb, w1_ref[0].astype(BF16))
        b = _dot(xb, w3_ref[0].astype(BF16))
        o_ref[...] += _dot((a * jax.nn.sigmoid(a) * b).astype(BF16), w2_ref[0].astype(BF16))


def expert_ffn(xg, blk_expert, blk_valid, w1, w3, w2):
    P, D = xg.shape
    F = w1.shape[2]
    tf = _tile(F, 512)
    nblk = P // MOE_ROWS
    grid_spec = pltpu.PrefetchScalarGridSpec(
        num_scalar_prefetch=2,
        grid=(nblk, F // tf),
        in_specs=[pl.BlockSpec((MOE_ROWS, D), lambda i, f, be, bv: (i, 0)),
                  pl.BlockSpec((1, D, tf), lambda i, f, be, bv: (be[i], 0, jnp.where(bv[i] > 0, f, 0))),
                  pl.BlockSpec((1, D, tf), lambda i, f, be, bv: (be[i], 0, jnp.where(bv[i] > 0, f, 0))),
                  pl.BlockSpec((1, tf, D), lambda i, f, be, bv: (be[i], jnp.where(bv[i] > 0, f, 0), 0))],
        out_specs=pl.BlockSpec((MOE_ROWS, D), lambda i, f, be, bv: (i, 0)),
    )
    return pl.pallas_call(
        _expert_kernel,
        grid_spec=grid_spec,
        out_shape=jax.ShapeDtypeStruct((P, D), F32),
        compiler_params=_cparams("parallel", "arbitrary"),
        name="expert_ffn",
    )(blk_expert, blk_valid, xg, w1, w3, w2)


def _combine_ln_kernel(x_ref, y0_ref, y1_ref, tg_ref, g_ref, lw_ref, lb_ref, o_ref, *, alpha):
    tg = tg_ref[0]
    y = tg[:, 0:1] * y0_ref[0] + tg[:, 1:2] * y1_ref[0]
    z = alpha * x_ref[0] + g_ref[0] * y
    o_ref[0] = _standardize(z, LN_EPS) * lw_ref[...] + lb_ref[...]


def combine_ln(x, y0, y1, tg, gate, ln_w, ln_b, alpha):
    B, T, D = x.shape
    tm = _tile(T, 512)
    tok = pl.BlockSpec((1, tm, D), lambda b, i: (b, i, 0))
    row = pl.BlockSpec((1, D), lambda b, i: (0, 0))
    return pl.pallas_call(
        functools.partial(_combine_ln_kernel, alpha=alpha),
        grid=(B, T // tm),
        in_specs=[tok, tok, tok, pl.BlockSpec((1, tm, LANES), lambda b, i: (b, i, 0)),
                  pl.BlockSpec((1, 1, D), lambda b, i: (b, 0, 0)), row, row],
        out_specs=tok,
        out_shape=jax.ShapeDtypeStruct((B, T, D), F32),
        compiler_params=_cparams("parallel", "parallel"),
        name="combine_ln",
    )(x, y0, y1, tg, gate, ln_w, ln_b)


def moe_ln(x, shift, scale, gate, ln_w, ln_b, router, w1, w3, w2, alpha):
    B, T, D = x.shape
    N = B * T
    E = router.shape[1]
    wr = jnp.zeros((D, LANES), F32).at[:, :E].set(router)
    h, ti, tg = moe_router(x, shift, scale, wr)
    top_i = ti.reshape(N, LANES)[:, :2]
    e_flat = top_i.reshape(-1)
    onehot = (e_flat[:, None] == jnp.arange(E, dtype=jnp.int32)[None, :]).astype(jnp.int32)
    rank = jnp.sum((jnp.cumsum(onehot, axis=0) - onehot) * onehot, axis=1)
    counts = jnp.sum(onehot, axis=0)
    padded = (counts + MOE_ROWS - 1) // MOE_ROWS * MOE_ROWS
    p_ends = jnp.cumsum(padded)
    p_starts = p_ends - padded
    dest = p_starts[e_flat] + rank
    P = -(-(2 * N) // MOE_ROWS) * MOE_ROWS + E * MOE_ROWS
    nblk = P // MOE_ROWS
    tok_flat = jnp.repeat(jnp.arange(N, dtype=jnp.int32), 2)
    slot_tok = jnp.zeros((P,), jnp.int32).at[dest].set(tok_flat)
    blk_start = jnp.arange(nblk, dtype=jnp.int32) * MOE_ROWS
    blk_expert = jnp.minimum(jnp.searchsorted(p_ends, blk_start, side="right"), E - 1).astype(jnp.int32)
    blk_valid = (blk_start < p_ends[-1]).astype(jnp.int32)
    xg = h.reshape(N, D)[slot_tok]
    yg = expert_ffn(xg, blk_expert, blk_valid, w1, w3, w2)
    d2 = dest.reshape(N, 2)
    y0 = yg[d2[:, 0]].reshape(B, T, D)
    y1 = yg[d2[:, 1]].reshape(B, T, D)
    return combine_ln(x, y0, y1, tg, gate, ln_w, ln_b, alpha)


def _rope_partner(w):
    half = ROPE_AXIS // 2
    idx = np.arange(QK_ROPE)
    first = (idx % ROPE_AXIS) < half
    src = np.where(first, idx + half, idx - half)
    sign = np.where(first, -1.0, 1.0).astype(np.float32)
    return w[:, src] * sign


def _layer_params(l, w_in, rw_conv, rw_w0, rw_w_up, rw_a0, rw_a_up, rw_g_up, rw_k_k, rw_k_a, rw_r_k, rw_gn_w,
                  rw_gn_b, mla_q_norm, mla_w_uq, mla_kv_norm, mla_w_ukv, w_out):
    D = w_in.shape[1]
    wi = w_in[l]
    o = np.cumsum([0, 3 * RW_WIDTH, G_RANK, LORA_RANK, LORA_RANK, LORA_RANK, LORA_RANK, FT_WIDTH, Q_RANK, KV_RANK, QK_ROPE])
    piece = lambda i: wi[:, o[i]:o[i + 1]]
    zpad = lambda n: jnp.zeros((D, n), F32)
    lora = [jnp.concatenate([piece(i), zpad(LANES - LORA_RANK)], axis=1) for i in (2, 3, 4, 5)]
    w_rw = jnp.concatenate([piece(0), piece(1)] + lora, axis=1)
    kr = piece(9)
    w_ckv = jnp.concatenate([piece(8), zpad(QK_NOPE), kr, _rope_partner(kr)], axis=1)
    pad_rows = lambda a: jnp.concatenate([a, jnp.zeros((a.shape[0], LANES - LORA_RANK, a.shape[2]), F32)], axis=1)
    head = jnp.arange(RW_WIDTH) // RW_HEAD
    uq = mla_w_uq[l].reshape(Q_RANK, MLA_HEADS, QK_NOPE + QK_ROPE)
    uq_rope = uq[:, :, QK_NOPE:]
    uq_partner = _rope_partner(uq_rope.reshape(Q_RANK * MLA_HEADS, QK_ROPE)).reshape(Q_RANK, MLA_HEADS, QK_ROPE)
    w_q = jnp.concatenate([uq, uq_partner], axis=2).reshape(Q_RANK, MLA_HEADS * HEAD_SLAB)
    ukv = mla_w_ukv[l].reshape(KV_RANK, MLA_HEADS, QK_NOPE + V_HEAD)
    w_k = jnp.concatenate([ukv[:, :, :QK_NOPE], jnp.zeros((KV_RANK, MLA_HEADS, HEAD_SLAB - QK_NOPE), F32)],
                          axis=2).reshape(KV_RANK, MLA_HEADS * HEAD_SLAB)
    w_v = jnp.concatenate([ukv[:, :, QK_NOPE:], jnp.zeros((KV_RANK, MLA_HEADS, ONES_ROWS), F32)],
                          axis=2).reshape(KV_RANK, MLA_HEADS * V_ROWS)
    return {
        "w_in": [w.astype(BF16) for w in (w_rw, piece(6), piece(7), w_ckv)],
        "rw": {"conv": rw_conv[l], "k_k": rw_k_k[l][None], "k_a": rw_k_a[l][None], "r_k": rw_r_k[l].reshape(1, RW_WIDTH),
               "w0": rw_w0[l], "a0": rw_a0[l], "w_up": pad_rows(rw_w_up[l]).astype(BF16), "a_up": pad_rows(rw_a_up[l]).astype(BF16),
               "g_up": rw_g_up[l].astype(BF16), "gn_w": rw_gn_w[l][None], "gn_b": rw_gn_b[l][None],
               "E": (head[:, None] == head[None, :]).astype(BF16)},
        "q_norm": mla_q_norm[l][None], "w_q": w_q.astype(BF16),
        "kv_norm": mla_kv_norm[l][None], "w_k": w_k.astype(BF16), "w_vt": w_v.T.astype(BF16),
        "w_out": w_out[l].astype(BF16),
    }


def _rope_tables(T, use_rope):
    ones = jnp.ones((T, QK_NOPE), F32)
    zeros = jnp.zeros((T, QK_NOPE), F32)
    zpad = jnp.zeros((T, HEAD_SLAB - QK_NOPE - QK_ROPE), F32)
    if use_rope:
        row = jnp.repeat(jnp.arange(T // GRID_W), GRID_W).astype(F32)
        col = (jnp.arange(T) % GRID_W).astype(F32)
        inv = ROPE_THETA ** (-jnp.arange(0, ROPE_AXIS, 2, dtype=F32) / ROPE_AXIS)
        ang = jnp.stack([row[:, None] * inv, col[:, None] * inv], axis=1)
        ang = jnp.broadcast_to(ang[:, :, None, :], (T, 2, 2, ROPE_AXIS // 2)).reshape(T, QK_ROPE)
        cos, sin = jnp.cos(ang), jnp.sin(ang)
    else:
        cos, sin = jnp.ones((T, QK_ROPE), F32), jnp.zeros((T, QK_ROPE), F32)
    return jnp.concatenate([ones, cos, zpad], axis=1), jnp.concatenate([zeros, sin, zpad], axis=1)


def _mixer(h_pieces_lat, h_pieces_ctx, p, tabs_lat, tabs_ctx, emit_ctx):
    rw_l, ft_l, cq_l, ckv_l = h_pieces_lat
    rw_c, ft_c, cq_c, ckv_c = h_pieces_ctx
    rwo_l, rwo_c = rwkv_branch(rw_l, rw_c, p["rw"], emit_ctx)
    fto_l = fourier_mixer(ft_l)
    q_l = q_projection(cq_l, p["q_norm"], p["w_q"], *tabs_lat)
    k_l, vt_l = kv_projection(ckv_l, p["kv_norm"], p["w_k"], p["w_vt"], *tabs_lat)
    k_c, vt_c = kv_projection(ckv_c, p["kv_norm"], p["w_k"], p["w_vt"], *tabs_ctx)
    att_l = attention(q_l, jnp.concatenate([k_l, k_c], axis=1), jnp.concatenate([vt_l, vt_c], axis=2))
    out_c = None
    if emit_ctx:
        fto_c = fourier_mixer(ft_c)
        q_c = q_projection(cq_c, p["q_norm"], p["w_q"], *tabs_ctx)
        att_c = attention(q_c, k_c, vt_c)
        out_c = (rwo_c, fto_c, att_c)
    return (rwo_l, fto_l, att_l), out_c


def kernel(x, c, ctx, c_ctx, ada_w, ada_b, w_in, rw_conv, rw_w0, rw_w_up, rw_a0, rw_a_up, rw_g_up, rw_k_k, rw_k_a,
           rw_r_k, rw_gn_w, rw_gn_b, mla_q_norm, mla_w_uq, mla_kv_norm, mla_w_ukv, w_out, ln1_w, ln1_b, ln2_w, ln2_b,
           ffn_w1, ffn_w3, ffn_w2, moe_router, moe_w1, moe_w3, moe_w2):
    B, T, D = x.shape
    Tc = ctx.shape[1]
    depth = w_in.shape[0]
    alpha = (2 * depth) ** 0.25
    assert B + 1 <= SUBLANES
    cc = jnp.zeros((SUBLANES, D), F32).at[:B].set(c).at[B].set(c_ctx)
    ada = ada_vectors(cc, ada_w, ada_b)
    tabs_lat = _rope_tables(T, True)
    tabs_ctx = _rope_tables(Tc, False)
    for l in range(depth):
        last = l == depth - 1
        p = _layer_params(l, w_in, rw_conv, rw_w0, rw_w_up, rw_a0, rw_a_up, rw_g_up, rw_k_k, rw_k_a, rw_r_k,
                          rw_gn_w, rw_gn_b, mla_q_norm, mla_w_uq, mla_kv_norm, mla_w_ukv, w_out)
        mods = ada[l].reshape(SUBLANES, 6, D)
        lat = [mods[:B, j][:, None, :] for j in range(6)]
        cx = [jnp.broadcast_to(mods[B, j][None, None, :], (B, 1, D)) for j in range(6)]
        sh_m, sc_m, g_m, sh_f, sc_f, g_f = lat
        csh_m, csc_m, cg_m, csh_f, csc_f, cg_f = cx
        ln1 = (ln1_w[l][None], ln1_b[l][None])
        ln2 = (ln2_w[l][None], ln2_b[l][None])

        pieces_l = in_projection(x, sh_m, sc_m, p["w_in"])
        pieces_c = in_projection(ctx, csh_m, csc_m, p["w_in"])
        mix_l, mix_c = _mixer(pieces_l, pieces_c, p, tabs_lat, tabs_ctx, not last)
        x = out_projection_ln(*mix_l, x, g_m, *ln1, p["w_out"], alpha)
        i = l // 2
        if l % 2 == 0:
            dense = (ffn_w1[i].astype(BF16), ffn_w3[i].astype(BF16), ffn_w2[i].astype(BF16))
            x = ffn_ln(x, sh_f, sc_f, g_f, *ln2, *dense, alpha)
        else:
            x = moe_ln(x, sh_f, sc_f, g_f, *ln2, moe_router[i], moe_w1[i], moe_w3[i], moe_w2[i], alpha)
        if not last:
            ctx = out_projection_ln(*mix_c, ctx, cg_m, *ln1, p["w_out"], alpha)
            if l % 2 == 0:
                ctx = ffn_ln(ctx, csh_f, csc_f, cg_f, *ln2, *dense, alpha)
            else:
                ctx = moe_ln(ctx, csh_f, csc_f, cg_f, *ln2, moe_router[i], moe_w1[i], moe_w3[i], moe_w2[i], alpha)
    return x
```

```python
import functools
import math

import numpy as np
import jax
import jax.numpy as jnp
from jax import lax
from jax.experimental import pallas as pl
from jax.experimental.pallas import tpu as pltpu

F32 = jnp.float32
BF16 = jnp.bfloat16
HI = lax.Precision.HIGHEST

LANES = 128
SUBLANES = 8
VMEM_LIMIT = 56 * 1024 * 1024

GRID_W = 64
RW_HEADS = 8
RW_HEAD = 64
RW_WIDTH = RW_HEADS * RW_HEAD
G_RANK = 128
LORA_RANK = 64
GN_EPS = 64e-5
FT_GROUP = 64
FT_WIDTH = 512
MLA_HEADS = 16
QK_NOPE = 64
QK_ROPE = 32
V_HEAD = 64
MLA_WIDTH = MLA_HEADS * V_HEAD
Q_RANK = 512
KV_RANK = 256
ROPE_AXIS = QK_ROPE // 2
ROPE_THETA = 10000.0
ATTN_SCALE = (QK_NOPE + QK_ROPE) ** -0.5
Q_SCALE = ATTN_SCALE * math.log2(math.e)
N_EXPERTS = 8
LN_EPS = 1e-5
MOD_EPS = 1e-6
RMS_EPS = 1e-6
CHUNK = 64
INV_BLOCK = 16
MOE_ROWS = 1024

NT = (((1,), (1,)), ((), ()))
TN = (((0,), (0,)), ((), ()))


def _cparams(*sem):
    return pltpu.CompilerParams(dimension_semantics=sem, vmem_limit_bytes=VMEM_LIMIT)


def _dot(a, b, prec=None):
    return jnp.dot(a, b, precision=prec, preferred_element_type=F32)


def _dot_split(x, w):
    hi = x.astype(BF16)
    r1 = x - hi.astype(F32)
    mid = r1.astype(BF16)
    lo = (r1 - mid.astype(F32)).astype(BF16)
    return _dot(hi, w) + _dot(mid, w) + _dot(lo, w)


def _dot3(a, b):
    a_hi = a.astype(BF16)
    b_hi = b.astype(BF16)
    a_lo = (a - a_hi.astype(F32)).astype(BF16)
    b_lo = (b - b_hi.astype(F32)).astype(BF16)
    return _dot(a_hi, b_hi) + _dot(a_hi, b_lo) + _dot(a_lo, b_hi)


def _standardize(x, eps):
    mu = jnp.mean(x, axis=-1, keepdims=True)
    xc = x - mu
    var = jnp.mean(xc * xc, axis=-1, keepdims=True)
    return xc * lax.rsqrt(var + eps)


def _tile(n, pref):
    t = min(n, pref)
    assert n % t == 0, (n, pref)
    return t


def _ada_kernel(c_ref, w_ref, b_ref, o_ref):
    c = c_ref[...]
    s = c * jax.nn.sigmoid(c)
    o_ref[0] = _dot(s, w_ref[0], HI) + b_ref[0]


def ada_vectors(cc, ada_w, ada_b):
    L, D, N6 = ada_w.shape
    tn = _tile(N6, 1024)
    return pl.pallas_call(
        _ada_kernel,
        grid=(L, N6 // tn),
        in_specs=[pl.BlockSpec((SUBLANES, D), lambda l, j: (0, 0)),
                  pl.BlockSpec((1, D, tn), lambda l, j: (l, 0, j)),
                  pl.BlockSpec((1, 1, tn), lambda l, j: (l, 0, j))],
        out_specs=pl.BlockSpec((1, SUBLANES, tn), lambda l, j: (l, 0, j)),
        out_shape=jax.ShapeDtypeStruct((L, SUBLANES, N6), F32),
        compiler_params=_cparams("parallel", "parallel"),
        name="ada_vectors",
    )(cc, ada_w, ada_b.reshape(L, 1, N6))


def _inproj_kernel(x_ref, sh_ref, sc_ref, *refs):
    nw = len(refs) // 2
    h = _standardize(x_ref[0], MOD_EPS) * (1.0 + sc_ref[0]) + sh_ref[0]
    hb = h.astype(BF16)
    for w_ref, o_ref in zip(refs[:nw], refs[nw:]):
        o_ref[0] = _dot(hb, w_ref[...])


def in_projection(x, shift, scale, weights):
    B, T, D = x.shape
    tm = _tile(T, 256)
    vec = pl.BlockSpec((1, 1, D), lambda b, i: (b, 0, 0))
    return pl.pallas_call(
        _inproj_kernel,
        grid=(B, T // tm),
        in_specs=[pl.BlockSpec((1, tm, D), lambda b, i: (b, i, 0)), vec, vec]
        + [pl.BlockSpec(w.shape, lambda b, i: (0, 0)) for w in weights],
        out_specs=[pl.BlockSpec((1, tm, w.shape[1]), lambda b, i: (b, i, 0)) for w in weights],
        out_shape=[jax.ShapeDtypeStruct((B, T, w.shape[1]), F32) for w in weights],
        compiler_params=_cparams("parallel", "parallel"),
        name="in_projection",
    )(x, shift, scale, *weights)


RW_COLS = 3 * RW_WIDTH + G_RANK + 4 * LANES


def _softplus(z):
    return jnp.maximum(z, 0.0) + jnp.log(1.0 + jnp.exp(-jnp.abs(z)))


def _rwkv_prep_kernel(x_ref, xp_ref, xn_ref, conv_ref, kk_ref, ka_ref, rk_ref, w0_ref, a0_ref,
                      wup_ref, aup_ref, gup_ref, e_ref,
                      r_o, v_o, kk_o, ld0_o, b0_o, kr0_o, ld1_o, b1_o, kr1_o, bonus_o, gate_o):
    i = pl.program_id(1)
    n = pl.num_programs(1)
    x = x_ref[0]
    W3 = 3 * RW_WIDTH
    raw = x[:, :W3]
    tm = raw.shape[0]
    row = lax.broadcasted_iota(jnp.int32, (tm, 1), 0)
    prev_row = jnp.where(i > 0, xp_ref[0, SUBLANES - 1:SUBLANES, :], 0.0)
    next_row = jnp.where(i < n - 1, xn_ref[0, 0:1, :], 0.0)
    xm = jnp.where(row == 0, prev_row, pltpu.roll(raw, 1, 0))
    xq = jnp.where(row == tm - 1, next_row, pltpu.roll(raw, tm - 1, 0))
    cw = conv_ref[...]
    y = xm * cw[0:1] + raw * cw[1:2] + xq * cw[2:3]
    r = y[:, :RW_WIDTH]
    k = y[:, RW_WIDTH:2 * RW_WIDTH]
    v = y[:, 2 * RW_WIDTH:W3]
    E = e_ref[...]
    kkv = k * kk_ref[...]
    kk = kkv / jnp.maximum(jnp.sqrt(_dot_split(kkv * kkv, E)), 1e-12)
    r_o[0] = r
    v_o[0] = v
    kk_o[0] = kk
    g_dn = x[:, W3:W3 + G_RANK]
    gate_o[0] = _dot(jax.nn.sigmoid(g_dn).astype(BF16), gup_ref[...])
    kr_sum = jnp.zeros_like(r)
    outs = ((ld0_o, b0_o, kr0_o), (ld1_o, b1_o, kr1_o))
    for d in range(2):
        base = W3 + G_RANK
        w_dn = x[:, base + d * LANES: base + (d + 1) * LANES]
        a_dn = x[:, base + (2 + d) * LANES: base + (3 + d) * LANES]
        z = w0_ref[d:d + 1, :] + _dot(jnp.tanh(w_dn).astype(BF16), wup_ref[d])
        logw = -_softplus(-z) - 0.5
        a = jax.nn.sigmoid(a0_ref[d:d + 1, :] + _dot(a_dn.astype(BF16), aup_ref[d]))
        kr = k * (1.0 + (a - 1.0) * ka_ref[...])
        ld_o, b_o, kr_o = outs[d]
        ld_o[0] = -jnp.exp(logw)
        b_o[0] = a * kk
        kr_o[0] = kr
        kr_sum = kr_sum + kr
    bonus_o[0] = _dot_split(r * kr_sum * rk_ref[...], E) * v


def rwkv_prep(rw, p):
    B, T, _ = rw.shape
    tm = _tile(T, 256)
    nh = tm // SUBLANES
    last = T // SUBLANES - 1
    W3 = 3 * RW_WIDTH
    full = lambda a: pl.BlockSpec(a.shape, lambda b, i: (0,) * a.ndim)
    params = [p["conv"], p["k_k"], p["k_a"], p["r_k"], p["w0"], p["a0"], p["w_up"], p["a_up"], p["g_up"], p["E"]]
    outs = pl.pallas_call(
        _rwkv_prep_kernel,
        grid=(B, T // tm),
        in_specs=[pl.BlockSpec((1, tm, RW_COLS), lambda b, i: (b, i, 0)),
                  pl.BlockSpec((1, SUBLANES, W3), lambda b, i: (b, jnp.maximum(i * nh - 1, 0), 0)),
                  pl.BlockSpec((1, SUBLANES, W3), lambda b, i: (b, jnp.minimum((i + 1) * nh, last), 0))]
        + [full(a) for a in params],
        out_specs=[pl.BlockSpec((1, tm, RW_WIDTH), lambda b, i: (b, i, 0))] * 11,
        out_shape=[jax.ShapeDtypeStruct((B, T, RW_WIDTH), F32)] * 11,
        compiler_params=_cparams("parallel", "parallel"),
        name="rwkv_prep",
    )(rw, rw, rw, *params)
    return outs


PAIR = 2 * RW_HEAD
N_PAIRS = RW_HEADS // 2
CHUNK_GROUP = 4


def _pair_masks(reverse):
    i = lax.broadcasted_iota(jnp.int32, (PAIR, PAIR), 0)
    j = lax.broadcasted_iota(jnp.int32, (PAIR, PAIR), 1)
    same = (i // CHUNK) == (j // CHUNK)
    strict = same & ((j > i) if reverse else (j < i))
    incl = same & ((j >= i) if reverse else (j <= i))
    blk = (i // INV_BLOCK) == (j // INV_BLOCK)
    eye = jnp.where(i == j, 1.0, 0.0).astype(F32)
    t = lax.broadcasted_iota(jnp.int32, (CHUNK, CHUNK), 0)
    u = lax.broadcasted_iota(jnp.int32, (CHUNK, CHUNK), 1)
    tri = jnp.where((u >= t) if reverse else (u <= t), 1.0, 0.0).astype(F32)
    first = lax.broadcasted_iota(jnp.int32, (1, PAIR), 1) < RW_HEAD
    return strict, incl, blk, eye, tri, first


def _pair_chunk_math(ld, r, v, kk, b, kr, masks):
    strict, incl, blk, eye, tri, first = masks
    P = ld.shape[0]
    ein = lambda spec, a, c, prec=None: jnp.einsum(spec, a, c, precision=prec, preferred_element_type=F32)
    tri_b = jnp.broadcast_to(tri.astype(BF16), (P, CHUNK, CHUNK))
    ld_hi = ld.astype(BF16)
    ld_r = ld - ld_hi.astype(F32)
    ld_mid = ld_r.astype(BF16)
    ld_lo = (ld_r - ld_mid.astype(F32)).astype(BF16)
    Lc = ein("pct,ptk->pck", tri_b, ld_hi) + ein("pct,ptk->pck", tri_b, ld_mid) + ein("pct,ptk->pck", tri_b, ld_lo)
    Lx = Lc - ld
    Lt = jnp.sum(ld, axis=1, keepdims=True)
    ginv = jnp.exp(-Lc)
    gout = jnp.exp(Lt - Lc)
    stack = lambda x: jnp.concatenate([jnp.where(first, x, 0.0), jnp.where(first, 0.0, x)], axis=1).astype(BF16)
    twice = lambda x: jnp.concatenate([x, x], axis=1).astype(BF16)
    Xk = stack(kk * jnp.exp(Lx))
    Xr = stack(r * jnp.exp(Lc))
    Vs = stack(v)
    Bs = stack(b * gout)
    Ks = stack(kr * gout)
    mm = lambda a, c: ein("pij,pjk->pik", a.astype(BF16), c.astype(BF16))
    nt = lambda a, c: ein("pik,pjk->pij", a, c)
    tn = lambda a, c: ein("pji,pjk->pik", a, c.astype(BF16))
    XX = jnp.concatenate([Xk, Xr], axis=1)
    Mb = nt(XX, twice(b * ginv))
    Mk = nt(XX, twice(kr * ginv))
    Mab = jnp.where(strict, Mb[:, :PAIR], 0.0)
    Arb = jnp.where(incl, Mb[:, PAIR:], 0.0)
    Mak = jnp.where(strict, Mk[:, :PAIR], 0.0)
    Ark = jnp.where(incl, Mk[:, PAIR:], 0.0)
    Nd = jnp.where(blk, Mab, 0.0)
    No = Mab - Nd
    N2 = mm(Nd, Nd)
    N4 = mm(N2, N2)
    N8 = mm(N4, N4)
    Td = mm(mm(mm(eye - Nd, eye + N2), eye + N4), eye + N8)
    M2 = mm(Td, No)
    Tm = mm(mm(eye - M2, eye + mm(M2, M2)), Td)
    P1 = mm(Tm, Xk)
    P2 = mm(Tm, mm(Mak, Vs))
    Q1s = Xr.astype(F32) - mm(Arb, P1)
    Yis = mm(Ark, Vs) - mm(Arb, P2)
    G = eye * jnp.exp(Lt) - tn(Bs, P1)
    H = tn(Ks, Vs) - tn(Bs, P2)
    return Q1s[:, :CHUNK] + Q1s[:, CHUNK:], Yis[:, :CHUNK] + Yis[:, CHUNK:], G, H


def _rwkv_chunk_kernel(ld_ref, r_ref, v_ref, kk_ref, b_ref, kr_ref, q_o, yi_o, g_o, h_o, *, nch, reverse):
    masks = _pair_masks(reverse)
    group = min(CHUNK_GROUP, nch)

    def body(i, carry):
        c0 = i * group
        rows = [pl.ds(pl.multiple_of((c0 + u) * CHUNK, CHUNK), CHUNK) for u in range(group)]
        get = lambda ref: jnp.stack([ref[0, rows[u], j * PAIR:(j + 1) * PAIR]
                                     for u in range(group) for j in range(N_PAIRS)])
        Q1, Yi, G, H = _pair_chunk_math(get(ld_ref), get(r_ref), get(v_ref), get(kk_ref), get(b_ref), get(kr_ref),
                                        masks)
        for u in range(group):
            for j in range(N_PAIRS):
                q_o[0, rows[u], j * PAIR:(j + 1) * PAIR] = Q1[u * N_PAIRS + j]
                yi_o[0, rows[u], j * PAIR:(j + 1) * PAIR] = Yi[u * N_PAIRS + j]
        mats = pl.ds(c0 * N_PAIRS, group * N_PAIRS)
        g_o[0, mats] = G
        h_o[0, mats] = H
        return carry

    lax.fori_loop(0, nch // group, body, 0)


def rwkv_chunks(ld, r, v, kk, b, kr, reverse):
    B, T, W = ld.shape
    tt = _tile(T, 4 * CHUNK)
    nch = tt // CHUNK
    tok = pl.BlockSpec((1, tt, W), lambda bb, i: (bb, i, 0))
    mat = pl.BlockSpec((1, nch * N_PAIRS, PAIR, PAIR), lambda bb, i: (bb, i, 0, 0))
    mats = jax.ShapeDtypeStruct((B, T // CHUNK * N_PAIRS, PAIR, PAIR), F32)
    return pl.pallas_call(
        functools.partial(_rwkv_chunk_kernel, nch=nch, reverse=reverse),
        grid=(B, T // tt),
        in_specs=[tok] * 6,
        out_specs=[tok, tok, mat, mat],
        out_shape=[jax.ShapeDtypeStruct((B, T, W), F32)] * 2 + [mats, mats],
        compiler_params=_cparams("parallel", "parallel"),
        name="rwkv_chunks_bwd" if reverse else "rwkv_chunks_fwd",
    )(ld, r, v, kk, b, kr)


def _rwkv_seq_kernel(q_ref, yi_ref, g_ref, h_ref, s0_ref, y_o, sf_o, s_scr, *, nch, reverse):
    @pl.when(pl.program_id(1) == 0)
    def _():
        s_scr[...] = s0_ref[0]

    def body(cc, carry):
        c = (nch - 1 - cc) if reverse else cc
        rows = pl.ds(pl.multiple_of(c * CHUNK, CHUNK), CHUNK)
        for j in range(N_PAIRS):
            lanes = slice(j * PAIR, (j + 1) * PAIR)
            S = s_scr[j]
            y_o[0, rows, lanes] = _dot3(q_ref[0, rows, lanes], S) + yi_ref[0, rows, lanes]
            s_scr[j] = _dot3(g_ref[0, c * N_PAIRS + j], S) + h_ref[0, c * N_PAIRS + j]
        return carry

    lax.fori_loop(0, nch, body, 0)
    sf_o[0] = s_scr[...]


def rwkv_sequential(q1, yi, g, hm, s0, reverse):
    B, T, W = q1.shape
    tt = _tile(T, 8 * CHUNK)
    n = T // tt
    nch = tt // CHUNK
    step = (lambda i: n - 1 - i) if reverse else (lambda i: i)
    tok = pl.BlockSpec((1, tt, W), lambda bb, i: (bb, step(i), 0))
    mat = pl.BlockSpec((1, nch * N_PAIRS, PAIR, PAIR), lambda bb, i: (bb, step(i), 0, 0))
    sspec = pl.BlockSpec((1, N_PAIRS, PAIR, PAIR), lambda bb, i: (bb, 0, 0, 0))
    return pl.pallas_call(
        functools.partial(_rwkv_seq_kernel, nch=nch, reverse=reverse),
        grid=(B, n),
        in_specs=[tok, tok, mat, mat, sspec],
        out_specs=[tok, sspec],
        out_shape=[jax.ShapeDtypeStruct((B, T, W), F32), jax.ShapeDtypeStruct((B, N_PAIRS, PAIR, PAIR), F32)],
        scratch_shapes=[pltpu.VMEM((N_PAIRS, PAIR, PAIR), F32)],
        compiler_params=_cparams("parallel", "arbitrary"),
        name="rwkv_seq_bwd" if reverse else "rwkv_seq_fwd",
    )(q1, yi, g, hm, s0)


def _rwkv_readout_kernel(yf_ref, yb_ref, bonus_ref, gate_ref, gw_ref, gb_ref, e_ref, o_ref):
    E = e_ref[...]
    ys = yf_ref[0] + yb_ref[0]
    yc = ys - _dot_split(ys, E) * (1.0 / RW_HEAD)
    var = _dot_split(yc * yc, E) * (1.0 / RW_HEAD)
    yn = yc * lax.rsqrt(var + GN_EPS) * gw_ref[...] + gb_ref[...]
    o_ref[0] = (yn + bonus_ref[0]) * gate_ref[0]


def rwkv_readout(yf, yb, bonus, gate, p):
    B, T, W = yf.shape
    tm = _tile(T, 512)
    spec = pl.BlockSpec((1, tm, W), lambda b, i: (b, i, 0))
    full = lambda a: pl.BlockSpec(a.shape, lambda b, i: (0,) * a.ndim)
    return pl.pallas_call(
        _rwkv_readout_kernel,
        grid=(B, T // tm),
        in_specs=[spec] * 4 + [full(p["gn_w"]), full(p["gn_b"]), full(p["E"])],
        out_specs=spec,
        out_shape=jax.ShapeDtypeStruct((B, T, W), F32),
        compiler_params=_cparams("parallel", "parallel"),
        name="rwkv_readout",
    )(yf, yb, bonus, gate, p["gn_w"], p["gn_b"], p["E"])


def rwkv_branch(rw_lat, rw_ctx, p, emit_ctx):
    prep_l = rwkv_prep(rw_lat, p)
    prep_c = rwkv_prep(rw_ctx, p)
    B = rw_lat.shape[0]
    s_zero = jnp.zeros((B, N_PAIRS, PAIR, PAIR), F32)

    def scans(prep, s0s):
        r, v, kk = prep[:3]
        ys, finals = [], []
        for d in range(2):
            ld, b, kr = prep[3 + 3 * d: 6 + 3 * d]
            q1, yi, g, hm = rwkv_chunks(ld, r, v, kk, b, kr, reverse=bool(d))
            y, sf = rwkv_sequential(q1, yi, g, hm, s0s[d], reverse=bool(d))
            ys.append(y)
            finals.append(sf)
        return ys, finals

    ys_c, fin_c = scans(prep_c, (s_zero, s_zero))
    ys_l, _ = scans(prep_l, fin_c)
    out_l = rwkv_readout(ys_l[0], ys_l[1], prep_l[9], prep_l[10], p)
    out_c = rwkv_readout(ys_c[0], ys_c[1], prep_c[9], prep_c[10], p) if emit_ctx else None
    return out_l, out_c


def _dft_mats(n):
    a = 2.0 * np.pi * np.outer(np.arange(n), np.arange(n)) / n
    return np.cos(a), np.sin(a)


def _fft1_kernel(u_ref, c_ref, s_ref, twc_ref, tws_ref, ar_o, ai_o, *, tn2, ch):
    U = u_ref[0]
    Ar = _dot3(c_ref[...], U)
    Ai = -_dot3(s_ref[...], U)
    twc = twc_ref[0]
    tws = tws_ref[0]
    for j in range(tn2):
        ct = twc[:, j:j + 1]
        st = tws[:, j:j + 1]
        a_r = Ar[:, j * ch:(j + 1) * ch]
        a_i = Ai[:, j * ch:(j + 1) * ch]
        ar_o[0, j] = a_r * ct + a_i * st
        ai_o[0, j] = a_i * ct - a_r * st


def _fft2_kernel(ar_ref, ai_ref, c_ref, s_ref, cc_ref, sc_ref, o_ref):
    Ar = ar_ref[0]
    Ai = ai_ref[0]
    C = c_ref[...]
    S = s_ref[...]
    Yr = _dot3(C, Ar) + _dot3(S, Ai)
    Yi = _dot3(C, Ai) - _dot3(S, Ar)
    Cc = cc_ref[...]
    Sc = sc_ref[...]
    for m in range(Ar.shape[1] // LANES):
        sl = slice(m * LANES, (m + 1) * LANES)
        o_ref[0, :, sl] = _dot3(Yr[:, sl], Cc) + _dot3(Yi[:, sl], Sc)


def fourier_mixer(u):
    B, T, ch = u.shape
    lg = int(round(math.log2(T)))
    assert 1 << lg == T
    N1 = 1 << ((lg + 1) // 2)
    N2 = T // N1
    c1, s1 = _dft_mats(N1)
    c2, s2 = _dft_mats(N2)
    tw = 2.0 * np.pi * np.outer(np.arange(N1), np.arange(N2)) / T
    tn2 = min(SUBLANES, N2)
    nj = N2 // tn2
    twc = np.cos(tw).reshape(N1, nj, tn2).transpose(1, 0, 2)
    tws = np.sin(tw).reshape(N1, nj, tn2).transpose(1, 0, 2)
    cg, sg = _dft_mats(FT_GROUP)
    scale = 1.0 / math.sqrt(T * FT_GROUP)
    eye2 = np.eye(LANES // FT_GROUP)
    cc = np.kron(eye2, cg) * scale
    sc = np.kron(eye2, sg) * scale
    f = lambda a: jnp.asarray(a, F32)
    full2 = lambda n, m: pl.BlockSpec((n, m), lambda b, j: (0, 0))

    ar, ai = pl.pallas_call(
        functools.partial(_fft1_kernel, tn2=tn2, ch=ch),
        grid=(B, nj),
        in_specs=[pl.BlockSpec((1, N1, tn2 * ch), lambda b, j: (b, 0, j)),
                  full2(N1, N1), full2(N1, N1),
                  pl.BlockSpec((1, N1, tn2), lambda b, j: (j, 0, 0)),
                  pl.BlockSpec((1, N1, tn2), lambda b, j: (j, 0, 0))],
        out_specs=[pl.BlockSpec((1, tn2, N1, ch), lambda b, j: (b, j, 0, 0))] * 2,
        out_shape=[jax.ShapeDtypeStruct((B, N2, N1, ch), F32)] * 2,
        compiler_params=_cparams("parallel", "parallel"),
        name="fft_stage1",
    )(u.reshape(B, N1, N2 * ch), f(c1), f(s1), f(twc), f(tws))

    tk1 = min(SUBLANES, N1)
    blk = pl.BlockSpec((1, N2, tk1 * ch), lambda b, j: (b, 0, j))
    out = pl.pallas_call(
        _fft2_kernel,
        grid=(B, N1 // tk1),
        in_specs=[blk, blk, full2(N2, N2), full2(N2, N2), full2(LANES, LANES), full2(LANES, LANES)],
        out_specs=blk,
        out_shape=jax.ShapeDtypeStruct((B, N2, N1 * ch), F32),
        compiler_params=_cparams("parallel", "parallel"),
        name="fft_stage2",
    )(ar.reshape(B, N2, N1 * ch), ai.reshape(B, N2, N1 * ch), f(c2), f(s2), f(cc), f(sc))
    return out.reshape(B, T, ch)


HEAD_SLAB = LANES
ROPE_SHIFT = HEAD_SLAB - QK_ROPE
ONES_ROWS = 16
V_ROWS = V_HEAD + ONES_ROWS
ATTN_UNROLL = 6
ATTN_TILES = 4


def _rms(x, w):
    return x * lax.rsqrt(jnp.mean(x * x, axis=-1, keepdims=True) + RMS_EPS) * w


def _qproj_kernel(cq_ref, nw_ref, w_ref, ct_ref, st_ref, q_o):
    q = _dot(_rms(cq_ref[0], nw_ref[...]).astype(BF16), w_ref[...])
    ct = ct_ref[...]
    st = st_ref[...]
    for h in range(MLA_HEADS):
        sl = slice(h * HEAD_SLAB, (h + 1) * HEAD_SLAB)
        s = q[:, sl]
        q_o[0, :, sl] = ((s * ct + pltpu.roll(s, ROPE_SHIFT, 1) * st) * Q_SCALE).astype(BF16)


def _kvproj_kernel(ckv_ref, nw_ref, wk_ref, wvt_ref, ct_ref, st_ref, k_o, vt_o):
    x = ckv_ref[0]
    n = _rms(x[:, :KV_RANK], nw_ref[...]).astype(BF16)
    rs = x[:, KV_RANK:KV_RANK + HEAD_SLAB]
    rope = rs * ct_ref[...] + pltpu.roll(rs, ROPE_SHIFT, 1) * st_ref[...]
    kn = _dot(n, wk_ref[...])
    for h in range(MLA_HEADS):
        sl = slice(h * HEAD_SLAB, (h + 1) * HEAD_SLAB)
        k_o[0, :, sl] = (kn[:, sl] + rope).astype(BF16)
    vt = lax.dot_general(wvt_ref[...], n, NT, preferred_element_type=F32)
    row = lax.broadcasted_iota(jnp.int32, vt.shape, 0)
    vt_o[0] = jnp.where(row % V_ROWS >= V_HEAD, 1.0, vt).astype(BF16)


def q_projection(cq, nw, w, ct, st):
    B, T, R = cq.shape
    tm = _tile(T, 256)
    W = MLA_HEADS * HEAD_SLAB
    tab = pl.BlockSpec((tm, HEAD_SLAB), lambda b, i: (i, 0))
    return pl.pallas_call(
        _qproj_kernel,
        grid=(B, T // tm),
        in_specs=[pl.BlockSpec((1, tm, R), lambda b, i: (b, i, 0)),
                  pl.BlockSpec(nw.shape, lambda b, i: (0, 0)),
                  pl.BlockSpec(w.shape, lambda b, i: (0, 0)), tab, tab],
        out_specs=pl.BlockSpec((1, tm, W), lambda b, i: (b, i, 0)),
        out_shape=jax.ShapeDtypeStruct((B, T, W), BF16),
        compiler_params=_cparams("parallel", "parallel"),
        name="q_projection",
    )(cq, nw, w, ct, st)


def kv_projection(ckv, nw, wk, wvt, ct, st):
    B, T, R = ckv.shape
    tm = _tile(T, 256)
    W = MLA_HEADS * HEAD_SLAB
    tab = pl.BlockSpec((tm, HEAD_SLAB), lambda b, i: (i, 0))
    return pl.pallas_call(
        _kvproj_kernel,
        grid=(B, T // tm),
        in_specs=[pl.BlockSpec((1, tm, R), lambda b, i: (b, i, 0)),
                  pl.BlockSpec(nw.shape, lambda b, i: (0, 0)),
                  pl.BlockSpec(wk.shape, lambda b, i: (0, 0)),
                  pl.BlockSpec(wvt.shape, lambda b, i: (0, 0)), tab, tab],
        out_specs=[pl.BlockSpec((1, tm, W), lambda b, i: (b, i, 0)),
                   pl.BlockSpec((1, MLA_HEADS * V_ROWS, tm), lambda b, i: (b, 0, i))],
        out_shape=[jax.ShapeDtypeStruct((B, T, W), BF16), jax.ShapeDtypeStruct((B, MLA_HEADS * V_ROWS, T), BF16)],
        compiler_params=_cparams("parallel", "parallel"),
        name="kv_projection",
    )(ckv, nw, wk, wvt, ct, st)


def _attn_kernel(q_ref, k_ref, vt_ref, o_ref, m_scr, acc_scr, sa_scr, sb_scr, ma_scr, mb_scr, *, tq, tkc, nkc):
    ntile = q_ref.shape[1] // tq
    bufs = ((sa_scr, ma_scr), (sb_scr, mb_scr))

    def chunk(c):
        return pl.ds(pl.multiple_of(c * tkc, tkc), tkc)

    def scores(t, c, dst, dmax):
        for hh in range(2):
            q = q_ref[0, t * tq:(t + 1) * tq, hh * HEAD_SLAB:(hh + 1) * HEAD_SLAB]
            kc = k_ref[0, chunk(c), hh * HEAD_SLAB:(hh + 1) * HEAD_SLAB]
            s = lax.dot_general(kc, q, NT, preferred_element_type=F32)
            dst[hh] = s
            dmax[hh] = jnp.max(s, axis=0, keepdims=True)

    def consume(src, smax, c):
        for hh in range(2):
            m_old = m_scr[hh]
            m_new = jnp.maximum(m_old, smax[hh])
            pr = jnp.exp2(src[hh] - m_new).astype(BF16)
            alpha = jnp.exp2(m_old - m_new)
            m_scr[hh] = m_new
            rows = slice(hh * V_ROWS, (hh + 1) * V_ROWS)
            acc_scr[rows, :] = alpha * acc_scr[rows, :] + _dot(vt_ref[0, rows, chunk(c)], pr)

    unroll = ATTN_UNROLL if nkc > 2 * ATTN_UNROLL else 1
    ngroups = (nkc - 1) // (2 * unroll)
    done = 2 * unroll * ngroups
    first = 0
    scores(0, 0, *bufs[first])
    for t in range(ntile):
        m_scr[...] = jnp.full(m_scr.shape, -jnp.inf, F32)
        acc_scr[...] = jnp.zeros(acc_scr.shape, F32)
        cur, nxt = bufs[first], bufs[1 - first]

        def body(i, carry, t=t, cur=cur, nxt=nxt):
            c = 2 * unroll * i
            for _ in range(unroll):
                scores(t, c + 1, *nxt)
                consume(*cur, c)
                scores(t, c + 2, *cur)
                consume(*nxt, c + 1)
                c = c + 2
            return carry

        lax.fori_loop(0, ngroups, body, 0)
        for cc in range(done, nkc):
            cur, nxt = bufs[first], bufs[1 - first]
            if cc + 1 < nkc:
                scores(t, cc + 1, *nxt)
            elif t + 1 < ntile:
                scores(t + 1, 0, *nxt)
            consume(*cur, cc)
            first = 1 - first
        outs = []
        for hh in range(2):
            base = hh * V_ROWS
            outs.append(acc_scr[base:base + V_HEAD, :] / acc_scr[base + V_HEAD:base + V_HEAD + 1, :])
        o_ref[0, t * tq:(t + 1) * tq, :] = jnp.concatenate(outs, axis=0).T.astype(o_ref.dtype)


def attention(q, k, vt):
    B, T, _ = q.shape
    Tk = k.shape[1]
    tq = _tile(T, 256)
    tb = _tile(T, ATTN_TILES * tq)
    tkc = next(c for c in (640, 512, 256, 128) if Tk % c == 0)
    hp = MLA_HEADS // 2
    return pl.pallas_call(
        functools.partial(_attn_kernel, tq=tq, tkc=tkc, nkc=Tk // tkc),
        grid=(B, hp, T // tb),
        in_specs=[pl.BlockSpec((1, tb, 2 * HEAD_SLAB), lambda b, h, i: (b, i, h)),
                  pl.BlockSpec((1, Tk, 2 * HEAD_SLAB), lambda b, h, i: (b, 0, h)),
                  pl.BlockSpec((1, 2 * V_ROWS, Tk), lambda b, h, i: (b, h, 0))],
        out_specs=pl.BlockSpec((1, tb, 2 * V_HEAD), lambda b, h, i: (b, i, h)),
        out_shape=jax.ShapeDtypeStruct((B, T, MLA_WIDTH), BF16),
        scratch_shapes=[pltpu.VMEM((2, 1, tq), F32), pltpu.VMEM((2 * V_ROWS, tq), F32),
                        pltpu.VMEM((2, tkc, tq), F32), pltpu.VMEM((2, tkc, tq), F32),
                        pltpu.VMEM((2, 1, tq), F32), pltpu.VMEM((2, 1, tq), F32)],
        compiler_params=_cparams("parallel", "parallel", "arbitrary"),
        name="mla_attention",
    )(q, k, vt)


def _outproj_kernel(rw_ref, ft_ref, att_ref, x_ref, g_ref, lw_ref, lb_ref, w_ref, o_ref, *, alpha):
    w = w_ref
    mix = (_dot(rw_ref[0].astype(BF16), w[0:RW_WIDTH, :])
           + _dot(ft_ref[0].astype(BF16), w[RW_WIDTH:RW_WIDTH + FT_WIDTH, :])
           + _dot(att_ref[0], w[RW_WIDTH + FT_WIDTH:, :]))
    z = alpha * x_ref[0] + g_ref[0] * mix
    o_ref[0] = _standardize(z, LN_EPS) * lw_ref[...] + lb_ref[...]


def out_projection_ln(rw, ft, att, x, gate, ln_w, ln_b, w_out, alpha):
    B, T, D = x.shape
    tm = _tile(T, 256)
    tok = lambda n: pl.BlockSpec((1, tm, n), lambda b, i: (b, i, 0))
    row = pl.BlockSpec((1, D), lambda b, i: (0, 0))
    return pl.pallas_call(
        functools.partial(_outproj_kernel, alpha=alpha),
        grid=(B, T // tm),
        in_specs=[tok(RW_WIDTH), tok(FT_WIDTH), tok(MLA_WIDTH), tok(D),
                  pl.BlockSpec((1, 1, D), lambda b, i: (b, 0, 0)), row, row,
                  pl.BlockSpec(w_out.shape, lambda b, i: (0, 0))],
        out_specs=tok(D),
        out_shape=jax.ShapeDtypeStruct((B, T, D), F32),
        compiler_params=_cparams("parallel", "parallel"),
        name="out_projection_ln",
    )(rw, ft, att, x, gate, ln_w, ln_b, w_out)


def _ffn_kernel(x_ref, sh_ref, sc_ref, g_ref, lw_ref, lb_ref, w1_ref, w3_ref, w2_ref, o_ref, h_scr, *, alpha):
    f = pl.program_id(2)

    @pl.when(f == 0)
    def _():
        h = _standardize(x_ref[0], MOD_EPS) * (1.0 + sc_ref[0]) + sh_ref[0]
        h_scr[...] = h.astype(BF16)
        o_ref[0] = jnp.zeros(o_ref.shape[1:], F32)

    hb = h_scr[...]
    a = _dot(hb, w1_ref[...])
    b = _dot(hb, w3_ref[...])
    o_ref[0] += _dot((a * jax.nn.sigmoid(a) * b).astype(BF16), w2_ref[...])

    @pl.when(f == pl.num_programs(2) - 1)
    def _():
        z = alpha * x_ref[0] + g_ref[0] * o_ref[0]
        o_ref[0] = _standardize(z, LN_EPS) * lw_ref[...] + lb_ref[...]


def ffn_ln(x, shift, scale, gate, ln_w, ln_b, w1, w3, w2, alpha):
    B, T, D = x.shape
    F = w1.shape[1]
    tm = _tile(T, 1024)
    tf = _tile(F, 256)
    tok = pl.BlockSpec((1, tm, D), lambda b, i, f: (b, i, 0))
    vec = pl.BlockSpec((1, 1, D), lambda b, i, f: (b, 0, 0))
    row = pl.BlockSpec((1, D), lambda b, i, f: (0, 0))
    return pl.pallas_call(
        functools.partial(_ffn_kernel, alpha=alpha),
        grid=(B, T // tm, F // tf),
        in_specs=[tok, vec, vec, vec, row, row,
                  pl.BlockSpec((D, tf), lambda b, i, f: (0, f)),
                  pl.BlockSpec((D, tf), lambda b, i, f: (0, f)),
                  pl.BlockSpec((tf, D), lambda b, i, f: (f, 0))],
        out_specs=tok,
        out_shape=jax.ShapeDtypeStruct((B, T, D), F32),
        scratch_shapes=[pltpu.VMEM((tm, D), BF16)],
        compiler_params=_cparams("parallel", "parallel", "arbitrary"),
        name="ffn_ln",
    )(x, shift, scale, gate, ln_w, ln_b, w1, w3, w2)


def _router_kernel(x_ref, sh_ref, sc_ref, wr_ref, h_o, ti_o, tg_o):
    h = _standardize(x_ref[0], MOD_EPS) * (1.0 + sc_ref[0]) + sh_ref[0]
    h_o[0] = h.astype(BF16)
    logits = _dot(h, wr_ref[...], HI)
    lane = lax.broadcasted_iota(jnp.int32, logits.shape, 1)
    neg = jnp.float32(-jnp.inf)
    logits = jnp.where(lane < N_EXPERTS, logits, neg)
    m1 = jnp.max(logits, axis=-1, keepdims=True)
    i1 = jnp.min(jnp.where(logits == m1, lane, LANES), axis=-1, keepdims=True)
    rest = jnp.where(lane == i1, neg, logits)
    m2 = jnp.max(rest, axis=-1, keepdims=True)
    i2 = jnp.min(jnp.where(rest == m2, lane, LANES), axis=-1, keepdims=True)
    e = jnp.exp(m2 - m1)
    g1 = 1.0 / (1.0 + e)
    g2 = e / (1.0 + e)
    ti_o[0] = jnp.where(lane == 0, i1, jnp.where(lane == 1, i2, 0))
    tg_o[0] = jnp.where(lane == 0, g1, jnp.where(lane == 1, g2, 0.0))


def moe_router(x, shift, scale, wr):
    B, T, D = x.shape
    tm = _tile(T, 512)
    tok = lambda n: pl.BlockSpec((1, tm, n), lambda b, i: (b, i, 0))
    vec = pl.BlockSpec((1, 1, D), lambda b, i: (b, 0, 0))
    return pl.pallas_call(
        _router_kernel,
        grid=(B, T // tm),
        in_specs=[tok(D), vec, vec, pl.BlockSpec(wr.shape, lambda b, i: (0, 0))],
        out_specs=[tok(D), tok(LANES), tok(LANES)],
        out_shape=[jax.ShapeDtypeStruct((B, T, D), BF16), jax.ShapeDtypeStruct((B, T, LANES), jnp.int32),
                   jax.ShapeDtypeStruct((B, T, LANES), F32)],
        compiler_params=_cparams("parallel", "parallel"),
        name="moe_router",
    )(x, shift, scale, wr)


def _expert_kernel(be_ref, bv_ref, x_ref, w1_ref, w3_ref, w2_ref, o_ref):
    blk = pl.program_id(0)
    f = pl.program_id(1)
    valid = bv_ref[blk] > 0

    @pl.when(f == 0)
    def _():
        o_ref[...] = jnp.zeros(o_ref.shape, F32)

    @pl.when(valid)
    def _():
        xb = x_ref[...]
        a = _dot(xb, w1_ref[0].astype(BF16))
        b = _dot(xb, w3_ref[0].astype(BF16))
        o_ref[...] += _dot((a * jax.nn.sigmoid(a) * b).astype(BF16), w2_ref[0].astype(BF16))


def expert_ffn(xg, blk_expert, blk_valid, w1, w3, w2):
    P, D = xg.shape
    F = w1.shape[2]
    tf = _tile(F, 512)
    nblk = P // MOE_ROWS
    grid_spec = pltpu.PrefetchScalarGridSpec(
        num_scalar_prefetch=2,
        grid=(nblk, F // tf),
        in_specs=[pl.BlockSpec((MOE_ROWS, D), lambda i, f, be, bv: (i, 0)),
                  pl.BlockSpec((1, D, tf), lambda i, f, be, bv: (be[i], 0, jnp.where(bv[i] > 0, f, 0))),
                  pl.BlockSpec((1, D, tf), lambda i, f, be, bv: (be[i], 0, jnp.where(bv[i] > 0, f, 0))),
                  pl.BlockSpec((1, tf, D), lambda i, f, be, bv: (be[i], jnp.where(bv[i] > 0, f, 0), 0))],
        out_specs=pl.BlockSpec((MOE_ROWS, D), lambda i, f, be, bv: (i, 0)),
    )
    return pl.pallas_call(
        _expert_kernel,
        grid_spec=grid_spec,
        out_shape=jax.ShapeDtypeStruct((P, D), F32),
        compiler_params=_cparams("parallel", "arbitrary"),
        name="expert_ffn",
    )(blk_expert, blk_valid, xg, w1, w3, w2)


def _combine_ln_kernel(x_ref, y0_ref, y1_ref, tg_ref, g_ref, lw_ref, lb_ref, o_ref, *, alpha):
    tg = tg_ref[0]
    y = tg[:, 0:1] * y0_ref[0] + tg[:, 1:2] * y1_ref[0]
    z = alpha * x_ref[0] + g_ref[0] * y
    o_ref[0] = _standardize(z, LN_EPS) * lw_ref[...] + lb_ref[...]


def combine_ln(x, y0, y1, tg, gate, ln_w, ln_b, alpha):
    B, T, D = x.shape
    tm = _tile(T, 512)
    tok = pl.BlockSpec((1, tm, D), lambda b, i: (b, i, 0))
    row = pl.BlockSpec((1, D), lambda b, i: (0, 0))
    return pl.pallas_call(
        functools.partial(_combine_ln_kernel, alpha=alpha),
        grid=(B, T // tm),
        in_specs=[tok, tok, tok, pl.BlockSpec((1, tm, LANES), lambda b, i: (b, i, 0)),
                  pl.BlockSpec((1, 1, D), lambda b, i: (b, 0, 0)), row, row],
        out_specs=tok,
        out_shape=jax.ShapeDtypeStruct((B, T, D), F32),
        compiler_params=_cparams("parallel", "parallel"),
        name="combine_ln",
    )(x, y0, y1, tg, gate, ln_w, ln_b)


def moe_ln(x, shift, scale, gate, ln_w, ln_b, router, w1, w3, w2, alpha):
    B, T, D = x.shape
    N = B * T
    E = router.shape[1]
    wr = jnp.zeros((D, LANES), F32).at[:, :E].set(router)
    h, ti, tg = moe_router(x, shift, scale, wr)
    top_i = ti.reshape(N, LANES)[:, :2]
    e_flat = top_i.reshape(-1)
    onehot = (e_flat[:, None] == jnp.arange(E, dtype=jnp.int32)[None, :]).astype(jnp.int32)
    rank = jnp.sum((jnp.cumsum(onehot, axis=0) - onehot) * onehot, axis=1)
    counts = jnp.sum(onehot, axis=0)
    padded = (counts + MOE_ROWS - 1) // MOE_ROWS * MOE_ROWS
    p_ends = jnp.cumsum(padded)
    p_starts = p_ends - padded
    dest = p_starts[e_flat] + rank
    P = -(-(2 * N) // MOE_ROWS) * MOE_ROWS + E * MOE_ROWS
    nblk = P // MOE_ROWS
    tok_flat = jnp.repeat(jnp.arange(N, dtype=jnp.int32), 2)
    slot_tok = jnp.zeros((P,), jnp.int32).at[dest].set(tok_flat)
    blk_start = jnp.arange(nblk, dtype=jnp.int32) * MOE_ROWS
    blk_expert = jnp.minimum(jnp.searchsorted(p_ends, blk_start, side="right"), E - 1).astype(jnp.int32)
    blk_valid = (blk_start < p_ends[-1]).astype(jnp.int32)
    xg = h.reshape(N, D)[slot_tok]
    yg = expert_ffn(xg, blk_expert, blk_valid, w1, w3, w2)
    d2 = dest.reshape(N, 2)
    y0 = yg[d2[:, 0]].reshape(B, T, D)
    y1 = yg[d2[:, 1]].reshape(B, T, D)
    return combine_ln(x, y0, y1, tg, gate, ln_w, ln_b, alpha)


def _rope_partner(w):
    half = ROPE_AXIS // 2
    idx = np.arange(QK_ROPE)
    first = (idx % ROPE_AXIS) < half
    src = np.where(first, idx + half, idx - half)
    sign = np.where(first, -1.0, 1.0).astype(np.float32)
    return w[:, src] * sign


def _layer_params(l, w_in, rw_conv, rw_w0, rw_w_up, rw_a0, rw_a_up, rw_g_up, rw_k_k, rw_k_a, rw_r_k, rw_gn_w,
                  rw_gn_b, mla_q_norm, mla_w_uq, mla_kv_norm, mla_w_ukv, w_out):
    D = w_in.shape[1]
    wi = w_in[l]
    o = np.cumsum([0, 3 * RW_WIDTH, G_RANK, LORA_RANK, LORA_RANK, LORA_RANK, LORA_RANK, FT_WIDTH, Q_RANK, KV_RANK, QK_ROPE])
    piece = lambda i: wi[:, o[i]:o[i + 1]]
    zpad = lambda n: jnp.zeros((D, n), F32)
    lora = [jnp.concatenate([piece(i), zpad(LANES - LORA_RANK)], axis=1) for i in (2, 3, 4, 5)]
    w_rw = jnp.concatenate([piece(0), piece(1)] + lora, axis=1)
    kr = piece(9)
    w_ckv = jnp.concatenate([piece(8), zpad(QK_NOPE), kr, _rope_partner(kr)], axis=1)
    pad_rows = lambda a: jnp.concatenate([a, jnp.zeros((a.shape[0], LANES - LORA_RANK, a.shape[2]), F32)], axis=1)
    head = jnp.arange(RW_WIDTH) // RW_HEAD
    uq = mla_w_uq[l].reshape(Q_RANK, MLA_HEADS, QK_NOPE + QK_ROPE)
    uq_rope = uq[:, :, QK_NOPE:]
    uq_partner = _rope_partner(uq_rope.reshape(Q_RANK * MLA_HEADS, QK_ROPE)).reshape(Q_RANK, MLA_HEADS, QK_ROPE)
    w_q = jnp.concatenate([uq, uq_partner], axis=2).reshape(Q_RANK, MLA_HEADS * HEAD_SLAB)
    ukv = mla_w_ukv[l].reshape(KV_RANK, MLA_HEADS, QK_NOPE + V_HEAD)
    w_k = jnp.concatenate([ukv[:, :, :QK_NOPE], jnp.zeros((KV_RANK, MLA_HEADS, HEAD_SLAB - QK_NOPE), F32)],
                          axis=2).reshape(KV_RANK, MLA_HEADS * HEAD_SLAB)
    w_v = jnp.concatenate([ukv[:, :, QK_NOPE:], jnp.zeros((KV_RANK, MLA_HEADS, ONES_ROWS), F32)],
                          axis=2).reshape(KV_RANK, MLA_HEADS * V_ROWS)
    return {
        "w_in": [w.astype(BF16) for w in (w_rw, piece(6), piece(7), w_ckv)],
        "rw": {"conv": rw_conv[l], "k_k": rw_k_k[l][None], "k_a": rw_k_a[l][None], "r_k": rw_r_k[l].reshape(1, RW_WIDTH),
               "w0": rw_w0[l], "a0": rw_a0[l], "w_up": pad_rows(rw_w_up[l]).astype(BF16), "a_up": pad_rows(rw_a_up[l]).astype(BF16),
               "g_up": rw_g_up[l].astype(BF16), "gn_w": rw_gn_w[l][None], "gn_b": rw_gn_b[l][None],
               "E": (head[:, None] == head[None, :]).astype(BF16)},
        "q_norm": mla_q_norm[l][None], "w_q": w_q.astype(BF16),
        "kv_norm": mla_kv_norm[l][None], "w_k": w_k.astype(BF16), "w_vt": w_v.T.astype(BF16),
        "w_out": w_out[l].astype(BF16),
    }


def _rope_tables(T, use_rope):
    ones = jnp.ones((T, QK_NOPE), F32)
    zeros = jnp.zeros((T, QK_NOPE), F32)
    zpad = jnp.zeros((T, HEAD_SLAB - QK_NOPE - QK_ROPE), F32)
    if use_rope:
        row = jnp.repeat(jnp.arange(T // GRID_W), GRID_W).astype(F32)
        col = (jnp.arange(T) % GRID_W).astype(F32)
        inv = ROPE_THETA ** (-jnp.arange(0, ROPE_AXIS, 2, dtype=F32) / ROPE_AXIS)
        ang = jnp.stack([row[:, None] * inv, col[:, None] * inv], axis=1)
        ang = jnp.broadcast_to(ang[:, :, None, :], (T, 2, 2, ROPE_AXIS // 2)).reshape(T, QK_ROPE)
        cos, sin = jnp.cos(ang), jnp.sin(ang)
    else:
        cos, sin = jnp.ones((T, QK_ROPE), F32), jnp.zeros((T, QK_ROPE), F32)
    return jnp.concatenate([ones, cos, zpad], axis=1), jnp.concatenate([zeros, sin, zpad], axis=1)


def _mixer(h_pieces_lat, h_pieces_ctx, p, tabs_lat, tabs_ctx, emit_ctx):
    rw_l, ft_l, cq_l, ckv_l = h_pieces_lat
    rw_c, ft_c, cq_c, ckv_c = h_pieces_ctx
    rwo_l, rwo_c = rwkv_branch(rw_l, rw_c, p["rw"], emit_ctx)
    fto_l = fourier_mixer(ft_l)
    q_l = q_projection(cq_l, p["q_norm"], p["w_q"], *tabs_lat)
    k_l, vt_l = kv_projection(ckv_l, p["kv_norm"], p["w_k"], p["w_vt"], *tabs_lat)
    k_c, vt_c = kv_projection(ckv_c, p["kv_norm"], p["w_k"], p["w_vt"], *tabs_ctx)
    att_l = attention(q_l, jnp.concatenate([k_l, k_c], axis=1), jnp.concatenate([vt_l, vt_c], axis=2))
    out_c = None
    if emit_ctx:
        fto_c = fourier_mixer(ft_c)
        q_c = q_projection(cq_c, p["q_norm"], p["w_q"], *tabs_ctx)
        att_c = attention(q_c, k_c, vt_c)
        out_c = (rwo_c, fto_c, att_c)
    return (rwo_l, fto_l, att_l), out_c


def kernel(x, c, ctx, c_ctx, ada_w, ada_b, w_in, rw_conv, rw_w0, rw_w_up, rw_a0, rw_a_up, rw_g_up, rw_k_k, rw_k_a,
           rw_r_k, rw_gn_w, rw_gn_b, mla_q_norm, mla_w_uq, mla_kv_norm, mla_w_ukv, w_out, ln1_w, ln1_b, ln2_w, ln2_b,
           ffn_w1, ffn_w3, ffn_w2, moe_router, moe_w1, moe_w3, moe_w2):
    B, T, D = x.shape
    Tc = ctx.shape[1]
    depth = w_in.shape[0]
    alpha = (2 * depth) ** 0.25
    assert B + 1 <= SUBLANES
    cc = jnp.zeros((SUBLANES, D), F32).at[:B].set(c).at[B].set(c_ctx)
    ada = ada_vectors(cc, ada_w, ada_b)
    tabs_lat = _rope_tables(T, True)
    tabs_ctx = _rope_tables(Tc, False)
    for l in range(depth):
        last = l == depth - 1
        p = _layer_params(l, w_in, rw_conv, rw_w0, rw_w_up, rw_a0, rw_a_up, rw_g_up, rw_k_k, rw_k_a, rw_r_k,
                          rw_gn_w, rw_gn_b, mla_q_norm, mla_w_uq, mla_kv_norm, mla_w_ukv, w_out)
        mods = ada[l].reshape(SUBLANES, 6, D)
        lat = [mods[:B, j][:, None, :] for j in range(6)]
        cx = [jnp.broadcast_to(mods[B, j][None, None, :], (B, 1, D)) for j in range(6)]
        sh_m, sc_m, g_m, sh_f, sc_f, g_f = lat
        csh_m, csc_m, cg_m, csh_f, csc_f, cg_f = cx
        ln1 = (ln1_w[l][None], ln1_b[l][None])
        ln2 = (ln2_w[l][None], ln2_b[l][None])

        pieces_l = in_projection(x, sh_m, sc_m, p["w_in"])
        pieces_c = in_projection(ctx, csh_m, csc_m, p["w_in"])
        mix_l, mix_c = _mixer(pieces_l, pieces_c, p, tabs_lat, tabs_ctx, not last)
        x = out_projection_ln(*mix_l, x, g_m, *ln1, p["w_out"], alpha)
        i = l // 2
        if l % 2 == 0:
            dense = (ffn_w1[i].astype(BF16), ffn_w3[i].astype(BF16), ffn_w2[i].astype(BF16))
            x = ffn_ln(x, sh_f, sc_f, g_f, *ln2, *dense, alpha)
        else:
            x = moe_ln(x, sh_f, sc_f, g_f, *ln2, moe_router[i], moe_w1[i], moe_w3[i], moe_w2[i], alpha)
        if not last:
            ctx = out_projection_ln(*mix_c, ctx, cg_m, *ln1, p["w_out"], alpha)
            if l % 2 == 0:
                ctx = ffn_ln(ctx, csh_f, csc_f, cg_f, *ln2, *dense, alpha)
            else:
                ctx = moe_ln(ctx, csh_f, csc_f, cg_f, *ln2, moe_router[i], moe_w1[i], moe_w3[i], moe_w2[i], alpha)
    return x
```

```python
import functools
import math

import numpy as np
import jax
import jax.numpy as jnp
from jax import lax
from jax.experimental import pallas as pl
from jax.experimental.pallas import tpu as pltpu

F32 = jnp.float32
BF16 = jnp.bfloat16
HI = lax.Precision.HIGHEST

LANES = 128
SUBLANES = 8
VMEM_LIMIT = 56 * 1024 * 1024

GRID_W = 64
RW_HEADS = 8
RW_HEAD = 64
RW_WIDTH = RW_HEADS * RW_HEAD
G_RANK = 128
LORA_RANK = 64
GN_EPS = 64e-5
FT_GROUP = 64
FT_WIDTH = 512
MLA_HEADS = 16
QK_NOPE = 64
QK_ROPE = 32
V_HEAD = 64
MLA_WIDTH = MLA_HEADS * V_HEAD
Q_RANK = 512
KV_RANK = 256
ROPE_AXIS = QK_ROPE // 2
ROPE_THETA = 10000.0
ATTN_SCALE = (QK_NOPE + QK_ROPE) ** -0.5
Q_SCALE = ATTN_SCALE * math.log2(math.e)
N_EXPERTS = 8
LN_EPS = 1e-5
MOD_EPS = 1e-6
RMS_EPS = 1e-6
CHUNK = 64
INV_BLOCK = 16
MOE_ROWS = 1024

NT = (((1,), (1,)), ((), ()))
TN = (((0,), (0,)), ((), ()))


def _cparams(*sem):
    return pltpu.CompilerParams(dimension_semantics=sem, vmem_limit_bytes=VMEM_LIMIT)


def _dot(a, b, prec=None):
    return jnp.dot(a, b, precision=prec, preferred_element_type=F32)


def _dot_split(x, w):
    hi = x.astype(BF16)
    r1 = x - hi.astype(F32)
    mid = r1.astype(BF16)
    lo = (r1 - mid.astype(F32)).astype(BF16)
    return _dot(hi, w) + _dot(mid, w) + _dot(lo, w)


def _dot3(a, b):
    a_hi = a.astype(BF16)
    b_hi = b.astype(BF16)
    a_lo = (a - a_hi.astype(F32)).astype(BF16)
    b_lo = (b - b_hi.astype(F32)).astype(BF16)
    return _dot(a_hi, b_hi) + _dot(a_hi, b_lo) + _dot(a_lo, b_hi)


def _standardize(x, eps):
    mu = jnp.mean(x, axis=-1, keepdims=True)
    xc = x - mu
    var = jnp.mean(xc * xc, axis=-1, keepdims=True)
    return xc * lax.rsqrt(var + eps)


def _tile(n, pref):
    t = min(n, pref)
    assert n % t == 0, (n, pref)
    return t


def _ada_kernel(c_ref, w_ref, b_ref, o_ref):
    c = c_ref[...]
    s = c * jax.nn.sigmoid(c)
    o_ref[0] = _dot(s, w_ref[0], HI) + b_ref[0]


def ada_vectors(cc, ada_w, ada_b):
    L, D, N6 = ada_w.shape
    tn = _tile(N6, 1024)
    return pl.pallas_call(
        _ada_kernel,
        grid=(L, N6 // tn),
        in_specs=[pl.BlockSpec((SUBLANES, D), lambda l, j: (0, 0)),
                  pl.BlockSpec((1, D, tn), lambda l, j: (l, 0, j)),
                  pl.BlockSpec((1, 1, tn), lambda l, j: (l, 0, j))],
        out_specs=pl.BlockSpec((1, SUBLANES, tn), lambda l, j: (l, 0, j)),
        out_shape=jax.ShapeDtypeStruct((L, SUBLANES, N6), F32),
        compiler_params=_cparams("parallel", "parallel"),
        name="ada_vectors",
    )(cc, ada_w, ada_b.reshape(L, 1, N6))


def _inproj_kernel(x_ref, sh_ref, sc_ref, *refs):
    nw = len(refs) // 2
    h = _standardize(x_ref[0], MOD_EPS) * (1.0 + sc_ref[0]) + sh_ref[0]
    hb = h.astype(BF16)
    for w_ref, o_ref in zip(refs[:nw], refs[nw:]):
        o_ref[0] = _dot(hb, w_ref[...])


def in_projection(x, shift, scale, weights):
    B, T, D = x.shape
    tm = _tile(T, 256)
    vec = pl.BlockSpec((1, 1, D), lambda b, i: (b, 0, 0))
    return pl.pallas_call(
        _inproj_kernel,
        grid=(B, T // tm),
        in_specs=[pl.BlockSpec((1, tm, D), lambda b, i: (b, i, 0)), vec, vec]
        + [pl.BlockSpec(w.shape, lambda b, i: (0, 0)) for w in weights],
        out_specs=[pl.BlockSpec((1, tm, w.shape[1]), lambda b, i: (b, i, 0)) for w in weights],
        out_shape=[jax.ShapeDtypeStruct((B, T, w.shape[1]), F32) for w in weights],
        compiler_params=_cparams("parallel", "parallel"),
        name="in_projection",
    )(x, shift, scale, *weights)


RW_COLS = 3 * RW_WIDTH + G_RANK + 4 * LANES


def _softplus(z):
    return jnp.maximum(z, 0.0) + jnp.log(1.0 + jnp.exp(-jnp.abs(z)))


def _rwkv_prep_kernel(x_ref, xp_ref, xn_ref, conv_ref, kk_ref, ka_ref, rk_ref, w0_ref, a0_ref,
                      wup_ref, aup_ref, gup_ref, e_ref,
                      r_o, v_o, kk_o, ld0_o, b0_o, kr0_o, ld1_o, b1_o, kr1_o, bonus_o, gate_o):
    i = pl.program_id(1)
    n = pl.num_programs(1)
    x = x_ref[0]
    W3 = 3 * RW_WIDTH
    raw = x[:, :W3]
    tm = raw.shape[0]
    row = lax.broadcasted_iota(jnp.int32, (tm, 1), 0)
    prev_row = jnp.where(i > 0, xp_ref[0, SUBLANES - 1:SUBLANES, :], 0.0)
    next_row = jnp.where(i < n - 1, xn_ref[0, 0:1, :], 0.0)
    xm = jnp.where(row == 0, prev_row, pltpu.roll(raw, 1, 0))
    xq = jnp.where(row == tm - 1, next_row, pltpu.roll(raw, tm - 1, 0))
    cw = conv_ref[...]
    y = xm * cw[0:1] + raw * cw[1:2] + xq * cw[2:3]
    r = y[:, :RW_WIDTH]
    k = y[:, RW_WIDTH:2 * RW_WIDTH]
    v = y[:, 2 * RW_WIDTH:W3]
    E = e_ref[...]
    kkv = k * kk_ref[...]
    kk = kkv / jnp.maximum(jnp.sqrt(_dot_split(kkv * kkv, E)), 1e-12)
    r_o[0] = r
    v_o[0] = v
    kk_o[0] = kk
    g_dn = x[:, W3:W3 + G_RANK]
    gate_o[0] = _dot(jax.nn.sigmoid(g_dn).astype(BF16), gup_ref[...])
    kr_sum = jnp.zeros_like(r)
    outs = ((ld0_o, b0_o, kr0_o), (ld1_o, b1_o, kr1_o))
    for d in range(2):
        base = W3 + G_RANK
        w_dn = x[:, base + d * LANES: base + (d + 1) * LANES]
        a_dn = x[:, base + (2 + d) * LANES: base + (3 + d) * LANES]
        z = w0_ref[d:d + 1, :] + _dot(jnp.tanh(w_dn).astype(BF16), wup_ref[d])
        logw = -_softplus(-z) - 0.5
        a = jax.nn.sigmoid(a0_ref[d:d + 1, :] + _dot(a_dn.astype(BF16), aup_ref[d]))
        kr = k * (1.0 + (a - 1.0) * ka_ref[...])
        ld_o, b_o, kr_o = outs[d]
        ld_o[0] = -jnp.exp(logw)
        b_o[0] = a * kk
        kr_o[0] = kr
        kr_sum = kr_sum + kr
    bonus_o[0] = _dot_split(r * kr_sum * rk_ref[...], E) * v


def rwkv_prep(rw, p):
    B, T, _ = rw.shape
    tm = _tile(T, 256)
    nh = tm // SUBLANES
    last = T // SUBLANES - 1
    W3 = 3 * RW_WIDTH
    full = lambda a: pl.BlockSpec(a.shape, lambda b, i: (0,) * a.ndim)
    params = [p["conv"], p["k_k"], p["k_a"], p["r_k"], p["w0"], p["a0"], p["w_up"], p["a_up"], p["g_up"], p["E"]]
    outs = pl.pallas_call(
        _rwkv_prep_kernel,
        grid=(B, T // tm),
        in_specs=[pl.BlockSpec((1, tm, RW_COLS), lambda b, i: (b, i, 0)),
                  pl.BlockSpec((1, SUBLANES, W3), lambda b, i: (b, jnp.maximum(i * nh - 1, 0), 0)),
                  pl.BlockSpec((1, SUBLANES, W3), lambda b, i: (b, jnp.minimum((i + 1) * nh, last), 0))]
        + [full(a) for a in params],
        out_specs=[pl.BlockSpec((1, tm, RW_WIDTH), lambda b, i: (b, i, 0))] * 11,
        out_shape=[jax.ShapeDtypeStruct((B, T, RW_WIDTH), F32)] * 11,
        compiler_params=_cparams("parallel", "parallel"),
        name="rwkv_prep",
    )(rw, rw, rw, *params)
    return outs


PAIR = 2 * RW_HEAD
N_PAIRS = RW_HEADS // 2
CHUNK_GROUP = 4


def _pair_masks(reverse):
    i = lax.broadcasted_iota(jnp.int32, (PAIR, PAIR), 0)
    j = lax.broadcasted_iota(jnp.int32, (PAIR, PAIR), 1)
    same = (i // CHUNK) == (j // CHUNK)
    strict = same & ((j > i) if reverse else (j < i))
    incl = same & ((j >= i) if reverse else (j <= i))
    blk = (i // INV_BLOCK) == (j // INV_BLOCK)
    eye = jnp.where(i == j, 1.0, 0.0).astype(F32)
    t = lax.broadcasted_iota(jnp.int32, (CHUNK, CHUNK), 0)
    u = lax.broadcasted_iota(jnp.int32, (CHUNK, CHUNK), 1)
    tri = jnp.where((u >= t) if reverse else (u <= t), 1.0, 0.0).astype(F32)
    first = lax.broadcasted_iota(jnp.int32, (1, PAIR), 1) < RW_HEAD
    return strict, incl, blk, eye, tri, first


def _pair_chunk_math(ld, r, v, kk, b, kr, masks):
    strict, incl, blk, eye, tri, first = masks
    P = ld.shape[0]
    ein = lambda spec, a, c: jnp.einsum(spec, a, c, preferred_element_type=F32)
    tri_b = jnp.broadcast_to(tri.astype(BF16), (P, CHUNK, CHUNK))
    ld_hi = ld.astype(BF16)
    ld_r = ld - ld_hi.astype(F32)
    ld_mid = ld_r.astype(BF16)
    ld_lo = (ld_r - ld_mid.astype(F32)).astype(BF16)
    Lc = ein("pct,ptk->pck", tri_b, ld_hi) + ein("pct,ptk->pck", tri_b, ld_mid) + ein("pct,ptk->pck", tri_b, ld_lo)
    Lx = Lc - ld
    Lt = jnp.sum(ld, axis=1, keepdims=True)
    ginv = jnp.exp(-Lc)
    gout = jnp.exp(Lt - Lc)
    stack = lambda x: jnp.concatenate([jnp.where(first, x, 0.0), jnp.where(first, 0.0, x)], axis=1).astype(BF16)
    twice = lambda x: jnp.concatenate([x, x], axis=1).astype(BF16)
    Xk = stack(kk * jnp.exp(Lx))
    Xr = stack(r * jnp.exp(Lc))
    Vs = stack(v)
    Bs = stack(b * gout)
    Ks = stack(kr * gout)
    mm = lambda a, c: ein("pij,pjk->pik", a.astype(BF16), c.astype(BF16))
    nt = lambda a, c: ein("pik,pjk->pij", a, c)
    tn = lambda a, c: ein("pji,pjk->pik", a, c.astype(BF16))
    XX = jnp.concatenate([Xk, Xr], axis=1)
    Mb = nt(XX, twice(b * ginv))
    Mk = nt(XX, twice(kr * ginv))
    Mab = jnp.where(strict, Mb[:, :PAIR], 0.0)
    Arb = jnp.where(incl, Mb[:, PAIR:], 0.0)
    Mak = jnp.where(strict, Mk[:, :PAIR], 0.0)
    Ark = jnp.where(incl, Mk[:, PAIR:], 0.0)
    Nd = jnp.where(blk, Mab, 0.0)
    No = Mab - Nd
    N2 = mm(Nd, Nd)
    N4 = mm(N2, N2)
    N8 = mm(N4, N4)
    Td = mm(mm(mm(eye - Nd, eye + N2), eye + N4), eye + N8)
    M2 = mm(Td, No)
    Tm = mm(mm(eye - M2, eye + mm(M2, M2)), Td)
    P1 = mm(Tm, Xk)
    P2 = mm(Tm, mm(Mak, Vs))
    Q1s = Xr.astype(F32) - mm(Arb, P1)
    Yis = mm(Ark, Vs) - mm(Arb, P2)
    G = eye * jnp.exp(Lt) - tn(Bs, P1)
    H = tn(Ks, Vs) - tn(Bs, P2)
    return Q1s[:, :CHUNK] + Q1s[:, CHUNK:], Yis[:, :CHUNK] + Yis[:, CHUNK:], G, H


def _rwkv_chunk_kernel(ld_ref, r_ref, v_ref, kk_ref, b_ref, kr_ref, q_o, yi_o, g_o, h_o, *, nch, reverse):
    masks = _pair_masks(reverse)
    group = min(CHUNK_GROUP, nch)

    def body(i, carry):
        c0 = i * group
        rows = [pl.ds(pl.multiple_of((c0 + u) * CHUNK, CHUNK), CHUNK) for u in range(group)]
        get = lambda ref: jnp.stack([ref[0, rows[u], j * PAIR:(j + 1) * PAIR]
                                     for u in range(group) for j in range(N_PAIRS)])
        Q1, Yi, G, H = _pair_chunk_math(get(ld_ref), get(r_ref), get(v_ref), get(kk_ref), get(b_ref), get(kr_ref),
                                        masks)
        for u in range(group):
            for j in range(N_PAIRS):
                q_o[0, rows[u], j * PAIR:(j + 1) * PAIR] = Q1[u * N_PAIRS + j]
                yi_o[0, rows[u], j * PAIR:(j + 1) * PAIR] = Yi[u * N_PAIRS + j]
        mats = pl.ds(c0 * N_PAIRS, group * N_PAIRS)
        g_o[0, mats] = G
        h_o[0, mats] = H
        return carry

    lax.fori_loop(0, nch // group, body, 0)


def rwkv_chunks(ld, r, v, kk, b, kr, reverse):
    B, T, W = ld.shape
    tt = _tile(T, 4 * CHUNK)
    nch = tt // CHUNK
    tok = pl.BlockSpec((1, tt, W), lambda bb, i: (bb, i, 0))
    mat = pl.BlockSpec((1, nch * N_PAIRS, PAIR, PAIR), lambda bb, i: (bb, i, 0, 0))
    mats = jax.ShapeDtypeStruct((B, T // CHUNK * N_PAIRS, PAIR, PAIR), F32)
    return pl.pallas_call(
        functools.partial(_rwkv_chunk_kernel, nch=nch, reverse=reverse),
        grid=(B, T // tt),
        in_specs=[tok] * 6,
        out_specs=[tok, tok, mat, mat],
        out_shape=[jax.ShapeDtypeStruct((B, T, W), F32)] * 2 + [mats, mats],
        compiler_params=_cparams("parallel", "parallel"),
        name="rwkv_chunks_bwd" if reverse else "rwkv_chunks_fwd",
    )(ld, r, v, kk, b, kr)


def _rwkv_seq_kernel(q_ref, yi_ref, g_ref, h_ref, s0_ref, y_o, sf_o, s_scr, *, nch, reverse):
    @pl.when(pl.program_id(1) == 0)
    def _():
        s_scr[...] = s0_ref[0]

    def body(cc, carry):
        c = (nch - 1 - cc) if reverse else cc
        rows = pl.ds(pl.multiple_of(c * CHUNK, CHUNK), CHUNK)
        for j in range(N_PAIRS):
            lanes = slice(j * PAIR, (j + 1) * PAIR)
            S = s_scr[j]
            y_o[0, rows, lanes] = _dot3(q_ref[0, rows, lanes], S) + yi_ref[0, rows, lanes]
            s_scr[j] = _dot3(g_ref[0, c * N_PAIRS + j], S) + h_ref[0, c * N_PAIRS + j]
        return carry

    lax.fori_loop(0, nch, body, 0)
    sf_o[0] = s_scr[...]


def rwkv_sequential(q1, yi, g, hm, s0, reverse):
    B, T, W = q1.shape
    tt = _tile(T, 8 * CHUNK)
    n = T // tt
    nch = tt // CHUNK
    step = (lambda i: n - 1 - i) if reverse else (lambda i: i)
    tok = pl.BlockSpec((1, tt, W), lambda bb, i: (bb, step(i), 0))
    mat = pl.BlockSpec((1, nch * N_PAIRS, PAIR, PAIR), lambda bb, i: (bb, step(i), 0, 0))
    sspec = pl.BlockSpec((1, N_PAIRS, PAIR, PAIR), lambda bb, i: (bb, 0, 0, 0))
    return pl.pallas_call(
        functools.partial(_rwkv_seq_kernel, nch=nch, reverse=reverse),
        grid=(B, n),
        in_specs=[tok, tok, mat, mat, sspec],
        out_specs=[tok, sspec],
        out_shape=[jax.ShapeDtypeStruct((B, T, W), F32), jax.ShapeDtypeStruct((B, N_PAIRS, PAIR, PAIR), F32)],
        scratch_shapes=[pltpu.VMEM((N_PAIRS, PAIR, PAIR), F32)],
        compiler_params=_cparams("parallel", "arbitrary"),
        name="rwkv_seq_bwd" if reverse else "rwkv_seq_fwd",
    )(q1, yi, g, hm, s0)


def _rwkv_readout_kernel(yf_ref, yb_ref, bonus_ref, gate_ref, gw_ref, gb_ref, e_ref, o_ref):
    E = e_ref[...]
    ys = yf_ref[0] + yb_ref[0]
    yc = ys - _dot_split(ys, E) * (1.0 / RW_HEAD)
    var = _dot_split(yc * yc, E) * (1.0 / RW_HEAD)
    yn = yc * lax.rsqrt(var + GN_EPS) * gw_ref[...] + gb_ref[...]
    o_ref[0] = (yn + bonus_ref[0]) * gate_ref[0]


def rwkv_readout(yf, yb, bonus, gate, p):
    B, T, W = yf.shape
    tm = _tile(T, 512)
    spec = pl.BlockSpec((1, tm, W), lambda b, i: (b, i, 0))
    full = lambda a: pl.BlockSpec(a.shape, lambda b, i: (0,) * a.ndim)
    return pl.pallas_call(
        _rwkv_readout_kernel,
        grid=(B, T // tm),
        in_specs=[spec] * 4 + [full(p["gn_w"]), full(p["gn_b"]), full(p["E"])],
        out_specs=spec,
        out_shape=jax.ShapeDtypeStruct((B, T, W), F32),
        compiler_params=_cparams("parallel", "parallel"),
        name="rwkv_readout",
    )(yf, yb, bonus, gate, p["gn_w"], p["gn_b"], p["E"])


def rwkv_branch(rw_lat, rw_ctx, p, emit_ctx):
    prep_l = rwkv_prep(rw_lat, p)
    prep_c = rwkv_prep(rw_ctx, p)
    B = rw_lat.shape[0]
    s_zero = jnp.zeros((B, N_PAIRS, PAIR, PAIR), F32)

    def scans(prep, s0s):
        r, v, kk = prep[:3]
        ys, finals = [], []
        for d in range(2):
            ld, b, kr = prep[3 + 3 * d: 6 + 3 * d]
            q1, yi, g, hm = rwkv_chunks(ld, r, v, kk, b, kr, reverse=bool(d))
            y, sf = rwkv_sequential(q1, yi, g, hm, s0s[d], reverse=bool(d))
            ys.append(y)
            finals.append(sf)
        return ys, finals

    ys_c, fin_c = scans(prep_c, (s_zero, s_zero))
    ys_l, _ = scans(prep_l, fin_c)
    out_l = rwkv_readout(ys_l[0], ys_l[1], prep_l[9], prep_l[10], p)
    out_c = rwkv_readout(ys_c[0], ys_c[1], prep_c[9], prep_c[10], p) if emit_ctx else None
    return out_l, out_c


def _dft_mats(n):
    a = 2.0 * np.pi * np.outer(np.arange(n), np.arange(n)) / n
    return np.cos(a), np.sin(a)


def _fft1_kernel(u_ref, c_ref, s_ref, twc_ref, tws_ref, ar_o, ai_o, *, tn2, ch):
    U = u_ref[0]
    Ar = _dot3(c_ref[...], U)
    Ai = -_dot3(s_ref[...], U)
    twc = twc_ref[0]
    tws = tws_ref[0]
    for j in range(tn2):
        ct = twc[:, j:j + 1]
        st = tws[:, j:j + 1]
        a_r = Ar[:, j * ch:(j + 1) * ch]
        a_i = Ai[:, j * ch:(j + 1) * ch]
        ar_o[0, j] = a_r * ct + a_i * st
        ai_o[0, j] = a_i * ct - a_r * st


def _fft2_kernel(ar_ref, ai_ref, c_ref, s_ref, cc_ref, sc_ref, o_ref):
    Ar = ar_ref[0]
    Ai = ai_ref[0]
    C = c_ref[...]
    S = s_ref[...]
    Yr = _dot3(C, Ar) + _dot3(S, Ai)
    Yi = _dot3(C, Ai) - _dot3(S, Ar)
    Cc = cc_ref[...]
    Sc = sc_ref[...]
    for m in range(Ar.shape[1] // LANES):
        sl = slice(m * LANES, (m + 1) * LANES)
        o_ref[0, :, sl] = _dot3(Yr[:, sl], Cc) + _dot3(Yi[:, sl], Sc)


def fourier_mixer(u):
    B, T, ch = u.shape
    lg = int(round(math.log2(T)))
    assert 1 << lg == T
    N1 = 1 << ((lg + 1) // 2)
    N2 = T // N1
    c1, s1 = _dft_mats(N1)
    c2, s2 = _dft_mats(N2)
    tw = 2.0 * np.pi * np.outer(np.arange(N1), np.arange(N2)) / T
    tn2 = min(SUBLANES, N2)
    nj = N2 // tn2
    twc = np.cos(tw).reshape(N1, nj, tn2).transpose(1, 0, 2)
    tws = np.sin(tw).reshape(N1, nj, tn2).transpose(1, 0, 2)
    cg, sg = _dft_mats(FT_GROUP)
    scale = 1.0 / math.sqrt(T * FT_GROUP)
    eye2 = np.eye(LANES // FT_GROUP)
    cc = np.kron(eye2, cg) * scale
    sc = np.kron(eye2, sg) * scale
    f = lambda a: jnp.asarray(a, F32)
    full2 = lambda n, m: pl.BlockSpec((n, m), lambda b, j: (0, 0))

    ar, ai = pl.pallas_call(
        functools.partial(_fft1_kernel, tn2=tn2, ch=ch),
        grid=(B, nj),
        in_specs=[pl.BlockSpec((1, N1, tn2 * ch), lambda b, j: (b, 0, j)),
                  full2(N1, N1), full2(N1, N1),
                  pl.BlockSpec((1, N1, tn2), lambda b, j: (j, 0, 0)),
                  pl.BlockSpec((1, N1, tn2), lambda b, j: (j, 0, 0))],
        out_specs=[pl.BlockSpec((1, tn2, N1, ch), lambda b, j: (b, j, 0, 0))] * 2,
        out_shape=[jax.ShapeDtypeStruct((B, N2, N1, ch), F32)] * 2,
        compiler_params=_cparams("parallel", "parallel"),
        name="fft_stage1",
    )(u.reshape(B, N1, N2 * ch), f(c1), f(s1), f(twc), f(tws))

    tk1 = min(SUBLANES, N1)
    blk = pl.BlockSpec((1, N2, tk1 * ch), lambda b, j: (b, 0, j))
    out = pl.pallas_call(
        _fft2_kernel,
        grid=(B, N1 // tk1),
        in_specs=[blk, blk, full2(N2, N2), full2(N2, N2), full2(LANES, LANES), full2(LANES, LANES)],
        out_specs=blk,
        out_shape=jax.ShapeDtypeStruct((B, N2, N1 * ch), F32),
        compiler_params=_cparams("parallel", "parallel"),
        name="fft_stage2",
    )(ar.reshape(B, N2, N1 * ch), ai.reshape(B, N2, N1 * ch), f(c2), f(s2), f(cc), f(sc))
    return out.reshape(B, T, ch)


HEAD_SLAB = LANES
ROPE_SHIFT = HEAD_SLAB - QK_ROPE
ONES_ROWS = 16
V_ROWS = V_HEAD + ONES_ROWS
ATTN_UNROLL = 6
ATTN_TILES = 4


def _rms(x, w):
    return x * lax.rsqrt(jnp.mean(x * x, axis=-1, keepdims=True) + RMS_EPS) * w


def _qproj_kernel(cq_ref, nw_ref, w_ref, ct_ref, st_ref, q_o):
    q = _dot(_rms(cq_ref[0], nw_ref[...]).astype(BF16), w_ref[...])
    ct = ct_ref[...]
    st = st_ref[...]
    for h in range(MLA_HEADS):
        sl = slice(h * HEAD_SLAB, (h + 1) * HEAD_SLAB)
        s = q[:, sl]
        q_o[0, :, sl] = ((s * ct + pltpu.roll(s, ROPE_SHIFT, 1) * st) * Q_SCALE).astype(BF16)


def _kvproj_kernel(ckv_ref, nw_ref, wk_ref, wvt_ref, ct_ref, st_ref, k_o, vt_o):
    x = ckv_ref[0]
    n = _rms(x[:, :KV_RANK], nw_ref[...]).astype(BF16)
    rs = x[:, KV_RANK:KV_RANK + HEAD_SLAB]
    rope = rs * ct_ref[...] + pltpu.roll(rs, ROPE_SHIFT, 1) * st_ref[...]
    kn = _dot(n, wk_ref[...])
    for h in range(MLA_HEADS):
        sl = slice(h * HEAD_SLAB, (h + 1) * HEAD_SLAB)
        k_o[0, :, sl] = (kn[:, sl] + rope).astype(BF16)
    vt = lax.dot_general(wvt_ref[...], n, NT, preferred_element_type=F32)
    row = lax.broadcasted_iota(jnp.int32, vt.shape, 0)
    vt_o[0] = jnp.where(row % V_ROWS >= V_HEAD, 1.0, vt).astype(BF16)


def q_projection(cq, nw, w, ct, st):
    B, T, R = cq.shape
    tm = _tile(T, 256)
    W = MLA_HEADS * HEAD_SLAB
    tab = pl.BlockSpec((tm, HEAD_SLAB), lambda b, i: (i, 0))
    return pl.pallas_call(
        _qproj_kernel,
        grid=(B, T // tm),
        in_specs=[pl.BlockSpec((1, tm, R), lambda b, i: (b, i, 0)),
                  pl.BlockSpec(nw.shape, lambda b, i: (0, 0)),
                  pl.BlockSpec(w.shape, lambda b, i: (0, 0)), tab, tab],
        out_specs=pl.BlockSpec((1, tm, W), lambda b, i: (b, i, 0)),
        out_shape=jax.ShapeDtypeStruct((B, T, W), BF16),
        compiler_params=_cparams("parallel", "parallel"),
        name="q_projection",
    )(cq, nw, w, ct, st)


def kv_projection(ckv, nw, wk, wvt, ct, st):
    B, T, R = ckv.shape
    tm = _tile(T, 256)
    W = MLA_HEADS * HEAD_SLAB
    tab = pl.BlockSpec((tm, HEAD_SLAB), lambda b, i: (i, 0))
    return pl.pallas_call(
        _kvproj_kernel,
        grid=(B, T // tm),
        in_specs=[pl.BlockSpec((1, tm, R), lambda b, i: (b, i, 0)),
                  pl.BlockSpec(nw.shape, lambda b, i: (0, 0)),
                  pl.BlockSpec(wk.shape, lambda b, i: (0, 0)),
                  pl.BlockSpec(wvt.shape, lambda b, i: (0, 0)), tab, tab],
        out_specs=[pl.BlockSpec((1, tm, W), lambda b, i: (b, i, 0)),
                   pl.BlockSpec((1, MLA_HEADS * V_ROWS, tm), lambda b, i: (b, 0, i))],
        out_shape=[jax.ShapeDtypeStruct((B, T, W), BF16), jax.ShapeDtypeStruct((B, MLA_HEADS * V_ROWS, T), BF16)],
        compiler_params=_cparams("parallel", "parallel"),
        name="kv_projection",
    )(ckv, nw, wk, wvt, ct, st)


def _attn_kernel(q_ref, k_ref, vt_ref, o_ref, m_scr, acc_scr, sa_scr, sb_scr, ma_scr, mb_scr, *, tq, tkc, nkc):
    ntile = q_ref.shape[1] // tq
    bufs = ((sa_scr, ma_scr), (sb_scr, mb_scr))

    def chunk(c):
        return pl.ds(pl.multiple_of(c * tkc, tkc), tkc)

    def scores(t, c, dst, dmax):
        for hh in range(2):
            q = q_ref[0, t * tq:(t + 1) * tq, hh * HEAD_SLAB:(hh + 1) * HEAD_SLAB]
            kc = k_ref[0, chunk(c), hh * HEAD_SLAB:(hh + 1) * HEAD_SLAB]
            s = lax.dot_general(kc, q, NT, preferred_element_type=F32)
            dst[hh] = s
            dmax[hh] = jnp.max(s, axis=0, keepdims=True)

    def consume(src, smax, c):
        for hh in range(2):
            m_old = m_scr[hh]
            m_new = jnp.maximum(m_old, smax[hh])
            pr = jnp.exp2(src[hh] - m_new).astype(BF16)
            alpha = jnp.exp2(m_old - m_new)
            m_scr[hh] = m_new
            rows = slice(hh * V_ROWS, (hh + 1) * V_ROWS)
            acc_scr[rows, :] = alpha * acc_scr[rows, :] + _dot(vt_ref[0, rows, chunk(c)], pr)

    unroll = ATTN_UNROLL if nkc > 2 * ATTN_UNROLL else 1
    ngroups = (nkc - 1) // (2 * unroll)
    done = 2 * unroll * ngroups
    first = 0
    scores(0, 0, *bufs[first])
    for t in range(ntile):
        m_scr[...] = jnp.full(m_scr.shape, -jnp.inf, F32)
        acc_scr[...] = jnp.zeros(acc_scr.shape, F32)
        cur, nxt = bufs[first], bufs[1 - first]

        def body(i, carry, t=t, cur=cur, nxt=nxt):
            c = 2 * unroll * i
            for _ in range(unroll):
                scores(t, c + 1, *nxt)
                consume(*cur, c)
                scores(t, c + 2, *cur)
                consume(*nxt, c + 1)
                c = c + 2
            return carry

        lax.fori_loop(0, ngroups, body, 0)
        for cc in range(done, nkc):
            cur, nxt = bufs[first], bufs[1 - first]
            if cc + 1 < nkc:
                scores(t, cc + 1, *nxt)
            elif t + 1 < ntile:
                scores(t + 1, 0, *nxt)
            consume(*cur, cc)
            first = 1 - first
        outs = []
        for hh in range(2):
            base = hh * V_ROWS
            outs.append(acc_scr[base:base + V_HEAD, :] / acc_scr[base + V_HEAD:base + V_HEAD + 1, :])
        o_ref[0, t * tq:(t + 1) * tq, :] = jnp.concatenate(outs, axis=0).T.astype(o_ref.dtype)


def attention(q, k, vt):
    B, T, _ = q.shape
    Tk = k.shape[1]
    tq = _tile(T, 256)
    tb = _tile(T, ATTN_TILES * tq)
    tkc = next(c for c in (640, 512, 256, 128) if Tk % c == 0)
    hp = MLA_HEADS // 2
    return pl.pallas_call(
        functools.partial(_attn_kernel, tq=tq, tkc=tkc, nkc=Tk // tkc),
        grid=(B, hp, T // tb),
        in_specs=[pl.BlockSpec((1, tb, 2 * HEAD_SLAB), lambda b, h, i: (b, i, h)),
                  pl.BlockSpec((1, Tk, 2 * HEAD_SLAB), lambda b, h, i: (b, 0, h)),
                  pl.BlockSpec((1, 2 * V_ROWS, Tk), lambda b, h, i: (b, h, 0))],
        out_specs=pl.BlockSpec((1, tb, 2 * V_HEAD), lambda b, h, i: (b, i, h)),
        out_shape=jax.ShapeDtypeStruct((B, T, MLA_WIDTH), BF16),
        scratch_shapes=[pltpu.VMEM((2, 1, tq), F32), pltpu.VMEM((2 * V_ROWS, tq), F32),
                        pltpu.VMEM((2, tkc, tq), F32), pltpu.VMEM((2, tkc, tq), F32),
                        pltpu.VMEM((2, 1, tq), F32), pltpu.VMEM((2, 1, tq), F32)],
        compiler_params=_cparams("parallel", "parallel", "arbitrary"),
        name="mla_attention",
    )(q, k, vt)


def _outproj_kernel(rw_ref, ft_ref, att_ref, x_ref, g_ref, lw_ref, lb_ref, w_ref, o_ref, *, alpha):
    w = w_ref
    mix = (_dot(rw_ref[0].astype(BF16), w[0:RW_WIDTH, :])
           + _dot(ft_ref[0].astype(BF16), w[RW_WIDTH:RW_WIDTH + FT_WIDTH, :])
           + _dot(att_ref[0], w[RW_WIDTH + FT_WIDTH:, :]))
    z = alpha * x_ref[0] + g_ref[0] * mix
    o_ref[0] = _standardize(z, LN_EPS) * lw_ref[...] + lb_ref[...]


def out_projection_ln(rw, ft, att, x, gate, ln_w, ln_b, w_out, alpha):
    B, T, D = x.shape
    tm = _tile(T, 256)
    tok = lambda n: pl.BlockSpec((1, tm, n), lambda b, i: (b, i, 0))
    row = pl.BlockSpec((1, D), lambda b, i: (0, 0))
    return pl.pallas_call(
        functools.partial(_outproj_kernel, alpha=alpha),
        grid=(B, T // tm),
        in_specs=[tok(RW_WIDTH), tok(FT_WIDTH), tok(MLA_WIDTH), tok(D),
                  pl.BlockSpec((1, 1, D), lambda b, i: (b, 0, 0)), row, row,
                  pl.BlockSpec(w_out.shape, lambda b, i: (0, 0))],
        out_specs=tok(D),
        out_shape=jax.ShapeDtypeStruct((B, T, D), F32),
        compiler_params=_cparams("parallel", "parallel"),
        name="out_projection_ln",
    )(rw, ft, att, x, gate, ln_w, ln_b, w_out)


def _ffn_kernel(x_ref, sh_ref, sc_ref, g_ref, lw_ref, lb_ref, w1_ref, w3_ref, w2_ref, o_ref, h_scr, *, alpha):
    f = pl.program_id(2)

    @pl.when(f == 0)
    def _():
        h = _standardize(x_ref[0], MOD_EPS) * (1.0 + sc_ref[0]) + sh_ref[0]
        h_scr[...] = h.astype(BF16)
        o_ref[0] = jnp.zeros(o_ref.shape[1:], F32)

    hb = h_scr[...]
    a = _dot(hb, w1_ref[...])
    b = _dot(hb, w3_ref[...])
    o_ref[0] += _dot((a * jax.nn.sigmoid(a) * b).astype(BF16), w2_ref[...])

    @pl.when(f == pl.num_programs(2) - 1)
    def _():
        z = alpha * x_ref[0] + g_ref[0] * o_ref[0]
        o_ref[0] = _standardize(z, LN_EPS) * lw_ref[...] + lb_ref[...]


def ffn_ln(x, shift, scale, gate, ln_w, ln_b, w1, w3, w2, alpha):
    B, T, D = x.shape
    F = w1.shape[1]
    tm = _tile(T, 1024)
    tf = _tile(F, 256)
    tok = pl.BlockSpec((1, tm, D), lambda b, i, f: (b, i, 0))
    vec = pl.BlockSpec((1, 1, D), lambda b, i, f: (b, 0, 0))
    row = pl.BlockSpec((1, D), lambda b, i, f: (0, 0))
    return pl.pallas_call(
        functools.partial(_ffn_kernel, alpha=alpha),
        grid=(B, T // tm, F // tf),
        in_specs=[tok, vec, vec, vec, row, row,
                  pl.BlockSpec((D, tf), lambda b, i, f: (0, f)),
                  pl.BlockSpec((D, tf), lambda b, i, f: (0, f)),
                  pl.BlockSpec((tf, D), lambda b, i, f: (f, 0))],
        out_specs=tok,
        out_shape=jax.ShapeDtypeStruct((B, T, D), F32),
        scratch_shapes=[pltpu.VMEM((tm, D), BF16)],
        compiler_params=_cparams("parallel", "parallel", "arbitrary"),
        name="ffn_ln",
    )(x, shift, scale, gate, ln_w, ln_b, w1, w3, w2)


def _router_kernel(x_ref, sh_ref, sc_ref, wr_ref, h_o, ti_o, tg_o):
    h = _standardize(x_ref[0], MOD_EPS) * (1.0 + sc_ref[0]) + sh_ref[0]
    h_o[0] = h.astype(BF16)
    logits = _dot(h, wr_ref[...], HI)
    lane = lax.broadcasted_iota(jnp.int32, logits.shape, 1)
    neg = jnp.float32(-jnp.inf)
    logits = jnp.where(lane < N_EXPERTS, logits, neg)
    m1 = jnp.max(logits, axis=-1, keepdims=True)
    i1 = jnp.min(jnp.where(logits == m1, lane, LANES), axis=-1, keepdims=True)
    rest = jnp.where(lane == i1, neg, logits)
    m2 = jnp.max(rest, axis=-1, keepdims=True)
    i2 = jnp.min(jnp.where(rest == m2, lane, LANES), axis=-1, keepdims=True)
    e = jnp.exp(m2 - m1)
    g1 = 1.0 / (1.0 + e)
    g2 = e / (1.0 + e)
    ti_o[0] = jnp.where(lane == 0, i1, jnp.where(lane == 1, i2, 0))
    tg_o[0] = jnp.where(lane == 0, g1, jnp.where(lane == 1, g2, 0.0))


def moe_router(x, shift, scale, wr):
    B, T, D = x.shape
    tm = _tile(T, 512)
    tok = lambda n: pl.BlockSpec((1, tm, n), lambda b, i: (b, i, 0))
    vec = pl.BlockSpec((1, 1, D), lambda b, i: (b, 0, 0))
    return pl.pallas_call(
        _router_kernel,
        grid=(B, T // tm),
        in_specs=[tok(D), vec, vec, pl.BlockSpec(wr.shape, lambda b, i: (0, 0))],
        out_specs=[tok(D), tok(LANES), tok(LANES)],
        out_shape=[jax.ShapeDtypeStruct((B, T, D), BF16), jax.ShapeDtypeStruct((B, T, LANES), jnp.int32),
                   jax.ShapeDtypeStruct((B, T, LANES), F32)],
        compiler_params=_cparams("parallel", "parallel"),
        name="moe_router",
    )(x, shift, scale, wr)


def _expert_kernel(be_ref, bv_ref, x_ref, w1_ref, w3_ref, w2_ref, o_ref):
    blk = pl.program_id(0)
    f = pl.program_id(1)
    valid = bv_ref[blk] > 0

    @pl.when(f == 0)
    def _():
        o_ref[...] = jnp.zeros(o_ref.shape, F32)

    @pl.when(valid)
    def _():
        xb = x_ref[...]
        a = _dot(xb, w1_ref[0].astype(BF16))
        b = _dot(xb, w3_ref[0].astype(BF16))
        o_ref[...] += _dot((a * jax.nn.sigmoid(a) * b).astype(BF16), w2_ref[0].astype(BF16))


def expert_ffn(xg, blk_expert, blk_valid, w1, w3, w2):
    P, D = xg.shape
    F = w1.shape[2]
    tf = _tile(F, 512)
    nblk = P // MOE_ROWS
    grid_spec = pltpu.PrefetchScalarGridSpec(
        num_scalar_prefetch=2,
        grid=(nblk, F // tf),
        in_specs=[pl.BlockSpec((MOE_ROWS, D), lambda i, f, be, bv: (i, 0)),
                  pl.BlockSpec((1, D, tf), lambda i, f, be, bv: (be[i], 0, jnp.where(bv[i] > 0, f, 0))),
                  pl.BlockSpec((1, D, tf), lambda i, f, be, bv: (be[i], 0, jnp.where(bv[i] > 0, f, 0))),
                  pl.BlockSpec((1, tf, D), lambda i, f, be, bv: (be[i], jnp.where(bv[i] > 0, f, 0), 0))],
        out_specs=pl.BlockSpec((MOE_ROWS, D), lambda i, f, be, bv: (i, 0)),
    )
    return pl.pallas_call(
        _expert_kernel,
        grid_spec=grid_spec,
        out_shape=jax.ShapeDtypeStruct((P, D), F32),
        compiler_params=_cparams("parallel", "arbitrary"),
        name="expert_ffn",
    )(blk_expert, blk_valid, xg, w1, w3, w2)


def _combine_ln_kernel(x_ref, y0_ref, y1_ref, tg_ref, g_ref, lw_ref, lb_ref, o_ref, *, alpha):
    tg = tg_ref[0]
    y = tg[:, 0:1] * y0_ref[0] + tg[:, 1:2] * y1_ref[0]
    z = alpha * x_ref[0] + g_ref[0] * y
    o_ref[0] = _standardize(z, LN_EPS) * lw_ref[...] + lb_ref[...]


def combine_ln(x, y0, y1, tg, gate, ln_w, ln_b, alpha):
    B, T, D = x.shape
    tm = _tile(T, 512)
    tok = pl.BlockSpec((1, tm, D), lambda b, i: (b, i, 0))
    row = pl.BlockSpec((1, D), lambda b, i: (0, 0))
    return pl.pallas_call(
        functools.partial(_combine_ln_kernel, alpha=alpha),
        grid=(B, T // tm),
        in_specs=[tok, tok, tok, pl.BlockSpec((1, tm, LANES), lambda b, i: (b, i, 0)),
                  pl.BlockSpec((1, 1, D), lambda b, i: (b, 0, 0)), row, row],
        out_specs=tok,
        out_shape=jax.ShapeDtypeStruct((B, T, D), F32),
        compiler_params=_cparams("parallel", "parallel"),
        name="combine_ln",
    )(x, y0, y1, tg, gate, ln_w, ln_b)


def moe_ln(x, shift, scale, gate, ln_w, ln_b, router, w1, w3, w2, alpha):
    B, T, D = x.shape
    N = B * T
    E = router.shape[1]
    assert E == N_EXPERTS
    wr = jnp.zeros((D, LANES), F32).at[:, :E].set(router)
    h, ti, tg = moe_router(x, shift, scale, wr)
    top_i = ti.reshape(N, LANES)[:, :2]
    e_flat = top_i.reshape(-1)
    onehot = (e_flat[:, None] == jnp.arange(E, dtype=jnp.int32)[None, :]).astype(jnp.int32)
    rank = jnp.sum((jnp.cumsum(onehot, axis=0) - onehot) * onehot, axis=1)
    counts = jnp.sum(onehot, axis=0)
    padded = (counts + MOE_ROWS - 1) // MOE_ROWS * MOE_ROWS
    p_ends = jnp.cumsum(padded)
    p_starts = p_ends - padded
    dest = p_starts[e_flat] + rank
    P = -(-(2 * N) // MOE_ROWS) * MOE_ROWS + E * MOE_ROWS
    nblk = P // MOE_ROWS
    tok_flat = jnp.repeat(jnp.arange(N, dtype=jnp.int32), 2)
    slot_tok = jnp.zeros((P,), jnp.int32).at[dest].set(tok_flat)
    blk_start = jnp.arange(nblk, dtype=jnp.int32) * MOE_ROWS
    blk_expert = jnp.minimum(jnp.searchsorted(p_ends, blk_start, side="right"), E - 1).astype(jnp.int32)
    blk_valid = (blk_start < p_ends[-1]).astype(jnp.int32)
    xg = h.reshape(N, D)[slot_tok]
    yg = expert_ffn(xg, blk_expert, blk_valid, w1, w3, w2)
    d2 = dest.reshape(N, 2)
    y0 = yg[d2[:, 0]].reshape(B, T, D)
    y1 = yg[d2[:, 1]].reshape(B, T, D)
    return combine_ln(x, y0, y1, tg, gate, ln_w, ln_b, alpha)


def _rope_partner(w):
    half = ROPE_AXIS // 2
    idx = np.arange(QK_ROPE)
    first = (idx % ROPE_AXIS) < half
    src = np.where(first, idx + half, idx - half)
    sign = np.where(first, -1.0, 1.0).astype(np.float32)
    return w[:, src] * sign


def _layer_params(l, w_in, rw_conv, rw_w0, rw_w_up, rw_a0, rw_a_up, rw_g_up, rw_k_k, rw_k_a, rw_r_k, rw_gn_w,
                  rw_gn_b, mla_q_norm, mla_w_uq, mla_kv_norm, mla_w_ukv, w_out):
    D = w_in.shape[1]
    wi = w_in[l]
    o = np.cumsum([0, 3 * RW_WIDTH, G_RANK, LORA_RANK, LORA_RANK, LORA_RANK, LORA_RANK, FT_WIDTH, Q_RANK, KV_RANK, QK_ROPE])
    piece = lambda i: wi[:, o[i]:o[i + 1]]
    zpad = lambda n: jnp.zeros((D, n), F32)
    lora = [jnp.concatenate([piece(i), zpad(LANES - LORA_RANK)], axis=1) for i in (2, 3, 4, 5)]
    w_rw = jnp.concatenate([piece(0), piece(1)] + lora, axis=1)
    kr = piece(9)
    w_ckv = jnp.concatenate([piece(8), zpad(QK_NOPE), kr, _rope_partner(kr)], axis=1)
    pad_rows = lambda a: jnp.concatenate([a, jnp.zeros((a.shape[0], LANES - LORA_RANK, a.shape[2]), F32)], axis=1)
    head = jnp.arange(RW_WIDTH) // RW_HEAD
    uq = mla_w_uq[l].reshape(Q_RANK, MLA_HEADS, QK_NOPE + QK_ROPE)
    uq_rope = uq[:, :, QK_NOPE:]
    uq_partner = _rope_partner(uq_rope.reshape(Q_RANK * MLA_HEADS, QK_ROPE)).reshape(Q_RANK, MLA_HEADS, QK_ROPE)
    w_q = jnp.concatenate([uq, uq_partner], axis=2).reshape(Q_RANK, MLA_HEADS * HEAD_SLAB)
    ukv = mla_w_ukv[l].reshape(KV_RANK, MLA_HEADS, QK_NOPE + V_HEAD)
    w_k = jnp.concatenate([ukv[:, :, :QK_NOPE], jnp.zeros((KV_RANK, MLA_HEADS, HEAD_SLAB - QK_NOPE), F32)],
                          axis=2).reshape(KV_RANK, MLA_HEADS * HEAD_SLAB)
    w_v = jnp.concatenate([ukv[:, :, QK_NOPE:], jnp.zeros((KV_RANK, MLA_HEADS, ONES_ROWS), F32)],
                          axis=2).reshape(KV_RANK, MLA_HEADS * V_ROWS)
    return {
        "w_in": [w.astype(BF16) for w in (w_rw, piece(6), piece(7), w_ckv)],
        "rw": {"conv": rw_conv[l], "k_k": rw_k_k[l][None], "k_a": rw_k_a[l][None], "r_k": rw_r_k[l].reshape(1, RW_WIDTH),
               "w0": rw_w0[l], "a0": rw_a0[l], "w_up": pad_rows(rw_w_up[l]).astype(BF16), "a_up": pad_rows(rw_a_up[l]).astype(BF16),
               "g_up": rw_g_up[l].astype(BF16), "gn_w": rw_gn_w[l][None], "gn_b": rw_gn_b[l][None],
               "E": (head[:, None] == head[None, :]).astype(BF16)},
        "q_norm": mla_q_norm[l][None], "w_q": w_q.astype(BF16),
        "kv_norm": mla_kv_norm[l][None], "w_k": w_k.astype(BF16), "w_vt": w_v.T.astype(BF16),
        "w_out": w_out[l].astype(BF16),
    }


def _rope_tables(T, use_rope):
    ones = jnp.ones((T, QK_NOPE), F32)
    zeros = jnp.zeros((T, QK_NOPE), F32)
    zpad = jnp.zeros((T, HEAD_SLAB - QK_NOPE - QK_ROPE), F32)
    if use_rope:
        row = jnp.repeat(jnp.arange(T // GRID_W), GRID_W).astype(F32)
        col = (jnp.arange(T) % GRID_W).astype(F32)
        inv = ROPE_THETA ** (-jnp.arange(0, ROPE_AXIS, 2, dtype=F32) / ROPE_AXIS)
        ang = jnp.stack([row[:, None] * inv, col[:, None] * inv], axis=1)
        ang = jnp.broadcast_to(ang[:, :, None, :], (T, 2, 2, ROPE_AXIS // 2)).reshape(T, QK_ROPE)
        cos, sin = jnp.cos(ang), jnp.sin(ang)
    else:
        cos, sin = jnp.ones((T, QK_ROPE), F32), jnp.zeros((T, QK_ROPE), F32)
    return jnp.concatenate([ones, cos, zpad], axis=1), jnp.concatenate([zeros, sin, zpad], axis=1)


def _mixer(h_pieces_lat, h_pieces_ctx, p, tabs_lat, tabs_ctx, emit_ctx):
    rw_l, ft_l, cq_l, ckv_l = h_pieces_lat
    rw_c, ft_c, cq_c, ckv_c = h_pieces_ctx
    rwo_l, rwo_c = rwkv_branch(rw_l, rw_c, p["rw"], emit_ctx)
    fto_l = fourier_mixer(ft_l)
    q_l = q_projection(cq_l, p["q_norm"], p["w_q"], *tabs_lat)
    k_l, vt_l = kv_projection(ckv_l, p["kv_norm"], p["w_k"], p["w_vt"], *tabs_lat)
    k_c, vt_c = kv_projection(ckv_c, p["kv_norm"], p["w_k"], p["w_vt"], *tabs_ctx)
    att_l = attention(q_l, jnp.concatenate([k_l, k_c], axis=1), jnp.concatenate([vt_l, vt_c], axis=2))
    out_c = None
    if emit_ctx:
        fto_c = fourier_mixer(ft_c)
        q_c = q_projection(cq_c, p["q_norm"], p["w_q"], *tabs_ctx)
        att_c = attention(q_c, k_c, vt_c)
        out_c = (rwo_c, fto_c, att_c)
    return (rwo_l, fto_l, att_l), out_c


def kernel(x, c, ctx, c_ctx, ada_w, ada_b, w_in, rw_conv, rw_w0, rw_w_up, rw_a0, rw_a_up, rw_g_up, rw_k_k, rw_k_a,
           rw_r_k, rw_gn_w, rw_gn_b, mla_q_norm, mla_w_uq, mla_kv_norm, mla_w_ukv, w_out, ln1_w, ln1_b, ln2_w, ln2_b,
           ffn_w1, ffn_w3, ffn_w2, moe_router, moe_w1, moe_w3, moe_w2):
    B, T, D = x.shape
    Tc = ctx.shape[1]
    depth = w_in.shape[0]
    alpha = (2 * depth) ** 0.25
    assert B + 1 <= SUBLANES
    cc = jnp.zeros((SUBLANES, D), F32).at[:B].set(c).at[B].set(c_ctx)
    ada = ada_vectors(cc, ada_w, ada_b)
    tabs_lat = _rope_tables(T, True)
    tabs_ctx = _rope_tables(Tc, False)
    for l in range(depth):
        last = l == depth - 1
        p = _layer_params(l, w_in, rw_conv, rw_w0, rw_w_up, rw_a0, rw_a_up, rw_g_up, rw_k_k, rw_k_a, rw_r_k,
                          rw_gn_w, rw_gn_b, mla_q_norm, mla_w_uq, mla_kv_norm, mla_w_ukv, w_out)
        mods = ada[l].reshape(SUBLANES, 6, D)
        lat = [mods[:B, j][:, None, :] for j in range(6)]
        cx = [jnp.broadcast_to(mods[B, j][None, None, :], (B, 1, D)) for j in range(6)]
        sh_m, sc_m, g_m, sh_f, sc_f, g_f = lat
        csh_m, csc_m, cg_m, csh_f, csc_f, cg_f = cx
        ln1 = (ln1_w[l][None], ln1_b[l][None])
        ln2 = (ln2_w[l][None], ln2_b[l][None])

        pieces_l = in_projection(x, sh_m, sc_m, p["w_in"])
        pieces_c = in_projection(ctx, csh_m, csc_m, p["w_in"])
        mix_l, mix_c = _mixer(pieces_l, pieces_c, p, tabs_lat, tabs_ctx, not last)
        x = out_projection_ln(*mix_l, x, g_m, *ln1, p["w_out"], alpha)
        i = l // 2
        if l % 2 == 0:
            dense = (ffn_w1[i].astype(BF16), ffn_w3[i].astype(BF16), ffn_w2[i].astype(BF16))
            x = ffn_ln(x, sh_f, sc_f, g_f, *ln2, *dense, alpha)
        else:
            x = moe_ln(x, sh_f, sc_f, g_f, *ln2, moe_router[i], moe_w1[i], moe_w3[i], moe_w2[i], alpha)
        if not last:
            ctx = out_projection_ln(*mix_c, ctx, cg_m, *ln1, p["w_out"], alpha)
            if l % 2 == 0:
                ctx = ffn_ln(ctx, csh_f, csc_f, cg_f, *ln2, *dense, alpha)
            else:
                ctx = moe_ln(ctx, csh_f, csc_f, cg_f, *ln2, moe_router[i], moe_w1[i], moe_w3[i], moe_w2[i], alpha)
    return x
```

```python
import functools
import math

import numpy as np
import jax
import jax.numpy as jnp
from jax import lax
from jax.experimental import pallas as pl
from jax.experimental.pallas import tpu as pltpu

F32 = jnp.float32
BF16 = jnp.bfloat16
HI = lax.Precision.HIGHEST

LANES = 128
SUBLANES = 8
VMEM_LIMIT = 56 * 1024 * 1024

GRID_W = 64
RW_HEADS = 8
RW_HEAD = 64
RW_WIDTH = RW_HEADS * RW_HEAD
G_RANK = 128
LORA_RANK = 64
GN_EPS = 64e-5
FT_GROUP = 64
FT_WIDTH = 512
MLA_HEADS = 16
QK_NOPE = 64
QK_ROPE = 32
V_HEAD = 64
MLA_WIDTH = MLA_HEADS * V_HEAD
Q_RANK = 512
KV_RANK = 256
ROPE_AXIS = QK_ROPE // 2
ROPE_THETA = 10000.0
ATTN_SCALE = (QK_NOPE + QK_ROPE) ** -0.5
Q_SCALE = ATTN_SCALE * math.log2(math.e)
N_EXPERTS = 8
LN_EPS = 1e-5
MOD_EPS = 1e-6
RMS_EPS = 1e-6
CHUNK = 64
INV_BLOCK = 16
MOE_ROWS = 1024

NT = (((1,), (1,)), ((), ()))
TN = (((0,), (0,)), ((), ()))


def _cparams(*sem):
    return pltpu.CompilerParams(dimension_semantics=sem, vmem_limit_bytes=VMEM_LIMIT)


def _dot(a, b, prec=None):
    return jnp.dot(a, b, precision=prec, preferred_element_type=F32)


def _dot_split(x, w):
    hi = x.astype(BF16)
    r1 = x - hi.astype(F32)
    mid = r1.astype(BF16)
    lo = (r1 - mid.astype(F32)).astype(BF16)
    return _dot(hi, w) + _dot(mid, w) + _dot(lo, w)


def _dot3(a, b):
    a_hi = a.astype(BF16)
    b_hi = b.astype(BF16)
    a_lo = (a - a_hi.astype(F32)).astype(BF16)
    b_lo = (b - b_hi.astype(F32)).astype(BF16)
    return _dot(a_hi, b_hi) + _dot(a_hi, b_lo) + _dot(a_lo, b_hi)


def _standardize(x, eps):
    mu = jnp.mean(x, axis=-1, keepdims=True)
    xc = x - mu
    var = jnp.mean(xc * xc, axis=-1, keepdims=True)
    return xc * lax.rsqrt(var + eps)


def _tile(n, pref):
    t = min(n, pref)
    assert n % t == 0, (n, pref)
    return t


def _ada_kernel(c_ref, w_ref, b_ref, o_ref):
    c = c_ref[...]
    s = c * jax.nn.sigmoid(c)
    o_ref[0] = _dot(s, w_ref[0], HI) + b_ref[0]


def ada_vectors(cc, ada_w, ada_b):
    L, D, N6 = ada_w.shape
    tn = _tile(N6, 1024)
    return pl.pallas_call(
        _ada_kernel,
        grid=(L, N6 // tn),
        in_specs=[pl.BlockSpec((SUBLANES, D), lambda l, j: (0, 0)),
                  pl.BlockSpec((1, D, tn), lambda l, j: (l, 0, j)),
                  pl.BlockSpec((1, 1, tn), lambda l, j: (l, 0, j))],
        out_specs=pl.BlockSpec((1, SUBLANES, tn), lambda l, j: (l, 0, j)),
        out_shape=jax.ShapeDtypeStruct((L, SUBLANES, N6), F32),
        compiler_params=_cparams("parallel", "parallel"),
        name="ada_vectors",
    )(cc, ada_w, ada_b.reshape(L, 1, N6))


def _inproj_kernel(x_ref, sh_ref, sc_ref, *refs):
    nw = len(refs) // 2
    h = _standardize(x_ref[0], MOD_EPS) * (1.0 + sc_ref[0]) + sh_ref[0]
    hb = h.astype(BF16)
    for w_ref, o_ref in zip(refs[:nw], refs[nw:]):
        o_ref[0] = _dot(hb, w_ref[...])


def in_projection(x, shift, scale, weights):
    B, T, D = x.shape
    tm = _tile(T, 256)
    vec = pl.BlockSpec((1, 1, D), lambda b, i: (b, 0, 0))
    return pl.pallas_call(
        _inproj_kernel,
        grid=(B, T // tm),
        in_specs=[pl.BlockSpec((1, tm, D), lambda b, i: (b, i, 0)), vec, vec]
        + [pl.BlockSpec(w.shape, lambda b, i: (0, 0)) for w in weights],
        out_specs=[pl.BlockSpec((1, tm, w.shape[1]), lambda b, i: (b, i, 0)) for w in weights],
        out_shape=[jax.ShapeDtypeStruct((B, T, w.shape[1]), F32) for w in weights],
        compiler_params=_cparams("parallel", "parallel"),
        name="in_projection",
    )(x, shift, scale, *weights)


RW_COLS = 3 * RW_WIDTH + G_RANK + 4 * LANES


def _softplus(z):
    return jnp.maximum(z, 0.0) + jnp.log(1.0 + jnp.exp(-jnp.abs(z)))


def _rwkv_prep_kernel(x_ref, xp_ref, xn_ref, conv_ref, kk_ref, ka_ref, rk_ref, w0_ref, a0_ref,
                      wup_ref, aup_ref, gup_ref, e_ref,
                      r_o, v_o, kk_o, ld0_o, b0_o, kr0_o, ld1_o, b1_o, kr1_o, bonus_o, gate_o):
    i = pl.program_id(1)
    n = pl.num_programs(1)
    x = x_ref[0]
    W3 = 3 * RW_WIDTH
    raw = x[:, :W3]
    tm = raw.shape[0]
    row = lax.broadcasted_iota(jnp.int32, (tm, 1), 0)
    prev_row = jnp.where(i > 0, xp_ref[0, SUBLANES - 1:SUBLANES, :], 0.0)
    next_row = jnp.where(i < n - 1, xn_ref[0, 0:1, :], 0.0)
    xm = jnp.where(row == 0, prev_row, pltpu.roll(raw, 1, 0))
    xq = jnp.where(row == tm - 1, next_row, pltpu.roll(raw, tm - 1, 0))
    cw = conv_ref[...]
    y = xm * cw[0:1] + raw * cw[1:2] + xq * cw[2:3]
    r = y[:, :RW_WIDTH]
    k = y[:, RW_WIDTH:2 * RW_WIDTH]
    v = y[:, 2 * RW_WIDTH:W3]
    E = e_ref[...]
    kkv = k * kk_ref[...]
    kk = kkv / jnp.maximum(jnp.sqrt(_dot_split(kkv * kkv, E)), 1e-12)
    r_o[0] = r
    v_o[0] = v
    kk_o[0] = kk
    g_dn = x[:, W3:W3 + G_RANK]
    gate_o[0] = _dot(jax.nn.sigmoid(g_dn).astype(BF16), gup_ref[...])
    kr_sum = jnp.zeros_like(r)
    outs = ((ld0_o, b0_o, kr0_o), (ld1_o, b1_o, kr1_o))
    for d in range(2):
        base = W3 + G_RANK
        w_dn = x[:, base + d * LANES: base + (d + 1) * LANES]
        a_dn = x[:, base + (2 + d) * LANES: base + (3 + d) * LANES]
        z = w0_ref[d:d + 1, :] + _dot(jnp.tanh(w_dn).astype(BF16), wup_ref[d])
        logw = -_softplus(-z) - 0.5
        a = jax.nn.sigmoid(a0_ref[d:d + 1, :] + _dot(a_dn.astype(BF16), aup_ref[d]))
        kr = k * (1.0 + (a - 1.0) * ka_ref[...])
        ld_o, b_o, kr_o = outs[d]
        ld_o[0] = -jnp.exp(logw)
        b_o[0] = a * kk
        kr_o[0] = kr
        kr_sum = kr_sum + kr
    bonus_o[0] = _dot_split(r * kr_sum * rk_ref[...], E) * v


def rwkv_prep(rw, p):
    B, T, _ = rw.shape
    tm = _tile(T, 256)
    nh = tm // SUBLANES
    last = T // SUBLANES - 1
    W3 = 3 * RW_WIDTH
    full = lambda a: pl.BlockSpec(a.shape, lambda b, i: (0,) * a.ndim)
    params = [p["conv"], p["k_k"], p["k_a"], p["r_k"], p["w0"], p["a0"], p["w_up"], p["a_up"], p["g_up"], p["E"]]
    outs = pl.pallas_call(
        _rwkv_prep_kernel,
        grid=(B, T // tm),
        in_specs=[pl.BlockSpec((1, tm, RW_COLS), lambda b, i: (b, i, 0)),
                  pl.BlockSpec((1, SUBLANES, W3), lambda b, i: (b, jnp.maximum(i * nh - 1, 0), 0)),
                  pl.BlockSpec((1, SUBLANES, W3), lambda b, i: (b, jnp.minimum((i + 1) * nh, last), 0))]
        + [full(a) for a in params],
        out_specs=[pl.BlockSpec((1, tm, RW_WIDTH), lambda b, i: (b, i, 0))] * 11,
        out_shape=[jax.ShapeDtypeStruct((B, T, RW_WIDTH), F32)] * 11,
        compiler_params=_cparams("parallel", "parallel"),
        name="rwkv_prep",
    )(rw, rw, rw, *params)
    return outs


PAIR = 2 * RW_HEAD
N_PAIRS = RW_HEADS // 2
CHUNK_GROUP = 4


def _pair_masks(reverse):
    i = lax.broadcasted_iota(jnp.int32, (PAIR, PAIR), 0)
    j = lax.broadcasted_iota(jnp.int32, (PAIR, PAIR), 1)
    same = (i // CHUNK) == (j // CHUNK)
    strict = same & ((j > i) if reverse else (j < i))
    incl = same & ((j >= i) if reverse else (j <= i))
    blk = (i // INV_BLOCK) == (j // INV_BLOCK)
    eye = jnp.where(i == j, 1.0, 0.0).astype(F32)
    t = lax.broadcasted_iota(jnp.int32, (CHUNK, CHUNK), 0)
    u = lax.broadcasted_iota(jnp.int32, (CHUNK, CHUNK), 1)
    tri = jnp.where((u >= t) if reverse else (u <= t), 1.0, 0.0).astype(F32)
    first = lax.broadcasted_iota(jnp.int32, (1, PAIR), 1) < RW_HEAD
    return strict, incl, blk, eye, tri, first


def _pair_chunk_math(ld, r, v, kk, b, kr, masks):
    strict, incl, blk, eye, tri, first = masks
    P = ld.shape[0]
    ein = lambda spec, a, c, prec=None: jnp.einsum(spec, a, c, precision=prec, preferred_element_type=F32)
    tri_b = jnp.broadcast_to(tri.astype(BF16), (P, CHUNK, CHUNK))
    ld_hi = ld.astype(BF16)
    ld_r = ld - ld_hi.astype(F32)
    ld_mid = ld_r.astype(BF16)
    ld_lo = (ld_r - ld_mid.astype(F32)).astype(BF16)
    Lc = ein("pct,ptk->pck", tri_b, ld_hi) + ein("pct,ptk->pck", tri_b, ld_mid) + ein("pct,ptk->pck", tri_b, ld_lo)
    Lx = Lc - ld
    Lt = jnp.sum(ld, axis=1, keepdims=True)
    ginv = jnp.exp(-Lc)
    gout = jnp.exp(Lt - Lc)
    stack = lambda x: jnp.concatenate([jnp.where(first, x, 0.0), jnp.where(first, 0.0, x)], axis=1).astype(BF16)
    twice = lambda x: jnp.concatenate([x, x], axis=1).astype(BF16)
    Xk = stack(kk * jnp.exp(Lx))
    Xr = stack(r * jnp.exp(Lc))
    Vs = stack(v)
    Bs = stack(b * gout)
    Ks = stack(kr * gout)
    mm = lambda a, c: ein("pij,pjk->pik", a.astype(BF16), c.astype(BF16))
    nt = lambda a, c: ein("pik,pjk->pij", a, c)
    tn = lambda a, c: ein("pji,pjk->pik", a, c.astype(BF16))
    XX = jnp.concatenate([Xk, Xr], axis=1)
    Mb = nt(XX, twice(b * ginv))
    Mk = nt(XX, twice(kr * ginv))
    Mab = jnp.where(strict, Mb[:, :PAIR], 0.0)
    Arb = jnp.where(incl, Mb[:, PAIR:], 0.0)
    Mak = jnp.where(strict, Mk[:, :PAIR], 0.0)
    Ark = jnp.where(incl, Mk[:, PAIR:], 0.0)
    Nd = jnp.where(blk, Mab, 0.0)
    No = Mab - Nd
    N2 = mm(Nd, Nd)
    N4 = mm(N2, N2)
    N8 = mm(N4, N4)
    Td = mm(mm(mm(eye - Nd, eye + N2), eye + N4), eye + N8)
    M2 = mm(Td, No)
    Tm = mm(mm(eye - M2, eye + mm(M2, M2)), Td)
    P1 = mm(Tm, Xk)
    P2 = mm(Tm, mm(Mak, Vs))
    Q1s = Xr.astype(F32) - mm(Arb, P1)
    Yis = mm(Ark, Vs) - mm(Arb, P2)
    G = eye * jnp.exp(Lt) - tn(Bs, P1)
    H = tn(Ks, Vs) - tn(Bs, P2)
    return Q1s[:, :CHUNK] + Q1s[:, CHUNK:], Yis[:, :CHUNK] + Yis[:, CHUNK:], G, H


def _rwkv_chunk_kernel(ld_ref, r_ref, v_ref, kk_ref, b_ref, kr_ref, q_o, yi_o, g_o, h_o, *, nch, reverse):
    masks = _pair_masks(reverse)
    group = min(CHUNK_GROUP, nch)

    def body(i, carry):
        c0 = i * group
        rows = [pl.ds(pl.multiple_of((c0 + u) * CHUNK, CHUNK), CHUNK) for u in range(group)]
        get = lambda ref: jnp.stack([ref[0, rows[u], j * PAIR:(j + 1) * PAIR]
                                     for u in range(group) for j in range(N_PAIRS)])
        Q1, Yi, G, H = _pair_chunk_math(get(ld_ref), get(r_ref), get(v_ref), get(kk_ref), get(b_ref), get(kr_ref),
                                        masks)
        for u in range(group):
            for j in range(N_PAIRS):
                q_o[0, rows[u], j * PAIR:(j + 1) * PAIR] = Q1[u * N_PAIRS + j]
                yi_o[0, rows[u], j * PAIR:(j + 1) * PAIR] = Yi[u * N_PAIRS + j]
        mats = pl.ds(c0 * N_PAIRS, group * N_PAIRS)
        g_o[0, mats] = G
        h_o[0, mats] = H
        return carry

    lax.fori_loop(0, nch // group, body, 0)


def rwkv_chunks(ld, r, v, kk, b, kr, reverse):
    B, T, W = ld.shape
    tt = _tile(T, 4 * CHUNK)
    nch = tt // CHUNK
    tok = pl.BlockSpec((1, tt, W), lambda bb, i: (bb, i, 0))
    mat = pl.BlockSpec((1, nch * N_PAIRS, PAIR, PAIR), lambda bb, i: (bb, i, 0, 0))
    mats = jax.ShapeDtypeStruct((B, T // CHUNK * N_PAIRS, PAIR, PAIR), F32)
    return pl.pallas_call(
        functools.partial(_rwkv_chunk_kernel, nch=nch, reverse=reverse),
        grid=(B, T // tt),
        in_specs=[tok] * 6,
        out_specs=[tok, tok, mat, mat],
        out_shape=[jax.ShapeDtypeStruct((B, T, W), F32)] * 2 + [mats, mats],
        compiler_params=_cparams("parallel", "parallel"),
        name="rwkv_chunks_bwd" if reverse else "rwkv_chunks_fwd",
    )(ld, r, v, kk, b, kr)


def _rwkv_seq_kernel(q_ref, yi_ref, g_ref, h_ref, s0_ref, y_o, sf_o, s_scr, *, nch, reverse):
    @pl.when(pl.program_id(1) == 0)
    def _():
        s_scr[...] = s0_ref[0]

    def body(cc, carry):
        c = (nch - 1 - cc) if reverse else cc
        rows = pl.ds(pl.multiple_of(c * CHUNK, CHUNK), CHUNK)
        for j in range(N_PAIRS):
            lanes = slice(j * PAIR, (j + 1) * PAIR)
            S = s_scr[j]
            y_o[0, rows, lanes] = _dot3(q_ref[0, rows, lanes], S) + yi_ref[0, rows, lanes]
            s_scr[j] = _dot3(g_ref[0, c * N_PAIRS + j], S) + h_ref[0, c * N_PAIRS + j]
        return carry

    lax.fori_loop(0, nch, body, 0)
    sf_o[0] = s_scr[...]


def rwkv_sequential(q1, yi, g, hm, s0, reverse):
    B, T, W = q1.shape
    tt = _tile(T, 8 * CHUNK)
    n = T // tt
    nch = tt // CHUNK
    step = (lambda i: n - 1 - i) if reverse else (lambda i: i)
    tok = pl.BlockSpec((1, tt, W), lambda bb, i: (bb, step(i), 0))
    mat = pl.BlockSpec((1, nch * N_PAIRS, PAIR, PAIR), lambda bb, i: (bb, step(i), 0, 0))
    sspec = pl.BlockSpec((1, N_PAIRS, PAIR, PAIR), lambda bb, i: (bb, 0, 0, 0))
    return pl.pallas_call(
        functools.partial(_rwkv_seq_kernel, nch=nch, reverse=reverse),
        grid=(B, n),
        in_specs=[tok, tok, mat, mat, sspec],
        out_specs=[tok, sspec],
        out_shape=[jax.ShapeDtypeStruct((B, T, W), F32), jax.ShapeDtypeStruct((B, N_PAIRS, PAIR, PAIR), F32)],
        scratch_shapes=[pltpu.VMEM((N_PAIRS, PAIR, PAIR), F32)],
        compiler_params=_cparams("parallel", "arbitrary"),
        name="rwkv_seq_bwd" if reverse else "rwkv_seq_fwd",
    )(q1, yi, g, hm, s0)


def _rwkv_readout_kernel(yf_ref, yb_ref, bonus_ref, gate_ref, gw_ref, gb_ref, e_ref, o_ref):
    E = e_ref[...]
    ys = yf_ref[0] + yb_ref[0]
    yc = ys - _dot_split(ys, E) * (1.0 / RW_HEAD)
    var = _dot_split(yc * yc, E) * (1.0 / RW_HEAD)
    yn = yc * lax.rsqrt(var + GN_EPS) * gw_ref[...] + gb_ref[...]
    o_ref[0] = (yn + bonus_ref[0]) * gate_ref[0]


def rwkv_readout(yf, yb, bonus, gate, p):
    B, T, W = yf.shape
    tm = _tile(T, 512)
    spec = pl.BlockSpec((1, tm, W), lambda b, i: (b, i, 0))
    full = lambda a: pl.BlockSpec(a.shape, lambda b, i: (0,) * a.ndim)
    return pl.pallas_call(
        _rwkv_readout_kernel,
        grid=(B, T // tm),
        in_specs=[spec] * 4 + [full(p["gn_w"]), full(p["gn_b"]), full(p["E"])],
        out_specs=spec,
        out_shape=jax.ShapeDtypeStruct((B, T, W), F32),
        compiler_params=_cparams("parallel", "parallel"),
        name="rwkv_readout",
    )(yf, yb, bonus, gate, p["gn_w"], p["gn_b"], p["E"])


def rwkv_branch(rw_lat, rw_ctx, p, emit_ctx):
    prep_l = rwkv_prep(rw_lat, p)
    prep_c = rwkv_prep(rw_ctx, p)
    B = rw_lat.shape[0]
    s_zero = jnp.zeros((B, N_PAIRS, PAIR, PAIR), F32)

    def scans(prep, s0s):
        r, v, kk = prep[:3]
        ys, finals = [], []
        for d in range(2):
            ld, b, kr = prep[3 + 3 * d: 6 + 3 * d]
            q1, yi, g, hm = rwkv_chunks(ld, r, v, kk, b, kr, reverse=bool(d))
            y, sf = rwkv_sequential(q1, yi, g, hm, s0s[d], reverse=bool(d))
            ys.append(y)
            finals.append(sf)
        return ys, finals

    ys_c, fin_c = scans(prep_c, (s_zero, s_zero))
    ys_l, _ = scans(prep_l, fin_c)
    out_l = rwkv_readout(ys_l[0], ys_l[1], prep_l[9], prep_l[10], p)
    out_c = rwkv_readout(ys_c[0], ys_c[1], prep_c[9], prep_c[10], p) if emit_ctx else None
    return out_l, out_c


def _dft_mats(n):
    a = 2.0 * np.pi * np.outer(np.arange(n), np.arange(n)) / n
    return np.cos(a), np.sin(a)


def _fft1_kernel(u_ref, c_ref, s_ref, twc_ref, tws_ref, ar_o, ai_o, *, tn2, ch):
    U = u_ref[0]
    Ar = _dot3(c_ref[...], U)
    Ai = -_dot3(s_ref[...], U)
    twc = twc_ref[0]
    tws = tws_ref[0]
    for j in range(tn2):
        ct = twc[:, j:j + 1]
        st = tws[:, j:j + 1]
        a_r = Ar[:, j * ch:(j + 1) * ch]
        a_i = Ai[:, j * ch:(j + 1) * ch]
        ar_o[0, j] = a_r * ct + a_i * st
        ai_o[0, j] = a_i * ct - a_r * st


def _fft2_kernel(ar_ref, ai_ref, c_ref, s_ref, cc_ref, sc_ref, o_ref):
    Ar = ar_ref[0]
    Ai = ai_ref[0]
    C = c_ref[...]
    S = s_ref[...]
    Yr = _dot3(C, Ar) + _dot3(S, Ai)
    Yi = _dot3(C, Ai) - _dot3(S, Ar)
    Cc = cc_ref[...]
    Sc = sc_ref[...]
    for m in range(Ar.shape[1] // LANES):
        sl = slice(m * LANES, (m + 1) * LANES)
        o_ref[0, :, sl] = _dot3(Yr[:, sl], Cc) + _dot3(Yi[:, sl], Sc)


def fourier_mixer(u):
    B, T, ch = u.shape
    lg = int(round(math.log2(T)))
    assert 1 << lg == T
    N1 = 1 << ((lg + 1) // 2)
    N2 = T // N1
    c1, s1 = _dft_mats(N1)
    c2, s2 = _dft_mats(N2)
    tw = 2.0 * np.pi * np.outer(np.arange(N1), np.arange(N2)) / T
    tn2 = min(SUBLANES, N2)
    nj = N2 // tn2
    twc = np.cos(tw).reshape(N1, nj, tn2).transpose(1, 0, 2)
    tws = np.sin(tw).reshape(N1, nj, tn2).transpose(1, 0, 2)
    cg, sg = _dft_mats(FT_GROUP)
    scale = 1.0 / math.sqrt(T * FT_GROUP)
    eye2 = np.eye(LANES // FT_GROUP)
    cc = np.kron(eye2, cg) * scale
    sc = np.kron(eye2, sg) * scale
    f = lambda a: jnp.asarray(a, F32)
    full2 = lambda n, m: pl.BlockSpec((n, m), lambda b, j: (0, 0))

    ar, ai = pl.pallas_call(
        functools.partial(_fft1_kernel, tn2=tn2, ch=ch),
        grid=(B, nj),
        in_specs=[pl.BlockSpec((1, N1, tn2 * ch), lambda b, j: (b, 0, j)),
                  full2(N1, N1), full2(N1, N1),
                  pl.BlockSpec((1, N1, tn2), lambda b, j: (j, 0, 0)),
                  pl.BlockSpec((1, N1, tn2), lambda b, j: (j, 0, 0))],
        out_specs=[pl.BlockSpec((1, tn2, N1, ch), lambda b, j: (b, j, 0, 0))] * 2,
        out_shape=[jax.ShapeDtypeStruct((B, N2, N1, ch), F32)] * 2,
        compiler_params=_cparams("parallel", "parallel"),
        name="fft_stage1",
    )(u.reshape(B, N1, N2 * ch), f(c1), f(s1), f(twc), f(tws))

    tk1 = min(SUBLANES, N1)
    blk = pl.BlockSpec((1, N2, tk1 * ch), lambda b, j: (b, 0, j))
    out = pl.pallas_call(
        _fft2_kernel,
        grid=(B, N1 // tk1),
        in_specs=[blk, blk, full2(N2, N2), full2(N2, N2), full2(LANES, LANES), full2(LANES, LANES)],
        out_specs=blk,
        out_shape=jax.ShapeDtypeStruct((B, N2, N1 * ch), F32),
        compiler_params=_cparams("parallel", "parallel"),
        name="fft_stage2",
    )(ar.reshape(B, N2, N1 * ch), ai.reshape(B, N2, N1 * ch), f(c2), f(s2), f(cc), f(sc))
    return out.reshape(B, T, ch)


HEAD_SLAB = LANES
ROPE_SHIFT = HEAD_SLAB - QK_ROPE
ONES_ROWS = 16
V_ROWS = V_HEAD + ONES_ROWS
ATTN_UNROLL = 6
ATTN_TILES = 4


def _rms(x, w):
    return x * lax.rsqrt(jnp.mean(x * x, axis=-1, keepdims=True) + RMS_EPS) * w


def _qproj_kernel(cq_ref, nw_ref, wt_ref, ct_ref, st_ref, qt_o):
    n = _rms(cq_ref[0], nw_ref[...]).astype(BF16)
    qt = lax.dot_general(wt_ref[...], n, NT, preferred_element_type=F32)
    ct = ct_ref[...]
    st = st_ref[...]
    for h in range(MLA_HEADS):
        sl = slice(h * HEAD_SLAB, (h + 1) * HEAD_SLAB)
        s = qt[sl, :]
        qt_o[0, sl, :] = ((s * ct + pltpu.roll(s, ROPE_SHIFT, 0) * st) * Q_SCALE).astype(BF16)


def _kvproj_kernel(ckv_ref, nw_ref, wk_ref, wvt_ref, ct_ref, st_ref, k_o, vt_o):
    x = ckv_ref[0]
    n = _rms(x[:, :KV_RANK], nw_ref[...]).astype(BF16)
    rs = x[:, KV_RANK:KV_RANK + HEAD_SLAB]
    rope = rs * ct_ref[...] + pltpu.roll(rs, ROPE_SHIFT, 1) * st_ref[...]
    kn = _dot(n, wk_ref[...])
    for h in range(MLA_HEADS):
        sl = slice(h * HEAD_SLAB, (h + 1) * HEAD_SLAB)
        k_o[0, :, sl] = (kn[:, sl] + rope).astype(BF16)
    vt = lax.dot_general(wvt_ref[...], n, NT, preferred_element_type=F32)
    row = lax.broadcasted_iota(jnp.int32, vt.shape, 0)
    vt_o[0] = jnp.where(row % V_ROWS >= V_HEAD, 1.0, vt).astype(BF16)


def q_projection(cq, nw, wt, ct_t, st_t):
    B, T, R = cq.shape
    tm = _tile(T, 256)
    W = MLA_HEADS * HEAD_SLAB
    tab = pl.BlockSpec((HEAD_SLAB, tm), lambda b, i: (0, i))
    return pl.pallas_call(
        _qproj_kernel,
        grid=(B, T // tm),
        in_specs=[pl.BlockSpec((1, tm, R), lambda b, i: (b, i, 0)),
                  pl.BlockSpec(nw.shape, lambda b, i: (0, 0)),
                  pl.BlockSpec(wt.shape, lambda b, i: (0, 0)), tab, tab],
        out_specs=pl.BlockSpec((1, W, tm), lambda b, i: (b, 0, i)),
        out_shape=jax.ShapeDtypeStruct((B, W, T), BF16),
        compiler_params=_cparams("parallel", "parallel"),
        name="q_projection",
    )(cq, nw, wt, ct_t, st_t)


def kv_projection(ckv, nw, wk, wvt, ct, st):
    B, T, R = ckv.shape
    tm = _tile(T, 256)
    W = MLA_HEADS * HEAD_SLAB
    tab = pl.BlockSpec((tm, HEAD_SLAB), lambda b, i: (i, 0))
    return pl.pallas_call(
        _kvproj_kernel,
        grid=(B, T // tm),
        in_specs=[pl.BlockSpec((1, tm, R), lambda b, i: (b, i, 0)),
                  pl.BlockSpec(nw.shape, lambda b, i: (0, 0)),
                  pl.BlockSpec(wk.shape, lambda b, i: (0, 0)),
                  pl.BlockSpec(wvt.shape, lambda b, i: (0, 0)), tab, tab],
        out_specs=[pl.BlockSpec((1, tm, W), lambda b, i: (b, i, 0)),
                   pl.BlockSpec((1, MLA_HEADS * V_ROWS, tm), lambda b, i: (b, 0, i))],
        out_shape=[jax.ShapeDtypeStruct((B, T, W), BF16), jax.ShapeDtypeStruct((B, MLA_HEADS * V_ROWS, T), BF16)],
        compiler_params=_cparams("parallel", "parallel"),
        name="kv_projection",
    )(ckv, nw, wk, wvt, ct, st)


def _attn_kernel(q_ref, k_ref, vt_ref, o_ref, m_scr, acc_scr, sa_scr, sb_scr, ma_scr, mb_scr, *, tq, tkc, nkc):
    ntile = q_ref.shape[2] // tq
    bufs = ((sa_scr, ma_scr), (sb_scr, mb_scr))

    def chunk(c):
        return pl.ds(pl.multiple_of(c * tkc, tkc), tkc)

    def scores(t, c, dst, dmax):
        for hh in range(2):
            qt = q_ref[0, hh * HEAD_SLAB:(hh + 1) * HEAD_SLAB, t * tq:(t + 1) * tq]
            kc = k_ref[0, chunk(c), hh * HEAD_SLAB:(hh + 1) * HEAD_SLAB]
            s = _dot(kc, qt)
            dst[hh] = s
            dmax[hh] = jnp.max(s, axis=0, keepdims=True)

    def consume(src, smax, c):
        for hh in range(2):
            m_old = m_scr[hh]
            m_new = jnp.maximum(m_old, smax[hh])
            pr = jnp.exp2(src[hh] - m_new).astype(BF16)
            alpha = jnp.exp2(m_old - m_new)
            m_scr[hh] = m_new
            rows = slice(hh * V_ROWS, (hh + 1) * V_ROWS)
            acc_scr[rows, :] = alpha * acc_scr[rows, :] + _dot(vt_ref[0, rows, chunk(c)], pr)

    unroll = ATTN_UNROLL if nkc > 2 * ATTN_UNROLL else 1
    ngroups = (nkc - 1) // (2 * unroll)
    done = 2 * unroll * ngroups
    first = 0
    scores(0, 0, *bufs[first])
    for t in range(ntile):
        m_scr[...] = jnp.full(m_scr.shape, -jnp.inf, F32)
        acc_scr[...] = jnp.zeros(acc_scr.shape, F32)
        cur, nxt = bufs[first], bufs[1 - first]

        def body(i, carry, t=t, cur=cur, nxt=nxt):
            c = 2 * unroll * i
            for _ in range(unroll):
                scores(t, c + 1, *nxt)
                consume(*cur, c)
                scores(t, c + 2, *cur)
                consume(*nxt, c + 1)
                c = c + 2
            return carry

        lax.fori_loop(0, ngroups, body, 0)
        for cc in range(done, nkc):
            cur, nxt = bufs[first], bufs[1 - first]
            if cc + 1 < nkc:
                scores(t, cc + 1, *nxt)
            elif t + 1 < ntile:
                scores(t + 1, 0, *nxt)
            consume(*cur, cc)
            first = 1 - first
        outs = []
        for hh in range(2):
            base = hh * V_ROWS
            outs.append(acc_scr[base:base + V_HEAD, :] / acc_scr[base + V_HEAD:base + V_HEAD + 1, :])
        o_ref[0, t * tq:(t + 1) * tq, :] = jnp.concatenate(outs, axis=0).T.astype(o_ref.dtype)


def attention(q, k, vt):
    B, _, T = q.shape
    Tk = k.shape[1]
    tq = _tile(T, 256)
    tb = _tile(T, ATTN_TILES * tq)
    tkc = next(c for c in (640, 512, 256, 128) if Tk % c == 0)
    hp = MLA_HEADS // 2
    return pl.pallas_call(
        functools.partial(_attn_kernel, tq=tq, tkc=tkc, nkc=Tk // tkc),
        grid=(B, hp, T // tb),
        in_specs=[pl.BlockSpec((1, 2 * HEAD_SLAB, tb), lambda b, h, i: (b, h, i)),
                  pl.BlockSpec((1, Tk, 2 * HEAD_SLAB), lambda b, h, i: (b, 0, h)),
                  pl.BlockSpec((1, 2 * V_ROWS, Tk), lambda b, h, i: (b, h, 0))],
        out_specs=pl.BlockSpec((1, tb, 2 * V_HEAD), lambda b, h, i: (b, i, h)),
        out_shape=jax.ShapeDtypeStruct((B, T, MLA_WIDTH), BF16),
        scratch_shapes=[pltpu.VMEM((2, 1, tq), F32), pltpu.VMEM((2 * V_ROWS, tq), F32),
                        pltpu.VMEM((2, tkc, tq), F32), pltpu.VMEM((2, tkc, tq), F32),
                        pltpu.VMEM((2, 1, tq), F32), pltpu.VMEM((2, 1, tq), F32)],
        compiler_params=_cparams("parallel", "parallel", "arbitrary"),
        name="mla_attention",
    )(q, k, vt)


def _outproj_kernel(rw_ref, ft_ref, att_ref, x_ref, g_ref, lw_ref, lb_ref, w_ref, o_ref, *, alpha):
    w = w_ref
    mix = (_dot(rw_ref[0].astype(BF16), w[0:RW_WIDTH, :])
           + _dot(ft_ref[0].astype(BF16), w[RW_WIDTH:RW_WIDTH + FT_WIDTH, :])
           + _dot(att_ref[0], w[RW_WIDTH + FT_WIDTH:, :]))
    z = alpha * x_ref[0] + g_ref[0] * mix
    o_ref[0] = _standardize(z, LN_EPS) * lw_ref[...] + lb_ref[...]


def out_projection_ln(rw, ft, att, x, gate, ln_w, ln_b, w_out, alpha):
    B, T, D = x.shape
    tm = _tile(T, 256)
    tok = lambda n: pl.BlockSpec((1, tm, n), lambda b, i: (b, i, 0))
    row = pl.BlockSpec((1, D), lambda b, i: (0, 0))
    return pl.pallas_call(
        functools.partial(_outproj_kernel, alpha=alpha),
        grid=(B, T // tm),
        in_specs=[tok(RW_WIDTH), tok(FT_WIDTH), tok(MLA_WIDTH), tok(D),
                  pl.BlockSpec((1, 1, D), lambda b, i: (b, 0, 0)), row, row,
                  pl.BlockSpec(w_out.shape, lambda b, i: (0, 0))],
        out_specs=tok(D),
        out_shape=jax.ShapeDtypeStruct((B, T, D), F32),
        compiler_params=_cparams("parallel", "parallel"),
        name="out_projection_ln",
    )(rw, ft, att, x, gate, ln_w, ln_b, w_out)


def _ffn_kernel(x_ref, sh_ref, sc_ref, g_ref, lw_ref, lb_ref, w1_ref, w3_ref, w2_ref, o_ref, h_scr, *, alpha):
    f = pl.program_id(2)

    @pl.when(f == 0)
    def _():
        h = _standardize(x_ref[0], MOD_EPS) * (1.0 + sc_ref[0]) + sh_ref[0]
        h_scr[...] = h.astype(BF16)
        o_ref[0] = jnp.zeros(o_ref.shape[1:], F32)

    hb = h_scr[...]
    a = _dot(hb, w1_ref[...])
    b = _dot(hb, w3_ref[...])
    o_ref[0] += _dot((a * jax.nn.sigmoid(a) * b).astype(BF16), w2_ref[...])

    @pl.when(f == pl.num_programs(2) - 1)
    def _():
        z = alpha * x_ref[0] + g_ref[0] * o_ref[0]
        o_ref[0] = _standardize(z, LN_EPS) * lw_ref[...] + lb_ref[...]


def ffn_ln(x, shift, scale, gate, ln_w, ln_b, w1, w3, w2, alpha):
    B, T, D = x.shape
    F = w1.shape[1]
    tm = _tile(T, 1024)
    tf = _tile(F, 256)
    tok = pl.BlockSpec((1, tm, D), lambda b, i, f: (b, i, 0))
    vec = pl.BlockSpec((1, 1, D), lambda b, i, f: (b, 0, 0))
    row = pl.BlockSpec((1, D), lambda b, i, f: (0, 0))
    return pl.pallas_call(
        functools.partial(_ffn_kernel, alpha=alpha),
        grid=(B, T // tm, F // tf),
        in_specs=[tok, vec, vec, vec, row, row,
                  pl.BlockSpec((D, tf), lambda b, i, f: (0, f)),
                  pl.BlockSpec((D, tf), lambda b, i, f: (0, f)),
                  pl.BlockSpec((tf, D), lambda b, i, f: (f, 0))],
        out_specs=tok,
        out_shape=jax.ShapeDtypeStruct((B, T, D), F32),
        scratch_shapes=[pltpu.VMEM((tm, D), BF16)],
        compiler_params=_cparams("parallel", "parallel", "arbitrary"),
        name="ffn_ln",
    )(x, shift, scale, gate, ln_w, ln_b, w1, w3, w2)


def _router_kernel(x_ref, sh_ref, sc_ref, wr_ref, h_o, ti_o, tg_o):
    h = _standardize(x_ref[0], MOD_EPS) * (1.0 + sc_ref[0]) + sh_ref[0]
    h_o[0] = h.astype(BF16)
    logits = _dot(h, wr_ref[...], HI)
    lane = lax.broadcasted_iota(jnp.int32, logits.shape, 1)
    neg = jnp.float32(-jnp.inf)
    logits = jnp.where(lane < N_EXPERTS, logits, neg)
    m1 = jnp.max(logits, axis=-1, keepdims=True)
    i1 = jnp.min(jnp.where(logits == m1, lane, LANES), axis=-1, keepdims=True)
    rest = jnp.where(lane == i1, neg, logits)
    m2 = jnp.max(rest, axis=-1, keepdims=True)
    i2 = jnp.min(jnp.where(rest == m2, lane, LANES), axis=-1, keepdims=True)
    e = jnp.exp(m2 - m1)
    g1 = 1.0 / (1.0 + e)
    g2 = e / (1.0 + e)
    ti_o[0] = jnp.where(lane == 0, i1, jnp.where(lane == 1, i2, 0))
    tg_o[0] = jnp.where(lane == 0, g1, jnp.where(lane == 1, g2, 0.0))


def moe_router(x, shift, scale, wr):
    B, T, D = x.shape
    tm = _tile(T, 512)
    tok = lambda n: pl.BlockSpec((1, tm, n), lambda b, i: (b, i, 0))
    vec = pl.BlockSpec((1, 1, D), lambda b, i: (b, 0, 0))
    return pl.pallas_call(
        _router_kernel,
        grid=(B, T // tm),
        in_specs=[tok(D), vec, vec, pl.BlockSpec(wr.shape, lambda b, i: (0, 0))],
        out_specs=[tok(D), tok(LANES), tok(LANES)],
        out_shape=[jax.ShapeDtypeStruct((B, T, D), BF16), jax.ShapeDtypeStruct((B, T, LANES), jnp.int32),
                   jax.ShapeDtypeStruct((B, T, LANES), F32)],
        compiler_params=_cparams("parallel", "parallel"),
        name="moe_router",
    )(x, shift, scale, wr)


def _expert_kernel(be_ref, bv_ref, x_ref, w1_ref, w3_ref, w2_ref, o_ref):
    blk = pl.program_id(0)
    f = pl.program_id(1)
    valid = bv_ref[blk] > 0

    @pl.when(f == 0)
    def _():
        o_ref[...] = jnp.zeros(o_ref.shape, F32)

    @pl.when(valid)
    def _():
        xb = x_ref[...]
        a = _dot(xb, w1_ref[0].astype(BF16))
        b = _dot(xb, w3_ref[0].astype(BF16))
        o_ref[...] += _dot((a * jax.nn.sigmoid(a) * b).astype(BF16), w2_ref[0].astype(BF16))


def expert_ffn(xg, blk_expert, blk_valid, w1, w3, w2):
    P, D = xg.shape
    F = w1.shape[2]
    tf = _tile(F, 512)
    nblk = P // MOE_ROWS
    grid_spec = pltpu.PrefetchScalarGridSpec(
        num_scalar_prefetch=2,
        grid=(nblk, F // tf),
        in_specs=[pl.BlockSpec((MOE_ROWS, D), lambda i, f, be, bv: (i, 0)),
                  pl.BlockSpec((1, D, tf), lambda i, f, be, bv: (be[i], 0, jnp.where(bv[i] > 0, f, 0))),
                  pl.BlockSpec((1, D, tf), lambda i, f, be, bv: (be[i], 0, jnp.where(bv[i] > 0, f, 0))),
                  pl.BlockSpec((1, tf, D), lambda i, f, be, bv: (be[i], jnp.where(bv[i] > 0, f, 0), 0))],
        out_specs=pl.BlockSpec((MOE_ROWS, D), lambda i, f, be, bv: (i, 0)),
    )
    return pl.pallas_call(
        _expert_kernel,
        grid_spec=grid_spec,
        out_shape=jax.ShapeDtypeStruct((P, D), F32),
        compiler_params=_cparams("parallel", "arbitrary"),
        name="expert_ffn",
    )(blk_expert, blk_valid, xg, w1, w3, w2)


def _combine_ln_kernel(x_ref, y0_ref, y1_ref, tg_ref, g_ref, lw_ref, lb_ref, o_ref, *, alpha):
    tg = tg_ref[0]
    y = tg[:, 0:1] * y0_ref[0] + tg[:, 1:2] * y1_ref[0]
    z = alpha * x_ref[0] + g_ref[0] * y
    o_ref[0] = _standardize(z, LN_EPS) * lw_ref[...] + lb_ref[...]


def combine_ln(x, y0, y1, tg, gate, ln_w, ln_b, alpha):
    B, T, D = x.shape
    tm = _tile(T, 512)
    tok = pl.BlockSpec((1, tm, D), lambda b, i: (b, i, 0))
    row = pl.BlockSpec((1, D), lambda b, i: (0, 0))
    return pl.pallas_call(
        functools.partial(_combine_ln_kernel, alpha=alpha),
        grid=(B, T // tm),
        in_specs=[tok, tok, tok, pl.BlockSpec((1, tm, LANES), lambda b, i: (b, i, 0)),
                  pl.BlockSpec((1, 1, D), lambda b, i: (b, 0, 0)), row, row],
        out_specs=tok,
        out_shape=jax.ShapeDtypeStruct((B, T, D), F32),
        compiler_params=_cparams("parallel", "parallel"),
        name="combine_ln",
    )(x, y0, y1, tg, gate, ln_w, ln_b)


def moe_ln(x, shift, scale, gate, ln_w, ln_b, router, w1, w3, w2, alpha):
    B, T, D = x.shape
    N = B * T
    E = router.shape[1]
    wr = jnp.zeros((D, LANES), F32).at[:, :E].set(router)
    h, ti, tg = moe_router(x, shift, scale, wr)
    top_i = ti.reshape(N, LANES)[:, :2]
    e_flat = top_i.reshape(-1)
    onehot = (e_flat[:, None] == jnp.arange(E, dtype=jnp.int32)[None, :]).astype(jnp.int32)
    rank = jnp.sum((jnp.cumsum(onehot, axis=0) - onehot) * onehot, axis=1)
    counts = jnp.sum(onehot, axis=0)
    padded = (counts + MOE_ROWS - 1) // MOE_ROWS * MOE_ROWS
    p_ends = jnp.cumsum(padded)
    p_starts = p_ends - padded
    dest = p_starts[e_flat] + rank
    P = -(-(2 * N) // MOE_ROWS) * MOE_ROWS + E * MOE_ROWS
    nblk = P // MOE_ROWS
    tok_flat = jnp.repeat(jnp.arange(N, dtype=jnp.int32), 2)
    slot_tok = jnp.zeros((P,), jnp.int32).at[dest].set(tok_flat)
    blk_start = jnp.arange(nblk, dtype=jnp.int32) * MOE_ROWS
    blk_expert = jnp.minimum(jnp.searchsorted(p_ends, blk_start, side="right"), E - 1).astype(jnp.int32)
    blk_valid = (blk_start < p_ends[-1]).astype(jnp.int32)
    xg = h.reshape(N, D)[slot_tok]
    yg = expert_ffn(xg, blk_expert, blk_valid, w1, w3, w2)
    d2 = dest.reshape(N, 2)
    y0 = yg[d2[:, 0]].reshape(B, T, D)
    y1 = yg[d2[:, 1]].reshape(B, T, D)
    return combine_ln(x, y0, y1, tg, gate, ln_w, ln_b, alpha)


def _rope_partner(w):
    half = ROPE_AXIS // 2
    idx = np.arange(QK_ROPE)
    first = (idx % ROPE_AXIS) < half
    src = np.where(first, idx + half, idx - half)
    sign = np.where(first, -1.0, 1.0).astype(np.float32)
    return w[:, src] * sign


def _layer_params(l, w_in, rw_conv, rw_w0, rw_w_up, rw_a0, rw_a_up, rw_g_up, rw_k_k, rw_k_a, rw_r_k, rw_gn_w,
                  rw_gn_b, mla_q_norm, mla_w_uq, mla_kv_norm, mla_w_ukv, w_out):
    D = w_in.shape[1]
    wi = w_in[l]
    o = np.cumsum([0, 3 * RW_WIDTH, G_RANK, LORA_RANK, LORA_RANK, LORA_RANK, LORA_RANK, FT_WIDTH, Q_RANK, KV_RANK, QK_ROPE])
    piece = lambda i: wi[:, o[i]:o[i + 1]]
    zpad = lambda n: jnp.zeros((D, n), F32)
    lora = [jnp.concatenate([piece(i), zpad(LANES - LORA_RANK)], axis=1) for i in (2, 3, 4, 5)]
    w_rw = jnp.concatenate([piece(0), piece(1)] + lora, axis=1)
    kr = piece(9)
    w_ckv = jnp.concatenate([piece(8), zpad(QK_NOPE), kr, _rope_partner(kr)], axis=1)
    pad_rows = lambda a: jnp.concatenate([a, jnp.zeros((a.shape[0], LANES - LORA_RANK, a.shape[2]), F32)], axis=1)
    head = jnp.arange(RW_WIDTH) // RW_HEAD
    uq = mla_w_uq[l].reshape(Q_RANK, MLA_HEADS, QK_NOPE + QK_ROPE)
    uq_rope = uq[:, :, QK_NOPE:]
    uq_partner = _rope_partner(uq_rope.reshape(Q_RANK * MLA_HEADS, QK_ROPE)).reshape(Q_RANK, MLA_HEADS, QK_ROPE)
    w_q = jnp.concatenate([uq, uq_partner], axis=2).reshape(Q_RANK, MLA_HEADS * HEAD_SLAB)
    ukv = mla_w_ukv[l].reshape(KV_RANK, MLA_HEADS, QK_NOPE + V_HEAD)
    w_k = jnp.concatenate([ukv[:, :, :QK_NOPE], jnp.zeros((KV_RANK, MLA_HEADS, HEAD_SLAB - QK_NOPE), F32)],
                          axis=2).reshape(KV_RANK, MLA_HEADS * HEAD_SLAB)
    w_v = jnp.concatenate([ukv[:, :, QK_NOPE:], jnp.zeros((KV_RANK, MLA_HEADS, ONES_ROWS), F32)],
                          axis=2).reshape(KV_RANK, MLA_HEADS * V_ROWS)
    return {
        "w_in": [w.astype(BF16) for w in (w_rw, piece(6), piece(7), w_ckv)],
        "rw": {"conv": rw_conv[l], "k_k": rw_k_k[l][None], "k_a": rw_k_a[l][None], "r_k": rw_r_k[l].reshape(1, RW_WIDTH),
               "w0": rw_w0[l], "a0": rw_a0[l], "w_up": pad_rows(rw_w_up[l]).astype(BF16), "a_up": pad_rows(rw_a_up[l]).astype(BF16),
               "g_up": rw_g_up[l].astype(BF16), "gn_w": rw_gn_w[l][None], "gn_b": rw_gn_b[l][None],
               "E": (head[:, None] == head[None, :]).astype(BF16)},
        "q_norm": mla_q_norm[l][None], "w_qt": w_q.T.astype(BF16),
        "kv_norm": mla_kv_norm[l][None], "w_k": w_k.astype(BF16), "w_vt": w_v.T.astype(BF16),
        "w_out": w_out[l].astype(BF16),
    }


def _rope_tables(T, use_rope):
    ones = jnp.ones((T, QK_NOPE), F32)
    zeros = jnp.zeros((T, QK_NOPE), F32)
    zpad = jnp.zeros((T, HEAD_SLAB - QK_NOPE - QK_ROPE), F32)
    if use_rope:
        row = jnp.repeat(jnp.arange(T // GRID_W), GRID_W).astype(F32)
        col = (jnp.arange(T) % GRID_W).astype(F32)
        inv = ROPE_THETA ** (-jnp.arange(0, ROPE_AXIS, 2, dtype=F32) / ROPE_AXIS)
        ang = jnp.stack([row[:, None] * inv, col[:, None] * inv], axis=1)
        ang = jnp.broadcast_to(ang[:, :, None, :], (T, 2, 2, ROPE_AXIS // 2)).reshape(T, QK_ROPE)
        cos, sin = jnp.cos(ang), jnp.sin(ang)
    else:
        cos, sin = jnp.ones((T, QK_ROPE), F32), jnp.zeros((T, QK_ROPE), F32)
    return jnp.concatenate([ones, cos, zpad], axis=1), jnp.concatenate([zeros, sin, zpad], axis=1)


def _mixer(h_pieces_lat, h_pieces_ctx, p, tabs_lat, tabs_ctx, emit_ctx):
    rw_l, ft_l, cq_l, ckv_l = h_pieces_lat
    rw_c, ft_c, cq_c, ckv_c = h_pieces_ctx
    rwo_l, rwo_c = rwkv_branch(rw_l, rw_c, p["rw"], emit_ctx)
    fto_l = fourier_mixer(ft_l)
    q_l = q_projection(cq_l, p["q_norm"], p["w_qt"], *(t.T for t in tabs_lat))
    k_l, vt_l = kv_projection(ckv_l, p["kv_norm"], p["w_k"], p["w_vt"], *tabs_lat)
    k_c, vt_c = kv_projection(ckv_c, p["kv_norm"], p["w_k"], p["w_vt"], *tabs_ctx)
    att_l = attention(q_l, jnp.concatenate([k_l, k_c], axis=1), jnp.concatenate([vt_l, vt_c], axis=2))
    out_c = None
    if emit_ctx:
        fto_c = fourier_mixer(ft_c)
        q_c = q_projection(cq_c, p["q_norm"], p["w_qt"], *(t.T for t in tabs_ctx))
        att_c = attention(q_c, k_c, vt_c)
        out_c = (rwo_c, fto_c, att_c)
    return (rwo_l, fto_l, att_l), out_c


def kernel(x, c, ctx, c_ctx, ada_w, ada_b, w_in, rw_conv, rw_w0, rw_w_up, rw_a0, rw_a_up, rw_g_up, rw_k_k, rw_k_a,
           rw_r_k, rw_gn_w, rw_gn_b, mla_q_norm, mla_w_uq, mla_kv_norm, mla_w_ukv, w_out, ln1_w, ln1_b, ln2_w, ln2_b,
           ffn_w1, ffn_w3, ffn_w2, moe_router, moe_w1, moe_w3, moe_w2):
    B, T, D = x.shape
    Tc = ctx.shape[1]
    depth = w_in.shape[0]
    alpha = (2 * depth) ** 0.25
    assert B + 1 <= SUBLANES
    cc = jnp.zeros((SUBLANES, D), F32).at[:B].set(c).at[B].set(c_ctx)
    ada = ada_vectors(cc, ada_w, ada_b)
    tabs_lat = _rope_tables(T, True)
    tabs_ctx = _rope_tables(Tc, False)
    for l in range(depth):
        last = l == depth - 1
        p = _layer_params(l, w_in, rw_conv, rw_w0, rw_w_up, rw_a0, rw_a_up, rw_g_up, rw_k_k, rw_k_a, rw_r_k,
                          rw_gn_w, rw_gn_b, mla_q_norm, mla_w_uq, mla_kv_norm, mla_w_ukv, w_out)
        mods = ada[l].reshape(SUBLANES, 6, D)
        lat = [mods[:B, j][:, None, :] for j in range(6)]
        cx = [jnp.broadcast_to(mods[B, j][None, None, :], (B, 1, D)) for j in range(6)]
        sh_m, sc_m, g_m, sh_f, sc_f, g_f = lat
        csh_m, csc_m, cg_m, csh_f, csc_f, cg_f = cx
        ln1 = (ln1_w[l][None], ln1_b[l][None])
        ln2 = (ln2_w[l][None], ln2_b[l][None])

        pieces_l = in_projection(x, sh_m, sc_m, p["w_in"])
        pieces_c = in_projection(ctx, csh_m, csc_m, p["w_in"])
        mix_l, mix_c = _mixer(pieces_l, pieces_c, p, tabs_lat, tabs_ctx, not last)
        x = out_projection_ln(*mix_l, x, g_m, *ln1, p["w_out"], alpha)
        i = l // 2
        if l % 2 == 0:
            dense = (ffn_w1[i].astype(BF16), ffn_w3[i].astype(BF16), ffn_w2[i].astype(BF16))
            x = ffn_ln(x, sh_f, sc_f, g_f, *ln2, *dense, alpha)
        else:
            x = moe_ln(x, sh_f, sc_f, g_f, *ln2, moe_router[i], moe_w1[i], moe_w3[i], moe_w2[i], alpha)
        if not last:
            ctx = out_projection_ln(*mix_c, ctx, cg_m, *ln1, p["w_out"], alpha)
            if l % 2 == 0:
                ctx = ffn_ln(ctx, csh_f, csc_f, cg_f, *ln2, *dense, alpha)
            else:
                ctx = moe_ln(ctx, csh_f, csc_f, cg_f, *ln2, moe_router[i], moe_w1[i], moe_w3[i], moe_w2[i], alpha)
    return x
```
